```python
import jax, jax.numpy as jnp
from jax import lax
import numpy as np

D_MODEL = 2048
BATCH = 8
SEQ = 4096
DEPTH = 1
DEC_BATCH = 16
DEC_SEQ = 2048
PAST_LEN = 128

N_META = 16
GRID_W = 64
Q_BLOCK = 128
N_HEADS = 16
N_KV_HEADS = 4
HEAD_DIM = D_MODEL // N_HEADS
Q_PER_KV = N_HEADS // N_KV_HEADS
Q_DIM = N_HEADS * HEAD_DIM
KV_DIM = N_KV_HEADS * HEAD_DIM
ROPE_AXIS_DIM = HEAD_DIM // 2
ROPE_HALF = ROPE_AXIS_DIM // 2
ROPE_THETA = 10000.0
N_FOURIER_GROUPS = 4
FOURIER_GROUP_DIM = D_MODEL // 8
FOURIER_DIM = N_FOURIER_GROUPS * FOURIER_GROUP_DIM
IN_DIM = FOURIER_DIM + Q_DIM + 2 * KV_DIM + 2 * D_MODEL
IN_SPLITS = (FOURIER_DIM, FOURIER_DIM + Q_DIM, FOURIER_DIM + Q_DIM + KV_DIM,
             FOURIER_DIM + Q_DIM + 2 * KV_DIM, FOURIER_DIM + Q_DIM + 2 * KV_DIM + D_MODEL)
N_EXPERT_GROUPS = 4
EXPERTS_PER_GROUP = 8
N_EXPERTS = N_EXPERT_GROUPS * EXPERTS_PER_GROUP
TOP_K = 2
D_EXPERT = D_MODEL // 2
MOE_BLOCK = 128
NORM_EPS = 1e-6

kernel_name = 'hybrid_gqa_fnet_hmoe_encoder'


def rms_norm(x, g):
    xf = x.astype(jnp.float32)
    y = xf * lax.rsqrt(jnp.mean(xf * xf, axis=-1, keepdims=True) + NORM_EPS)
    return (y * g.astype(jnp.float32)).astype(x.dtype)


def axial_rope_tables(n_tokens):
    rows = n_tokens // GRID_W
    row = jnp.repeat(jnp.arange(rows, dtype=jnp.float32), GRID_W)
    col = jnp.tile(jnp.arange(GRID_W, dtype=jnp.float32), rows)
    inv_freq = ROPE_THETA ** (-jnp.arange(ROPE_HALF, dtype=jnp.float32) / ROPE_HALF)
    ang = jnp.stack([row[:, None] * inv_freq, col[:, None] * inv_freq], axis=1)
    ang = jnp.concatenate([jnp.zeros((N_META, 2, ROPE_HALF), jnp.float32), ang], axis=0)
    return jnp.cos(ang), jnp.sin(ang)


def apply_axial_rope(x, cos, sin):
    B, L, H, _ = x.shape
    xs = x.astype(jnp.float32).reshape(B, L, H, 2, 2, ROPE_HALF)
    x1, x2 = xs[..., 0, :], xs[..., 1, :]
    c, s = cos[:, None], sin[:, None]
    out = jnp.stack([x1 * c - x2 * s, x1 * s + x2 * c], axis=-2)
    return out.reshape(B, L, H, HEAD_DIM).astype(x.dtype)


def gqa_attention(q, k, v):
    B, L, _, _ = q.shape
    n_real = L - N_META
    n_blk = n_real // Q_BLOCK
    scale = HEAD_DIM ** -0.5
    qg = q.reshape(B, L, N_KV_HEADS, Q_PER_KV, HEAD_DIM)

    def attend(qb):
        s = jnp.einsum('bqkgd,bskd->bkgqs', qb, k).astype(jnp.float32) * scale
        p = jax.nn.softmax(s, axis=-1).astype(v.dtype)
        return jnp.einsum('bkgqs,bskd->bqkgd', p, v)

    meta_out = attend(qg[:, :N_META]).reshape(B, N_META, Q_DIM)
    q_blocks = qg[:, N_META:].reshape(B, n_blk, Q_BLOCK, N_KV_HEADS, Q_PER_KV, HEAD_DIM)
    q_blocks = jnp.moveaxis(q_blocks, 1, 0)
    real_out = lax.map(attend, q_blocks)
    real_out = jnp.moveaxis(real_out, 0, 1).reshape(B, n_real, Q_DIM)
    return jnp.concatenate([meta_out, real_out], axis=1)


def fourier_mix(u):
    B, L, _ = u.shape
    ug = u.astype(jnp.float32).reshape(B, L, N_FOURIER_GROUPS, FOURIER_GROUP_DIM)
    z = jnp.fft.fftn(ug, axes=(1, 3), norm='ortho').real
    return z.reshape(B, L, FOURIER_DIM).astype(u.dtype)


def token_mixer(h, cos, sin, w_in, q_gain, k_gain, w_attn_o, w_fourier_o, w_out):
    B, L, _ = h.shape
    proj = h @ w_in
    u_f, q, k, v, g_a, g_f = jnp.split(proj, IN_SPLITS, axis=-1)
    q = apply_axial_rope(rms_norm(q.reshape(B, L, N_HEADS, HEAD_DIM), q_gain), cos, sin)
    k = apply_axial_rope(rms_norm(k.reshape(B, L, N_KV_HEADS, HEAD_DIM), k_gain), cos, sin)
    v = v.reshape(B, L, N_KV_HEADS, HEAD_DIM)
    a_branch = gqa_attention(q, k, v) @ w_attn_o
    f_branch = fourier_mix(u_f) @ w_fourier_o
    merged = jax.nn.sigmoid(g_a) * a_branch + jax.nn.sigmoid(g_f) * f_branch
    return merged @ w_out


def hierarchical_moe(x, w_rg, w_re, w_gate, w_up, w_down):
    T, D = x.shape
    p_group = jax.nn.softmax((x @ w_rg).astype(jnp.float32), axis=-1)
    p_g, g_idx = lax.top_k(p_group, 1)
    logits_e = (x @ w_re).astype(jnp.float32).reshape(T, N_EXPERT_GROUPS, EXPERTS_PER_GROUP)
    logits_in = jnp.take_along_axis(logits_e, g_idx[:, :, None], axis=1)[:, 0]
    p_e, e_local = lax.top_k(jax.nn.softmax(logits_in, axis=-1), TOP_K)
    p_e = p_e / jnp.sum(p_e, axis=-1, keepdims=True)
    combine_w = (p_g * p_e).astype(x.dtype)
    expert_id = (g_idx * EXPERTS_PER_GROUP + e_local).astype(jnp.int32)

    A = T * TOP_K
    flat_e = expert_id.reshape(A)
    order = jnp.argsort(flat_e).astype(jnp.int32)
    sorted_e = flat_e[order]
    counts = jnp.bincount(flat_e, length=N_EXPERTS).astype(jnp.int32)
    start = jnp.cumsum(counts) - counts
    padded = (counts + MOE_BLOCK - 1) // MOE_BLOCK * MOE_BLOCK
    pad_end = jnp.cumsum(padded)
    pad_start = pad_end - padded
    dest = (pad_start[sorted_e] + jnp.arange(A, dtype=jnp.int32) - start[sorted_e]).astype(jnp.int32)
    n_blocks = -(-A // MOE_BLOCK) + N_EXPERTS
    P = n_blocks * MOE_BLOCK
    token_of_slot = jnp.full((P,), T, jnp.int32).at[dest].set(order // TOP_K)
    block_expert = jnp.minimum(
        jnp.searchsorted(pad_end, jnp.arange(n_blocks, dtype=jnp.int32) * MOE_BLOCK, side='right'),
        N_EXPERTS - 1)
    x_pad = jnp.concatenate([x, jnp.zeros((1, D), x.dtype)], axis=0)
    xb = x_pad[token_of_slot].reshape(n_blocks, MOE_BLOCK, D)

    def run_block(args):
        xs, e = args
        hid = jax.nn.silu(xs @ w_gate[e]) * (xs @ w_up[e])
        return hid @ w_down[e]

    yb = lax.map(run_block, (xb, block_expert)).reshape(P, D)
    slot_of_assign = jnp.zeros((A,), jnp.int32).at[order].set(dest).reshape(T, TOP_K)
    return jnp.einsum('tkd,tk->td', yb[slot_of_assign], combine_w)


def encoder_forward(x, meta_tokens, mix_norm, w_in, q_gain, k_gain, w_attn_o, w_fourier_o, w_out,
                    moe_norm, w_router_group, w_router_expert, w_expert_gate, w_expert_up,
                    w_expert_down, final_norm):
    B, N, D = x.shape
    meta = jnp.broadcast_to(meta_tokens[None].astype(x.dtype), (B, N_META, D))
    h = jnp.concatenate([meta, x], axis=1)
    L = h.shape[1]
    cos, sin = axial_rope_tables(N)
    for l in range(DEPTH):
        h = h + token_mixer(rms_norm(h, mix_norm[l]), cos, sin, w_in[l], q_gain[l], k_gain[l],
                            w_attn_o[l], w_fourier_o[l], w_out[l])
        hn = rms_norm(h, moe_norm[l]).reshape(B * L, D)
        h = h + hierarchical_moe(hn, w_router_group[l], w_router_expert[l], w_expert_gate[l],
                                 w_expert_up[l], w_expert_down[l]).reshape(B, L, D)
    return rms_norm(h, final_norm)[:, N_META:]


def setup_inputs(seed: int = 0) -> dict:
    key = jax.random.key(seed)
    ks = jax.random.split(key, 20)
    f32 = jnp.float32

    def nrm(k, shape, scale):
        return jax.random.normal(k, shape, f32) * scale

    def gain(k, shape):
        return 1.0 + 0.02 * jax.random.normal(k, shape, f32)

    return {
        'x_prompt': nrm(ks[0], (BATCH, SEQ, D_MODEL), 1.0),
        'x_sample': nrm(ks[1], (DEC_BATCH, DEC_SEQ, D_MODEL), 1.0),
        'meta_tokens': nrm(ks[2], (N_META, D_MODEL), 1.0),
        'mix_norm': gain(ks[3], (DEPTH, D_MODEL)),
        'w_in': nrm(ks[4], (DEPTH, D_MODEL, IN_DIM), D_MODEL ** -0.5),
        'q_gain': gain(ks[5], (DEPTH, HEAD_DIM)),
        'k_gain': gain(ks[6], (DEPTH, HEAD_DIM)),
        'w_attn_o': nrm(ks[7], (DEPTH, Q_DIM, D_MODEL), Q_DIM ** -0.5),
        'w_fourier_o': nrm(ks[8], (DEPTH, FOURIER_DIM, D_MODEL), FOURIER_DIM ** -0.5),
        'w_out': nrm(ks[9], (DEPTH, D_MODEL, D_MODEL), D_MODEL ** -0.5),
        'moe_norm': gain(ks[10], (DEPTH, D_MODEL)),
        'w_router_group': nrm(ks[11], (DEPTH, D_MODEL, N_EXPERT_GROUPS), D_MODEL ** -0.5),
        'w_router_expert': nrm(ks[12], (DEPTH, D_MODEL, N_EXPERTS), D_MODEL ** -0.5),
        'w_expert_gate': nrm(ks[13], (DEPTH, N_EXPERTS, D_MODEL, D_EXPERT), D_MODEL ** -0.5),
        'w_expert_up': nrm(ks[14], (DEPTH, N_EXPERTS, D_MODEL, D_EXPERT), D_MODEL ** -0.5),
        'w_expert_down': nrm(ks[15], (DEPTH, N_EXPERTS, D_EXPERT, D_MODEL), D_EXPERT ** -0.5),
        'final_norm': gain(ks[16], (D_MODEL,)),
    }


def reference(x_prompt, x_sample, meta_tokens, mix_norm, w_in, q_gain, k_gain, w_attn_o,
              w_fourier_o, w_out, moe_norm, w_router_group, w_router_expert, w_expert_gate,
              w_expert_up, w_expert_down, final_norm):
    y_prompt = encoder_forward(x_prompt, meta_tokens, mix_norm, w_in, q_gain, k_gain, w_attn_o,
                               w_fourier_o, w_out, moe_norm, w_router_group, w_router_expert,
                               w_expert_gate, w_expert_up, w_expert_down, final_norm)
    y_sample = encoder_forward(x_sample, meta_tokens, mix_norm, w_in, q_gain, k_gain, w_attn_o,
                               w_fourier_o, w_out, moe_norm, w_router_group, w_router_expert,
                               w_expert_gate, w_expert_up, w_expert_down, final_norm)
    return (y_prompt, y_sample)
```

```python
import functools
import math

import jax
import jax.numpy as jnp
from jax import lax
from jax.experimental import pallas as pl
from jax.experimental.pallas import tpu as pltpu

F32, BF16, I32, U32 = jnp.float32, jnp.bfloat16, jnp.int32, jnp.uint32

GRID_W = 64
HEAD_DIM = 128
ROPE_HALF = 32
Q_PER_KV = 4
N_FOURIER_GROUPS = 4
ROPE_THETA = 10000.0
NORM_EPS = 1e-6
TOP_K = 2
EXPERTS_PER_GROUP = 8
LOG2E = 1.4426950408889634

VMEM_LIMIT_BYTES = 56 * 1024 * 1024


def _params(n_axes):
    return pltpu.CompilerParams(dimension_semantics=("arbitrary",) * n_axes,
                                vmem_limit_bytes=VMEM_LIMIT_BYTES)


def _resident(shape):
    return pl.BlockSpec(shape, lambda *_: (0,) * len(shape), pipeline_mode=pl.Buffered(1))


def _pack_bf16_pair(lo, hi):
    lo_bits = lax.bitcast_convert_type(lo.astype(BF16).astype(F32), U32) >> 16
    hi_bits = lax.bitcast_convert_type(hi.astype(BF16).astype(F32), U32) & jnp.uint32(0xFFFF0000)
    return lo_bits | hi_bits


def _unpack_bf16_pair(w):
    lo = lax.bitcast_convert_type(w << 16, F32)
    hi = lax.bitcast_convert_type(w & jnp.uint32(0xFFFF0000), F32)
    return lo, hi


def _rope_heads(acc, gain, cos, sin, out_ref):
    lo_half = (lax.broadcasted_iota(I32, cos.shape, 1) & ROPE_HALF) == 0
    for h in range(acc.shape[1] // HEAD_DIM):
        a = acc[:, h * HEAD_DIM:(h + 1) * HEAD_DIM]
        y = a * lax.rsqrt(jnp.mean(a * a, axis=-1, keepdims=True) + NORM_EPS) * gain
        partner = jnp.where(lo_half, pltpu.roll(y, HEAD_DIM - ROPE_HALF, 1), pltpu.roll(y, ROPE_HALF, 1))
        out_ref[:, h * HEAD_DIM:(h + 1) * HEAD_DIM] = (y * cos + partner * sin).astype(out_ref.dtype)


def _inproj_body(x_ref, gain_ref, w_ref, qg_ref, kg_ref, cos_ref, sin_ref, cdft_ref,
                 uc_ref, us_ref, q_ref, k_ref, v_ref, sg_ref, hn_ref, *, kv_dim, cg):
    n = pl.program_id(1)

    @pl.when(n == 0)
    def _norm():
        x = x_ref[...]
        y = x * lax.rsqrt(jnp.mean(x * x, axis=-1, keepdims=True) + NORM_EPS)
        hn_ref[...] = (y * gain_ref[...]).astype(BF16)

    acc = jnp.dot(hn_ref[...], w_ref[...], preferred_element_type=F32)

    @pl.when(n == 0)
    def _fourier():
        uf = acc.astype(BF16)
        for g in range(N_FOURIER_GROUPS):
            r = jnp.dot(uf[:, g * cg:(g + 1) * cg], cdft_ref[...], preferred_element_type=F32)
            uc_ref[:, g * cg:(g + 1) * cg] = r[:, :cg].astype(BF16)
            us_ref[:, g * cg:(g + 1) * cg] = r[:, cg:].astype(BF16)

    @pl.when((n == 1) | (n == 2))
    def _q():
        _rope_heads(acc, qg_ref[...], cos_ref[...], sin_ref[...], q_ref)

    @pl.when(n == 3)
    def _kv():
        _rope_heads(acc[:, :kv_dim], kg_ref[...], cos_ref[...], sin_ref[...], k_ref)
        v_ref[...] = acc[:, kv_dim:].astype(BF16)

    @pl.when(n >= 4)
    def _gates():
        sg_ref[...] = jax.nn.sigmoid(acc).astype(BF16)


def _inproj(x, gain, w_in, qg, kg, cos, sin, cdft, *, tm, pos_block):
    t, d = x.shape
    tn = d // 2
    f_dim, q_dim, kv_dim = d // 2, d, d // 4
    cg = f_dim // N_FOURIER_GROUPS
    assert w_in.shape == (d, 8 * tn) and t % tm == 0
    body = functools.partial(_inproj_body, kv_dim=kv_dim, cg=cg)
    row = lambda i, n: (i, 0)
    const = lambda i, n: (0, 0)
    return pl.pallas_call(
        body,
        grid=(t // tm, 8),
        in_specs=[
            pl.BlockSpec((tm, d), row),
            pl.BlockSpec((1, d), const),
            pl.BlockSpec((d, tn), lambda i, n: (0, n)),
            pl.BlockSpec((1, HEAD_DIM), const),
            pl.BlockSpec((1, HEAD_DIM), const),
            pl.BlockSpec((tm, HEAD_DIM), lambda i, n: (pos_block(i), 0)),
            pl.BlockSpec((tm, HEAD_DIM), lambda i, n: (pos_block(i), 0)),
            pl.BlockSpec((cg, 2 * cg), const),
        ],
        out_specs=[
            pl.BlockSpec((tm, f_dim), row),
            pl.BlockSpec((tm, f_dim), row),
            pl.BlockSpec((tm, tn), lambda i, n: (i, jnp.clip(n - 1, 0, 1))),
            pl.BlockSpec((tm, kv_dim), row),
            pl.BlockSpec((tm, kv_dim), row),
            pl.BlockSpec((tm, tn), lambda i, n: (i, jnp.clip(n - 4, 0, 3))),
        ],
        out_shape=[
            jax.ShapeDtypeStruct((t, f_dim), BF16),
            jax.ShapeDtypeStruct((t, f_dim), BF16),
            jax.ShapeDtypeStruct((t, q_dim), BF16),
            jax.ShapeDtypeStruct((t, kv_dim), BF16),
            jax.ShapeDtypeStruct((t, kv_dim), BF16),
            jax.ShapeDtypeStruct((t, 2 * d), BF16),
        ],
        scratch_shapes=[pltpu.VMEM((tm, d), BF16)],
        compiler_params=_params(2),
        name="inproj",
    )(x, gain, w_in, qg, kg, cos, sin, cdft)


def _attn_body(q_ref, k_ref, v_ref, km_ref, vm_ref, *rest):
    o_ref = rest[-1]
    k, v, km, vm = k_ref[...], v_ref[...], km_ref[...], vm_ref[...]
    nt = (((1,), (1,)), ((), ()))
    for h in range(Q_PER_KV):
        qh = q_ref[:, h * HEAD_DIM:(h + 1) * HEAD_DIM]
        s = lax.dot_general(qh, k, nt, preferred_element_type=F32)
        sm = lax.dot_general(qh, km, nt, preferred_element_type=F32)
        m = jnp.maximum(jnp.max(s, axis=-1, keepdims=True), jnp.max(sm, axis=-1, keepdims=True))
        p = jnp.exp2(s - m)
        pm = jnp.exp2(sm - m)
        denom = jnp.sum(p, axis=-1, keepdims=True) + jnp.sum(pm, axis=-1, keepdims=True)
        o = jnp.dot(p.astype(BF16), v, preferred_element_type=F32)
        o = o + jnp.dot(pm.astype(BF16), vm, preferred_element_type=F32)
        o_ref[:, h * HEAD_DIM:(h + 1) * HEAD_DIM] = (o / denom).astype(o_ref.dtype)


def _attention(q, k, v, km, vm, prev, *, row_off, n_seq, n_tok, tq):
    t, q_dim = q.shape
    n_kv = k.shape[1] // HEAD_DIM
    n_meta = km.shape[0]
    gw = Q_PER_KV * HEAD_DIM
    assert row_off % n_tok == 0 and n_tok % tq == 0
    qmap = lambda b, kh, qi: ((row_off + b * n_tok) // tq + qi, kh)
    kmap = lambda b, kh, qi: (row_off // n_tok + b, kh)
    mmap = lambda b, kh, qi: (0, kh)
    in_specs = [
        pl.BlockSpec((tq, gw), qmap),
        pl.BlockSpec((n_tok, HEAD_DIM), kmap),
        pl.BlockSpec((n_tok, HEAD_DIM), kmap),
        pl.BlockSpec((n_meta, HEAD_DIM), mmap),
        pl.BlockSpec((n_meta, HEAD_DIM), mmap),
    ]
    args = [q, k, v, km, vm]
    aliases = {}
    if prev is not None:
        in_specs.append(pl.BlockSpec(memory_space=pl.ANY))
        args.append(prev)
        aliases = {5: 0}
    return pl.pallas_call(
        _attn_body,
        grid=(n_seq, n_kv, n_tok // tq),
        in_specs=in_specs,
        out_specs=pl.BlockSpec((tq, gw), qmap),
        out_shape=jax.ShapeDtypeStruct((t, q_dim), BF16),
        input_output_aliases=aliases,
        compiler_params=_params(3),
        name="attention",
    )(*args)


def _fourier_body(c_ref, ns_ref, cm_ref, nsm_ref, uc_ref, us_ref, ucm_ref, usm_ref, *rest):
    z_ref = rest[-1]
    z = jnp.dot(c_ref[...], uc_ref[...], preferred_element_type=F32)
    z = z + jnp.dot(ns_ref[...], us_ref[...], preferred_element_type=F32)
    z = z + jnp.dot(cm_ref[...], ucm_ref[...], preferred_element_type=F32)
    z = z + jnp.dot(nsm_ref[...], usm_ref[...], preferred_element_type=F32)
    z_ref[...] = z.astype(z_ref.dtype)


def _fourier(uc, us, mats, ucm, usm, prev, *, row_off, n_seq, n_tok, tl, tf):
    cmat, nsmat, cmeta, nsmeta = mats
    t, f_dim = uc.shape
    n_meta = ucm.shape[0]
    assert row_off % n_tok == 0 and n_tok % tl == 0 and f_dim % tf == 0
    cmap = lambda b, j, i: (i, 0)
    umap = lambda b, j, i: (row_off // n_tok + b, j)
    mmap = lambda b, j, i: (0, j)
    in_specs = [
        pl.BlockSpec((tl, n_tok), cmap),
        pl.BlockSpec((tl, n_tok), cmap),
        pl.BlockSpec((tl, n_meta), cmap),
        pl.BlockSpec((tl, n_meta), cmap),
        pl.BlockSpec((n_tok, tf), umap),
        pl.BlockSpec((n_tok, tf), umap),
        pl.BlockSpec((n_meta, tf), mmap),
        pl.BlockSpec((n_meta, tf), mmap),
    ]
    args = [cmat, nsmat, cmeta, nsmeta, uc, us, ucm, usm]
    aliases = {}
    if prev is not None:
        in_specs.append(pl.BlockSpec(memory_space=pl.ANY))
        args.append(prev)
        aliases = {8: 0}
    return pl.pallas_call(
        _fourier_body,
        grid=(n_seq, f_dim // tf, n_tok // tl),
        in_specs=in_specs,
        out_specs=pl.BlockSpec((tl, tf), lambda b, j, i: ((row_off + b * n_tok) // tl + i, j)),
        out_shape=jax.ShapeDtypeStruct((t, f_dim), BF16),
        input_output_aliases=aliases,
        compiler_params=_params(3),
        name="fourier",
    )(*args)


def _merge_body(a_ref, z_ref, sga_ref, sgf_ref, wa_ref, wf_ref, o_ref):
    a = jnp.dot(a_ref[...], wa_ref[...], preferred_element_type=F32)
    f = jnp.dot(z_ref[...], wf_ref[...], preferred_element_type=F32)
    o_ref[...] = (sga_ref[...].astype(F32) * a + sgf_ref[...].astype(F32) * f).astype(o_ref.dtype)


def _merge(attn, z, sg, wa, wf, *, tm):
    t, d = attn.shape
    f_dim = z.shape[1]
    return pl.pallas_call(
        _merge_body,
        grid=(t // tm,),
        in_specs=[
            pl.BlockSpec((tm, d), lambda i: (i, 0)),
            pl.BlockSpec((tm, f_dim), lambda i: (i, 0)),
            pl.BlockSpec((tm, d), lambda i: (i, 0)),
            pl.BlockSpec((tm, d), lambda i: (i, 1)),
            _resident((d, d)),
            _resident((f_dim, d)),
        ],
        out_specs=pl.BlockSpec((tm, d), lambda i: (i, 0)),
        out_shape=jax.ShapeDtypeStruct((t, d), BF16),
        compiler_params=_params(1),
        name="merge",
    )(attn, z, sg, sg, wa, wf)


def _route_body(m_ref, x_ref, wo_ref, gain_ref, wr_ref, tri_ref,
                hmid_ref, hn_ref, eid_ref, cw_ref, rank_ref, cnt_ref, *, n_groups):
    i = pl.program_id(0)
    tm, d = x_ref.shape
    dh = d // 2
    n_exp = cnt_ref.shape[0]

    @pl.when(i == 0)
    def _init():
        cnt_ref[...] = jnp.zeros_like(cnt_ref)

    h = x_ref[...] + jnp.dot(m_ref[...], wo_ref[...], preferred_element_type=F32)
    hmid_ref[...] = h
    hn = h * lax.rsqrt(jnp.mean(h * h, axis=-1, keepdims=True) + NORM_EPS) * gain_ref[...]
    hb = hn.astype(BF16)
    hn_ref[...] = _pack_bf16_pair(hn[:, :dh], hn[:, dh:])

    lt = lax.dot_general(wr_ref[...], hb, (((1,), (1,)), ((), ())), preferred_element_type=F32)
    row8 = lax.broadcasted_iota(I32, (EXPERTS_PER_GROUP, tm), 0)
    neg = jnp.float32(-jnp.inf)
    lg = jnp.where(row8 < n_groups, lt[0:8], neg)
    gmax = jnp.max(lg, axis=0, keepdims=True)
    gidx = jnp.min(jnp.where(lg == gmax, row8, 8), axis=0, keepdims=True)
    p_g = 1.0 / jnp.sum(jnp.exp(lg - gmax), axis=0, keepdims=True)
    sel = lt[8:8 + EXPERTS_PER_GROUP]
    for g in range(1, n_groups):
        sel = jnp.where(gidx == g, lt[8 + g * EXPERTS_PER_GROUP:8 + (g + 1) * EXPERTS_PER_GROUP], sel)
    m1 = jnp.max(sel, axis=0, keepdims=True)
    i1 = jnp.min(jnp.where(sel == m1, row8, 8), axis=0, keepdims=True)
    sel2 = jnp.where(row8 == i1, neg, sel)
    m2 = jnp.max(sel2, axis=0, keepdims=True)
    i2 = jnp.min(jnp.where(sel2 == m2, row8, 8), axis=0, keepdims=True)
    e21 = jnp.exp(m2 - m1)
    p1 = 1.0 / (1.0 + e21)
    p2 = e21 * p1
    e1 = gidx * EXPERTS_PER_GROUP + i1
    e2 = gidx * EXPERTS_PER_GROUP + i2
    eid_ref[0, 0:1, :] = e1
    eid_ref[0, 1:2, :] = e2
    cw_ref[0, 0:1, :] = p_g * p1
    cw_ref[0, 1:2, :] = p_g * p2

    row_e = lax.broadcasted_iota(I32, (n_exp, tm), 0)
    oh1 = (row_e == e1)
    oh2 = (row_e == e2)
    cs1 = jnp.dot(oh1.astype(BF16), tri_ref[...], preferred_element_type=F32)
    cs2 = jnp.dot(oh2.astype(BF16), tri_ref[...], preferred_element_type=F32)
    oh1f, oh2f = oh1.astype(F32), oh2.astype(F32)
    c1 = jnp.sum(oh1f, axis=1, keepdims=True)
    c2 = jnp.sum(oh2f, axis=1, keepdims=True)
    base = cnt_ref[:, 0:1]
    rank_ref[0, 0:1, :] = jnp.sum(oh1f * (base + cs1), axis=0, keepdims=True).astype(I32)
    rank_ref[0, 1:2, :] = jnp.sum(oh2f * (base + c1 + cs2), axis=0, keepdims=True).astype(I32)
    cnt_ref[...] = cnt_ref[...] + (c1 + c2)


def _outproj_route(merged, x, wout, gain, wr_t, tri, *, tm, n_groups, n_exp):
    t, d = x.shape
    nt = t // tm
    tok3 = lambda i: (i, 0, 0)
    body = functools.partial(_route_body, n_groups=n_groups)
    return pl.pallas_call(
        body,
        grid=(nt,),
        in_specs=[
            pl.BlockSpec((tm, d), lambda i: (i, 0)),
            pl.BlockSpec((tm, d), lambda i: (i, 0)),
            _resident((d, d)),
            pl.BlockSpec((1, d), lambda i: (0, 0)),
            _resident(wr_t.shape),
            _resident((tm, tm)),
        ],
        out_specs=[
            pl.BlockSpec((tm, d), lambda i: (i, 0)),
            pl.BlockSpec((tm, d // 2), lambda i: (i, 0)),
            pl.BlockSpec((1, TOP_K, tm), tok3),
            pl.BlockSpec((1, TOP_K, tm), tok3),
            pl.BlockSpec((1, TOP_K, tm), tok3),
            pl.BlockSpec((n_exp, HEAD_DIM), lambda i: (0, 0)),
        ],
        out_shape=[
            jax.ShapeDtypeStruct((t, d), F32),
            jax.ShapeDtypeStruct((t, d // 2), U32),
            jax.ShapeDtypeStruct((nt, TOP_K, tm), I32),
            jax.ShapeDtypeStruct((nt, TOP_K, tm), F32),
            jax.ShapeDtypeStruct((nt, TOP_K, tm), I32),
            jax.ShapeDtypeStruct((n_exp, HEAD_DIM), F32),
        ],
        compiler_params=_params(1),
        name="outproj_route",
    )(merged, x, wout, gain, wr_t, tri)


def _dispatch_body(pad_end_ref, dest_ref, x_ref, xs_ref, zbuf_ref, zsem, sem, *, bm, n_exp):
    i = pl.program_id(0)
    tm = x_ref.shape[0]

    @pl.when(i == 0)
    def _zero_padding():
        zbuf_ref[...] = jnp.zeros_like(zbuf_ref)
        for e in range(n_exp):
            end = pad_end_ref[e]
            start = pad_end_ref[e - 1] if e > 0 else 0

            @pl.when(end > start)
            def _():
                cp = pltpu.make_async_copy(zbuf_ref, xs_ref.at[pl.ds(pl.multiple_of(end - bm, bm), bm)], zsem)
                cp.start()
                cp.wait()

    def issue(r, carry):
        for k in range(TOP_K):
            d = dest_ref[0, k, r]
            pltpu.make_async_copy(x_ref.at[pl.ds(r, 1)], xs_ref.at[pl.ds(d, 1)], sem).start()
        return carry

    lax.fori_loop(0, tm, issue, 0, unroll=8)
    for k in range(TOP_K):
        pltpu.make_async_copy(x_ref, xs_ref.at[pl.ds(0, tm)], sem).wait()


def _dispatch(hn2p, dest, pad_end, *, tm, bm, n_rows):
    t, dh = hn2p.shape
    n_exp = pad_end.shape[0]
    body = functools.partial(_dispatch_body, bm=bm, n_exp=n_exp)
    return pl.pallas_call(
        body,
        grid_spec=pltpu.PrefetchScalarGridSpec(
            num_scalar_prefetch=1,
            grid=(t // tm,),
            in_specs=[
                pl.BlockSpec((1, TOP_K, tm), lambda i, pe: (i, 0, 0), memory_space=pltpu.SMEM),
                pl.BlockSpec((tm, dh), lambda i, pe: (i, 0)),
            ],
            out_specs=pl.BlockSpec(memory_space=pl.ANY),
            scratch_shapes=[pltpu.VMEM((bm, dh), U32), pltpu.SemaphoreType.DMA(()), pltpu.SemaphoreType.DMA(())],
        ),
        out_shape=jax.ShapeDtypeStruct((n_rows, dh), U32),
        compiler_params=pltpu.CompilerParams(dimension_semantics=("arbitrary",),
                                             vmem_limit_bytes=VMEM_LIMIT_BYTES, has_side_effects=True),
        name="dispatch",
    )(pad_end, dest, hn2p)


def _ffn_body(bexp_ref, nvalid_ref, xs_ref, wg_ref, wu_ref, wd_ref, ys_ref):
    b = pl.program_id(0)

    @pl.when(b < nvalid_ref[0])
    def _():
        dh = xs_ref.shape[1]
        lo, hi = _unpack_bf16_pair(xs_ref[...])
        lo, hi = lo.astype(BF16), hi.astype(BF16)
        g = jnp.dot(lo, wg_ref[0, :dh, :], preferred_element_type=F32)
        g = g + jnp.dot(hi, wg_ref[0, dh:, :], preferred_element_type=F32)
        u = jnp.dot(lo, wu_ref[0, :dh, :], preferred_element_type=F32)
        u = u + jnp.dot(hi, wu_ref[0, dh:, :], preferred_element_type=F32)
        hid = (jax.nn.silu(g) * u).astype(BF16)
        y = jnp.dot(hid, wd_ref[0], preferred_element_type=F32)
        ys_ref[...] = _pack_bf16_pair(y[:, :dh], y[:, dh:])


def _expert_ffn(xs, wg, wu, wd, block_expert, n_valid, *, bm):
    n_rows, dh = xs.shape
    n_exp, d, de = wg.shape
    nb = n_rows // bm
    blk = lambda b, be, nv: (jnp.minimum(b, nv[0] - 1), 0)
    wmap = lambda b, be, nv: (be[jnp.minimum(b, nv[0] - 1)], 0, 0)
    return pl.pallas_call(
        _ffn_body,
        grid_spec=pltpu.PrefetchScalarGridSpec(
            num_scalar_prefetch=2,
            grid=(nb,),
            in_specs=[
                pl.BlockSpec((bm, dh), blk),
                pl.BlockSpec((1, d, de), wmap),
                pl.BlockSpec((1, d, de), wmap),
                pl.BlockSpec((1, de, d), wmap),
            ],
            out_specs=pl.BlockSpec((bm, dh), blk),
        ),
        out_shape=jax.ShapeDtypeStruct((n_rows, dh), U32),
        compiler_params=_params(1),
        name="expert_ffn",
    )(block_expert, n_valid, xs, wg, wu, wd)


def _combine_body(dest_ref, h_ref, cw_ref, gain_ref, ys_ref, o_ref, buf_ref, sem):
    tm, d = h_ref.shape
    dh = d // 2

    def issue(r, carry):
        for k in range(TOP_K):
            s = dest_ref[0, k, r]
            pltpu.make_async_copy(ys_ref.at[pl.ds(s, 1)], buf_ref.at[k, pl.ds(r, 1)], sem).start()
        return carry

    lax.fori_loop(0, tm, issue, 0, unroll=8)
    for k in range(TOP_K):
        pltpu.make_async_copy(ys_ref.at[pl.ds(0, tm)], buf_ref.at[k], sem).wait()

    lo1, hi1 = _unpack_bf16_pair(buf_ref[0])
    lo2, hi2 = _unpack_bf16_pair(buf_ref[1])
    w1 = cw_ref[:, 0:1]
    w2 = cw_ref[:, 1:2]
    o_lo = h_ref[:, :dh] + (w1 * lo1 + w2 * lo2)
    o_hi = h_ref[:, dh:] + (w1 * hi1 + w2 * hi2)
    ms = (jnp.sum(o_lo * o_lo, axis=-1, keepdims=True) + jnp.sum(o_hi * o_hi, axis=-1, keepdims=True)) / d
    r = lax.rsqrt(ms + NORM_EPS)
    o_ref[:, :dh] = o_lo * r * gain_ref[:, :dh]
    o_ref[:, dh:] = o_hi * r * gain_ref[:, dh:]


def _combine(ys, dest, hmid, cw_t, gain, *, tm):
    t, d = hmid.shape
    return pl.pallas_call(
        _combine_body,
        grid=(t // tm,),
        in_specs=[
            pl.BlockSpec((1, TOP_K, tm), lambda i: (i, 0, 0), memory_space=pltpu.SMEM),
            pl.BlockSpec((tm, d), lambda i: (i, 0)),
            pl.BlockSpec((tm, TOP_K), lambda i: (i, 0)),
            pl.BlockSpec((1, d), lambda i: (0, 0)),
            pl.BlockSpec(memory_space=pl.ANY),
        ],
        out_specs=pl.BlockSpec((tm, d), lambda i: (i, 0)),
        out_shape=jax.ShapeDtypeStruct((t, d), F32),
        scratch_shapes=[pltpu.VMEM((TOP_K, tm, d // 2), U32), pltpu.SemaphoreType.DMA(())],
        compiler_params=_params(1),
        name="combine",
    )(dest, hmid, cw_t, gain, ys)


def _rope_tables(n_tokens):
    rows = n_tokens // GRID_W
    row = jnp.repeat(jnp.arange(rows, dtype=F32), GRID_W)
    col = jnp.tile(jnp.arange(GRID_W, dtype=F32), rows)
    inv_freq = ROPE_THETA ** (-jnp.arange(ROPE_HALF, dtype=F32) / ROPE_HALF)
    ar, ac = row[:, None] * inv_freq, col[:, None] * inv_freq
    cos = jnp.concatenate([jnp.cos(ar), jnp.cos(ar), jnp.cos(ac), jnp.cos(ac)], axis=1)
    sin = jnp.concatenate([-jnp.sin(ar), jnp.sin(ar), -jnp.sin(ac), jnp.sin(ac)], axis=1)
    return cos, sin


def _seq_dft(n_tok, n_meta):
    length = n_tok + n_meta
    p_out = n_meta + jnp.arange(n_tok, dtype=I32)
    p_in = jnp.concatenate([p_out, jnp.arange(n_meta, dtype=I32)])
    ang = ((p_out[:, None] * p_in[None, :]) % length).astype(F32) * (2.0 * math.pi / length)
    scale = 1.0 / math.sqrt(length)
    c = (jnp.cos(ang) * scale).astype(BF16)
    ns = (-jnp.sin(ang) * scale).astype(BF16)
    return c[:, :n_tok], ns[:, :n_tok], c[:, n_tok:], ns[:, n_tok:]


def _chan_dft(cg):
    idx = jnp.arange(cg, dtype=I32)
    ang = ((idx[:, None] * idx[None, :]) % cg).astype(F32) * (2.0 * math.pi / cg)
    scale = 1.0 / math.sqrt(cg)
    return jnp.concatenate([jnp.cos(ang) * scale, jnp.sin(ang) * scale], axis=1).astype(BF16)


def _tile(n, pref):
    return min(n, pref)


def kernel(x_prompt, x_sample, meta_tokens, mix_norm, w_in, q_gain, k_gain, w_attn_o, w_fourier_o, w_out,
           moe_norm, w_router_group, w_router_expert, w_expert_gate, w_expert_up, w_expert_down, final_norm):
    bp, n_p, d = x_prompt.shape
    bs, n_s, _ = x_sample.shape
    n_meta = meta_tokens.shape[0]
    t_p, t_s = bp * n_p, bs * n_s
    t = t_p + t_s
    f_dim = d // 2
    cg = f_dim // N_FOURIER_GROUPS
    n_groups = w_router_group.shape[-1]
    n_exp = w_router_expert.shape[-1]
    assert n_exp == n_groups * EXPERTS_PER_GROUP and n_groups <= 8

    tm = _tile(math.gcd(n_p, n_s), 512)
    bm = _tile(t, 512)

    w_in_b = w_in[0].astype(BF16)
    wa_b, wf_b, wo_b = w_attn_o[0].astype(BF16), w_fourier_o[0].astype(BF16), w_out[0].astype(BF16)
    wg_b, wu_b, wd_b = (w_expert_gate[0].astype(BF16), w_expert_up[0].astype(BF16),
                        w_expert_down[0].astype(BF16))
    qg = (q_gain[0] * (HEAD_DIM ** -0.5 * LOG2E)).reshape(1, HEAD_DIM)
    kg = k_gain[0].reshape(1, HEAD_DIM)
    wr_t = jnp.zeros((8 + n_exp, d), F32)
    wr_t = wr_t.at[:n_groups].set(w_router_group[0].T).at[8:].set(w_router_expert[0].T).astype(BF16)
    tri = (jnp.arange(tm)[:, None] < jnp.arange(tm)[None, :]).astype(BF16)

    cos, sin = _rope_tables(max(n_p, n_s))
    cdft = _chan_dft(cg)

    x = jnp.concatenate([x_prompt.reshape(t_p, d), x_sample.reshape(t_s, d)], axis=0)
    ptiles, np_t, ns_t = t_p // tm, n_p // tm, n_s // tm
    pos_block = lambda i: jnp.where(i < ptiles, i % np_t, (i - ptiles) % ns_t)
    inproj = functools.partial(_inproj, gain=mix_norm[0].reshape(1, d), w_in=w_in_b, qg=qg, kg=kg, cdft=cdft)
    uc, us, q, k, v, sg = inproj(x, cos=cos, sin=sin, tm=tm, pos_block=pos_block)
    ucm, usm, _, km, vm, _ = inproj(meta_tokens, cos=jnp.ones((n_meta, HEAD_DIM), F32),
                                    sin=jnp.zeros((n_meta, HEAD_DIM), F32), tm=n_meta,
                                    pos_block=lambda i: 0)

    attn, z = None, None
    for row_off, n_seq, n_tok in ((0, bp, n_p), (t_p, bs, n_s)):
        attn = _attention(q, k, v, km, vm, attn, row_off=row_off, n_seq=n_seq, n_tok=n_tok,
                          tq=_tile(n_tok, 256))
        z = _fourier(uc, us, _seq_dft(n_tok, n_meta), ucm, usm, z, row_off=row_off, n_seq=n_seq,
                     n_tok=n_tok, tl=_tile(n_tok, 512), tf=_tile(f_dim, 512))

    merged = _merge(attn, z, sg, wa_b, wf_b, tm=tm)
    hmid, hn2p, eid, cw, rank, counts = _outproj_route(
        merged, x, wo_b, moe_norm[0].reshape(1, d), wr_t, tri, tm=tm, n_groups=n_groups, n_exp=n_exp)

    cnt = counts[:, 0].astype(I32)
    padded = (cnt + bm - 1) // bm * bm
    pad_end = jnp.cumsum(padded).astype(I32)
    pad_start = pad_end - padded
    dest = pad_start[eid] + rank
    nb = (t * TOP_K) // bm + n_exp
    n_valid = (pad_end[-1:] // bm).astype(I32)
    block_expert = jnp.minimum(
        jnp.searchsorted(pad_end, jnp.arange(nb, dtype=I32) * bm, side='right'), n_exp - 1).astype(I32)

    xs = _dispatch(hn2p, dest, pad_end, tm=tm, bm=bm, n_rows=nb * bm)
    ys = _expert_ffn(xs, wg_b, wu_b, wd_b, block_expert, n_valid, bm=bm)
    cw_t = jnp.transpose(cw, (0, 2, 1)).reshape(t, TOP_K)
    y = _combine(ys, dest, hmid, cw_t, final_norm.reshape(1, d), tm=tm)
    return y[:t_p].reshape(bp, n_p, d), y[t_p:].reshape(bs, n_s, d)
```

```python
import functools
import math

import jax
import jax.numpy as jnp
from jax import lax
from jax.experimental import pallas as pl
from jax.experimental.pallas import tpu as pltpu

F32, BF16, I32, U32 = jnp.float32, jnp.bfloat16, jnp.int32, jnp.uint32

GRID_W = 64
HEAD_DIM = 128
ROPE_HALF = 32
Q_PER_KV = 4
N_FOURIER_GROUPS = 4
ROPE_THETA = 10000.0
NORM_EPS = 1e-6
TOP_K = 2
EXPERTS_PER_GROUP = 8
LOG2E = 1.4426950408889634

VMEM_LIMIT_BYTES = 56 * 1024 * 1024


def _params(n_axes):
    return pltpu.CompilerParams(dimension_semantics=("arbitrary",) * n_axes,
                                vmem_limit_bytes=VMEM_LIMIT_BYTES)


def _resident(shape):
    return pl.BlockSpec(shape, lambda *_: (0,) * len(shape), pipeline_mode=pl.Buffered(1))


def _pack_bf16_pair(lo, hi):
    lo_bits = lax.bitcast_convert_type(lo.astype(BF16).astype(F32), U32) >> 16
    hi_bits = lax.bitcast_convert_type(hi.astype(BF16).astype(F32), U32) & jnp.uint32(0xFFFF0000)
    return lo_bits | hi_bits


def _unpack_bf16_pair(w):
    lo = lax.bitcast_convert_type(w << 16, F32)
    hi = lax.bitcast_convert_type(w & jnp.uint32(0xFFFF0000), F32)
    return lo, hi


def _inproj_body(xa_ref, xb_ref, gain_ref, w_ref, uf_ref, q_ref, k_ref, v_ref, g_ref, hn_ref, *,
                 n_a_tiles, kv_dim):
    i = pl.program_id(0)
    n = pl.program_id(1)

    def norm(x_ref):
        x = x_ref[...]
        y = x * lax.rsqrt(jnp.mean(x * x, axis=-1, keepdims=True) + NORM_EPS)
        hn_ref[...] = (y * gain_ref[...]).astype(BF16)

    pl.when((n == 0) & (i < n_a_tiles))(lambda: norm(xa_ref))
    pl.when((n == 0) & (i >= n_a_tiles))(lambda: norm(xb_ref))

    acc = jnp.dot(hn_ref[...], w_ref[...], preferred_element_type=F32)

    @pl.when(n == 0)
    def _fourier():
        uf_ref[...] = acc.astype(BF16)

    @pl.when((n == 1) | (n == 2))
    def _q():
        q_ref[...] = acc

    @pl.when(n == 3)
    def _kv():
        k_ref[...] = acc[:, :kv_dim]
        v_ref[...] = acc[:, kv_dim:].astype(BF16)

    @pl.when(n >= 4)
    def _gates():
        g_ref[...] = acc.astype(BF16)


def _inproj(xa, xb, gain, w_in, *, tm):
    (ta, d), tb = xa.shape, xb.shape[0]
    t = ta + tb
    tn = d // 2
    f_dim, q_dim, kv_dim = d // 2, d, d // 4
    assert w_in.shape == (d, 8 * tn) and ta % tm == 0 and tb % tm == 0
    na = ta // tm
    body = functools.partial(_inproj_body, n_a_tiles=na, kv_dim=kv_dim)
    row = lambda i, n: (i, 0)
    return pl.pallas_call(
        body,
        grid=(t // tm, 8),
        in_specs=[
            pl.BlockSpec((tm, d), lambda i, n: (jnp.minimum(i, na - 1), 0)),
            pl.BlockSpec((tm, d), lambda i, n: (jnp.maximum(i - na, 0), 0)),
            pl.BlockSpec((1, d), lambda i, n: (0, 0)),
            pl.BlockSpec((d, tn), lambda i, n: (0, n)),
        ],
        out_specs=[
            pl.BlockSpec((tm, f_dim), row),
            pl.BlockSpec((tm, tn), lambda i, n: (i, jnp.clip(n - 1, 0, 1))),
            pl.BlockSpec((tm, kv_dim), row),
            pl.BlockSpec((tm, kv_dim), row),
            pl.BlockSpec((tm, tn), lambda i, n: (i, jnp.clip(n - 4, 0, 3))),
        ],
        out_shape=[
            jax.ShapeDtypeStruct((t, f_dim), BF16),
            jax.ShapeDtypeStruct((t, q_dim), F32),
            jax.ShapeDtypeStruct((t, kv_dim), F32),
            jax.ShapeDtypeStruct((t, kv_dim), BF16),
            jax.ShapeDtypeStruct((t, 2 * d), BF16),
        ],
        scratch_shapes=[pltpu.VMEM((tm, d), BF16)],
        compiler_params=_params(2),
        name="inproj",
    )(xa, xb, gain, w_in)


def _head_norm(a, gain):
    return a * lax.rsqrt(jnp.mean(a * a, axis=-1, keepdims=True) + NORM_EPS) * gain


def _rope(y, cos, sin):
    lo_half = (lax.broadcasted_iota(I32, y.shape, 1) & ROPE_HALF) == 0
    partner = jnp.where(lo_half, pltpu.roll(y, HEAD_DIM - ROPE_HALF, 1), pltpu.roll(y, ROPE_HALF, 1))
    return y * cos + partner * sin


def _attn_body(q_ref, k_ref, v_ref, km_ref, vm_ref, qg_ref, kg_ref, cq_ref, sq_ref, ck_ref, sk_ref, *rest):
    o_ref, kb_ref, kmb_ref = rest[-3:]

    @pl.when(pl.program_id(2) == 0)
    def _prepare_keys():
        kb_ref[...] = _rope(_head_norm(k_ref[...], kg_ref[...]), ck_ref[...], sk_ref[...]).astype(BF16)
        kmb_ref[...] = _head_norm(km_ref[...], kg_ref[...]).astype(BF16)

    k, v, km, vm = kb_ref[...], v_ref[...], kmb_ref[...], vm_ref[...]
    cq, sq, qg = cq_ref[...], sq_ref[...], qg_ref[...]
    nt = (((1,), (1,)), ((), ()))
    for h in range(Q_PER_KV):
        qh = _rope(_head_norm(q_ref[:, h * HEAD_DIM:(h + 1) * HEAD_DIM], qg), cq, sq).astype(BF16)
        s = lax.dot_general(qh, k, nt, preferred_element_type=F32)
        sm = lax.dot_general(qh, km, nt, preferred_element_type=F32)
        m = jnp.maximum(jnp.max(s, axis=-1, keepdims=True), jnp.max(sm, axis=-1, keepdims=True))
        p = jnp.exp2(s - m)
        pm = jnp.exp2(sm - m)
        denom = jnp.sum(p, axis=-1, keepdims=True) + jnp.sum(pm, axis=-1, keepdims=True)
        o = jnp.dot(p.astype(BF16), v, preferred_element_type=F32)
        o = o + jnp.dot(pm.astype(BF16), vm, preferred_element_type=F32)
        o_ref[:, h * HEAD_DIM:(h + 1) * HEAD_DIM] = (o / denom).astype(o_ref.dtype)


def _attention(q, k, v, km, vm, qg, kg, cos, sin, prev, *, row_off, n_seq, n_tok, tq):
    t, q_dim = q.shape
    n_kv = k.shape[1] // HEAD_DIM
    n_meta = km.shape[0]
    gw = Q_PER_KV * HEAD_DIM
    assert row_off % n_tok == 0 and n_tok % tq == 0
    qmap = lambda b, kh, qi: ((row_off + b * n_tok) // tq + qi, kh)
    kmap = lambda b, kh, qi: (row_off // n_tok + b, kh)
    mmap = lambda b, kh, qi: (0, kh)
    const = lambda b, kh, qi: (0, 0)
    qpos = lambda b, kh, qi: (qi, 0)
    in_specs = [
        pl.BlockSpec((tq, gw), qmap),
        pl.BlockSpec((n_tok, HEAD_DIM), kmap),
        pl.BlockSpec((n_tok, HEAD_DIM), kmap),
        pl.BlockSpec((n_meta, HEAD_DIM), mmap),
        pl.BlockSpec((n_meta, HEAD_DIM), mmap),
        pl.BlockSpec((1, HEAD_DIM), const),
        pl.BlockSpec((1, HEAD_DIM), const),
        pl.BlockSpec((tq, HEAD_DIM), qpos),
        pl.BlockSpec((tq, HEAD_DIM), qpos),
        pl.BlockSpec((n_tok, HEAD_DIM), const),
        pl.BlockSpec((n_tok, HEAD_DIM), const),
    ]
    args = [q, k, v, km, vm, qg, kg, cos, sin, cos, sin]
    aliases = {}
    if prev is not None:
        in_specs.append(pl.BlockSpec(memory_space=pl.ANY))
        args.append(prev)
        aliases = {len(args) - 1: 0}
    return pl.pallas_call(
        _attn_body,
        grid=(n_seq, n_kv, n_tok // tq),
        in_specs=in_specs,
        out_specs=pl.BlockSpec((tq, gw), qmap),
        out_shape=jax.ShapeDtypeStruct((t, q_dim), BF16),
        scratch_shapes=[pltpu.VMEM((n_tok, HEAD_DIM), BF16), pltpu.VMEM((n_meta, HEAD_DIM), BF16)],
        input_output_aliases=aliases,
        compiler_params=_params(3),
        name="attention",
    )(*args)


def _dftgen_body(tac_ref, tas_ref, tbc_ref, tbs_ref, c_ref, ns_ref):
    ac, asn = tac_ref[0], tas_ref[0]
    bc, bsn = tbc_ref[...], tbs_ref[...]
    c_ref[...] = (ac * bc - asn * bsn).astype(BF16)
    ns_ref[...] = (-(asn * bc + ac * bsn)).astype(BF16)


def _seq_dft(n_tok, n_meta):
    length = n_tok + n_meta
    rb = min(n_tok, HEAD_DIM)
    w = 2.0 * math.pi / length
    scale = 1.0 / math.sqrt(length)
    p_real = n_meta + jnp.arange(n_tok, dtype=I32)

    def angles(p_rows, p_cols):
        return ((p_rows[:, None] * p_cols[None, :]) % length).astype(F32) * w

    ang_a = angles(n_meta + rb * jnp.arange(n_tok // rb, dtype=I32), p_real)
    ang_b = angles(jnp.arange(rb, dtype=I32), p_real)
    ang_m = angles(p_real, jnp.arange(n_meta, dtype=I32))
    tac = (jnp.cos(ang_a) * scale).reshape(n_tok // rb, 1, n_tok)
    tas = (jnp.sin(ang_a) * scale).reshape(n_tok // rb, 1, n_tok)
    blk = pl.BlockSpec((1, 1, n_tok), lambda a: (a, 0, 0))
    full = pl.BlockSpec((rb, n_tok), lambda a: (0, 0))
    cmat, nsmat = pl.pallas_call(
        _dftgen_body,
        grid=(n_tok // rb,),
        in_specs=[blk, blk, full, full],
        out_specs=[pl.BlockSpec((rb, n_tok), lambda a: (a, 0))] * 2,
        out_shape=[jax.ShapeDtypeStruct((n_tok, n_tok), BF16)] * 2,
        compiler_params=_params(1),
        name="dft_matrices",
    )(tac, tas, jnp.cos(ang_b), jnp.sin(ang_b))
    return cmat, nsmat, (jnp.cos(ang_m) * scale).astype(BF16), (-jnp.sin(ang_m) * scale).astype(BF16)


def _chan_dft(cg):
    idx = jnp.arange(cg, dtype=I32)
    ang = ((idx[:, None] * idx[None, :]) % cg).astype(F32) * (2.0 * math.pi / cg)
    scale = 1.0 / math.sqrt(cg)
    return (jnp.cos(ang) * scale).astype(BF16), (jnp.sin(ang) * scale).astype(BF16)


def _fourier_body(c_ref, ns_ref, cm_ref, nsm_ref, u_ref, um_ref, cdc_ref, cds_ref, *rest):
    z_ref = rest[-1]
    u, um = u_ref[...], um_ref[...]
    a = jnp.dot(c_ref[...], u, preferred_element_type=F32) + jnp.dot(cm_ref[...], um, preferred_element_type=F32)
    b = jnp.dot(ns_ref[...], u, preferred_element_type=F32) + jnp.dot(nsm_ref[...], um, preferred_element_type=F32)
    a, b = a.astype(BF16), b.astype(BF16)
    cg = cdc_ref.shape[0]
    for g in range(z_ref.shape[1] // cg):
        sl = slice(g * cg, (g + 1) * cg)
        z = jnp.dot(a[:, sl], cdc_ref[...], preferred_element_type=F32)
        z = z + jnp.dot(b[:, sl], cds_ref[...], preferred_element_type=F32)
        z_ref[:, sl] = z.astype(z_ref.dtype)


def _fourier(uf, ufm, mats, cdc, cds, prev, *, row_off, n_seq, n_tok, tl, tf):
    cmat, nsmat, cmeta, nsmeta = mats
    t, f_dim = uf.shape
    n_meta = ufm.shape[0]
    cg = cdc.shape[0]
    assert row_off % n_tok == 0 and n_tok % tl == 0 and f_dim % tf == 0 and tf % cg == 0
    cmap = lambda b, j, i: (i, 0)
    const = lambda b, j, i: (0, 0)
    in_specs = [
        pl.BlockSpec((tl, n_tok), cmap),
        pl.BlockSpec((tl, n_tok), cmap),
        pl.BlockSpec((tl, n_meta), cmap),
        pl.BlockSpec((tl, n_meta), cmap),
        pl.BlockSpec((n_tok, tf), lambda b, j, i: (row_off // n_tok + b, j)),
        pl.BlockSpec((n_meta, tf), lambda b, j, i: (0, j)),
        pl.BlockSpec((cg, cg), const),
        pl.BlockSpec((cg, cg), const),
    ]
    args = [cmat, nsmat, cmeta, nsmeta, uf, ufm, cdc, cds]
    aliases = {}
    if prev is not None:
        in_specs.append(pl.BlockSpec(memory_space=pl.ANY))
        args.append(prev)
        aliases = {len(args) - 1: 0}
    return pl.pallas_call(
        _fourier_body,
        grid=(n_seq, f_dim // tf, n_tok // tl),
        in_specs=in_specs,
        out_specs=pl.BlockSpec((tl, tf), lambda b, j, i: ((row_off + b * n_tok) // tl + i, j)),
        out_shape=jax.ShapeDtypeStruct((t, f_dim), BF16),
        input_output_aliases=aliases,
        compiler_params=_params(3),
        name="fourier",
    )(*args)


def _merge_body(a_ref, z_ref, ga_ref, gf_ref, wa_ref, wf_ref, o_ref):
    a = jnp.dot(a_ref[...], wa_ref[...], preferred_element_type=F32)
    f = jnp.dot(z_ref[...], wf_ref[...], preferred_element_type=F32)
    sga = jax.nn.sigmoid(ga_ref[...].astype(F32))
    sgf = jax.nn.sigmoid(gf_ref[...].astype(F32))
    o_ref[...] = (sga * a + sgf * f).astype(o_ref.dtype)


def _merge(attn, z, g, wa, wf, *, tm):
    t, d = attn.shape
    f_dim = z.shape[1]
    return pl.pallas_call(
        _merge_body,
        grid=(t // tm,),
        in_specs=[
            pl.BlockSpec((tm, d), lambda i: (i, 0)),
            pl.BlockSpec((tm, f_dim), lambda i: (i, 0)),
            pl.BlockSpec((tm, d), lambda i: (i, 0)),
            pl.BlockSpec((tm, d), lambda i: (i, 1)),
            _resident((d, d)),
            _resident((f_dim, d)),
        ],
        out_specs=pl.BlockSpec((tm, d), lambda i: (i, 0)),
        out_shape=jax.ShapeDtypeStruct((t, d), BF16),
        compiler_params=_params(1),
        name="merge",
    )(attn, z, g, g, wa, wf)


def _route_body(m_ref, xa_ref, xb_ref, wo_ref, gain_ref, wr_ref, tri_ref,
                hmid_ref, hn_ref, eid_ref, cw_ref, rank_ref, cnt_ref, *, n_a_tiles, n_groups):
    i = pl.program_id(0)
    tm, d = xa_ref.shape
    dh = d // 2
    n_exp = cnt_ref.shape[0]

    @pl.when(i == 0)
    def _init():
        cnt_ref[...] = jnp.zeros_like(cnt_ref)

    x = jnp.where(i < n_a_tiles, xa_ref[...], xb_ref[...])
    h = x + jnp.dot(m_ref[...], wo_ref[...], preferred_element_type=F32)
    hmid_ref[...] = h
    hn = h * lax.rsqrt(jnp.mean(h * h, axis=-1, keepdims=True) + NORM_EPS) * gain_ref[...]
    hb = hn.astype(BF16)
    hn_ref[...] = _pack_bf16_pair(hn[:, :dh], hn[:, dh:])

    lt = lax.dot_general(wr_ref[...], hb, (((1,), (1,)), ((), ())), preferred_element_type=F32)
    row8 = lax.broadcasted_iota(I32, (EXPERTS_PER_GROUP, tm), 0)
    neg = jnp.float32(-jnp.inf)
    lg = jnp.where(row8 < n_groups, lt[0:8], neg)
    gmax = jnp.max(lg, axis=0, keepdims=True)
    gidx = jnp.min(jnp.where(lg == gmax, row8, 8), axis=0, keepdims=True)
    p_g = 1.0 / jnp.sum(jnp.exp(lg - gmax), axis=0, keepdims=True)
    sel = lt[8:8 + EXPERTS_PER_GROUP]
    for g in range(1, n_groups):
        sel = jnp.where(gidx == g, lt[8 + g * EXPERTS_PER_GROUP:8 + (g + 1) * EXPERTS_PER_GROUP], sel)
    m1 = jnp.max(sel, axis=0, keepdims=True)
    i1 = jnp.min(jnp.where(sel == m1, row8, 8), axis=0, keepdims=True)
    sel2 = jnp.where(row8 == i1, neg, sel)
    m2 = jnp.max(sel2, axis=0, keepdims=True)
    i2 = jnp.min(jnp.where(sel2 == m2, row8, 8), axis=0, keepdims=True)
    e21 = jnp.exp(m2 - m1)
    p1 = 1.0 / (1.0 + e21)
    p2 = e21 * p1
    e1 = gidx * EXPERTS_PER_GROUP + i1
    e2 = gidx * EXPERTS_PER_GROUP + i2
    eid_ref[0, 0:1, :] = e1
    eid_ref[0, 1:2, :] = e2
    cw_ref[0, 0:1, :] = p_g * p1
    cw_ref[0, 1:2, :] = p_g * p2

    row_e = lax.broadcasted_iota(I32, (n_exp, tm), 0)
    oh1 = (row_e == e1)
    oh2 = (row_e == e2)
    cs1 = jnp.dot(oh1.astype(BF16), tri_ref[...], preferred_element_type=F32)
    cs2 = jnp.dot(oh2.astype(BF16), tri_ref[...], preferred_element_type=F32)
    oh1f, oh2f = oh1.astype(F32), oh2.astype(F32)
    c1 = jnp.sum(oh1f, axis=1, keepdims=True)
    c2 = jnp.sum(oh2f, axis=1, keepdims=True)
    base = cnt_ref[:, 0:1]
    rank_ref[0, 0:1, :] = jnp.sum(oh1f * (base + cs1), axis=0, keepdims=True).astype(I32)
    rank_ref[0, 1:2, :] = jnp.sum(oh2f * (base + c1 + cs2), axis=0, keepdims=True).astype(I32)
    cnt_ref[...] = cnt_ref[...] + (c1 + c2)


def _outproj_route(merged, xa, xb, wout, gain, wr_t, tri, *, tm, n_groups, n_exp):
    t, d = merged.shape
    nt = t // tm
    na = xa.shape[0] // tm
    tok3 = lambda i: (i, 0, 0)
    body = functools.partial(_route_body, n_a_tiles=na, n_groups=n_groups)
    return pl.pallas_call(
        body,
        grid=(nt,),
        in_specs=[
            pl.BlockSpec((tm, d), lambda i: (i, 0)),
            pl.BlockSpec((tm, d), lambda i: (jnp.minimum(i, na - 1), 0)),
            pl.BlockSpec((tm, d), lambda i: (jnp.maximum(i - na, 0), 0)),
            _resident((d, d)),
            pl.BlockSpec((1, d), lambda i: (0, 0)),
            _resident(wr_t.shape),
            _resident((tm, tm)),
        ],
        out_specs=[
            pl.BlockSpec((tm, d), lambda i: (i, 0)),
            pl.BlockSpec((tm, d // 2), lambda i: (i, 0)),
            pl.BlockSpec((1, TOP_K, tm), tok3),
            pl.BlockSpec((1, TOP_K, tm), tok3),
            pl.BlockSpec((1, TOP_K, tm), tok3),
            pl.BlockSpec((n_exp, HEAD_DIM), lambda i: (0, 0)),
        ],
        out_shape=[
            jax.ShapeDtypeStruct((t, d), F32),
            jax.ShapeDtypeStruct((t, d // 2), U32),
            jax.ShapeDtypeStruct((nt, TOP_K, tm), I32),
            jax.ShapeDtypeStruct((nt, TOP_K, tm), F32),
            jax.ShapeDtypeStruct((nt, TOP_K, tm), I32),
            jax.ShapeDtypeStruct((n_exp, HEAD_DIM), F32),
        ],
        compiler_params=_params(1),
        name="outproj_route",
    )(merged, xa, xb, wout, gain, wr_t, tri)


def _dispatch_body(pad_end_ref, dest_ref, x_ref, xs_ref, zbuf_ref, zsem, sem, *, bm, n_exp):
    i = pl.program_id(0)
    tm = x_ref.shape[0]

    @pl.when(i == 0)
    def _zero_padding():
        zbuf_ref[...] = jnp.zeros_like(zbuf_ref)
        for e in range(n_exp):
            end = pad_end_ref[e]
            start = pad_end_ref[e - 1] if e > 0 else 0

            @pl.when(end > start)
            def _():
                cp = pltpu.make_async_copy(zbuf_ref, xs_ref.at[pl.ds(pl.multiple_of(end - bm, bm), bm)], zsem)
                cp.start()
                cp.wait()

    def issue(r, carry):
        for k in range(TOP_K):
            d = dest_ref[0, k, r]
            pltpu.make_async_copy(x_ref.at[pl.ds(r, 1)], xs_ref.at[pl.ds(d, 1)], sem).start()
        return carry

    lax.fori_loop(0, tm, issue, 0, unroll=8)
    for k in range(TOP_K):
        pltpu.make_async_copy(x_ref, xs_ref.at[pl.ds(0, tm)], sem).wait()


def _dispatch(hn2p, dest, pad_end, *, tm, bm, n_rows):
    t, dh = hn2p.shape
    n_exp = pad_end.shape[0]
    body = functools.partial(_dispatch_body, bm=bm, n_exp=n_exp)
    return pl.pallas_call(
        body,
        grid_spec=pltpu.PrefetchScalarGridSpec(
            num_scalar_prefetch=1,
            grid=(t // tm,),
            in_specs=[
                pl.BlockSpec((1, TOP_K, tm), lambda i, pe: (i, 0, 0), memory_space=pltpu.SMEM),
                pl.BlockSpec((tm, dh), lambda i, pe: (i, 0)),
            ],
            out_specs=pl.BlockSpec(memory_space=pl.ANY),
            scratch_shapes=[pltpu.VMEM((bm, dh), U32), pltpu.SemaphoreType.DMA(()), pltpu.SemaphoreType.DMA(())],
        ),
        out_shape=jax.ShapeDtypeStruct((n_rows, dh), U32),
        compiler_params=_params(1),
        name="dispatch",
    )(pad_end, dest, hn2p)


def _ffn_body(bexp_ref, nvalid_ref, xs_ref, wg_ref, wu_ref, wd_ref, ys_ref):
    b = pl.program_id(0)

    @pl.when(b < nvalid_ref[0])
    def _():
        dh = xs_ref.shape[1]
        lo, hi = _unpack_bf16_pair(xs_ref[...])
        lo, hi = lo.astype(BF16), hi.astype(BF16)
        g = jnp.dot(lo, wg_ref[0, :dh, :], preferred_element_type=F32)
        g = g + jnp.dot(hi, wg_ref[0, dh:, :], preferred_element_type=F32)
        u = jnp.dot(lo, wu_ref[0, :dh, :], preferred_element_type=F32)
        u = u + jnp.dot(hi, wu_ref[0, dh:, :], preferred_element_type=F32)
        hid = (jax.nn.silu(g) * u).astype(BF16)
        y = jnp.dot(hid, wd_ref[0], preferred_element_type=F32)
        ys_ref[...] = _pack_bf16_pair(y[:, :dh], y[:, dh:])


def _expert_ffn(xs, wg, wu, wd, block_expert, n_valid, *, bm):
    n_rows, dh = xs.shape
    n_exp, d, de = wg.shape
    nb = n_rows // bm
    blk = lambda b, be, nv: (jnp.minimum(b, nv[0] - 1), 0)
    wmap = lambda b, be, nv: (be[jnp.minimum(b, nv[0] - 1)], 0, 0)
    return pl.pallas_call(
        _ffn_body,
        grid_spec=pltpu.PrefetchScalarGridSpec(
            num_scalar_prefetch=2,
            grid=(nb,),
            in_specs=[
                pl.BlockSpec((bm, dh), blk),
                pl.BlockSpec((1, d, de), wmap),
                pl.BlockSpec((1, d, de), wmap),
                pl.BlockSpec((1, de, d), wmap),
            ],
            out_specs=pl.BlockSpec((bm, dh), blk),
        ),
        out_shape=jax.ShapeDtypeStruct((n_rows, dh), U32),
        compiler_params=_params(1),
        name="expert_ffn",
    )(block_expert, n_valid, xs, wg, wu, wd)


def _combine_body(dest_ref, h_ref, cw_ref, gain_ref, ys_ref, o_ref, buf_ref, sem):
    tm, d = h_ref.shape
    dh = d // 2

    def issue(r, carry):
        for k in range(TOP_K):
            s = dest_ref[0, k, r]
            pltpu.make_async_copy(ys_ref.at[pl.ds(s, 1)], buf_ref.at[k, pl.ds(r, 1)], sem).start()
        return carry

    lax.fori_loop(0, tm, issue, 0, unroll=8)
    for k in range(TOP_K):
        pltpu.make_async_copy(ys_ref.at[pl.ds(0, tm)], buf_ref.at[k], sem).wait()

    lo1, hi1 = _unpack_bf16_pair(buf_ref[0])
    lo2, hi2 = _unpack_bf16_pair(buf_ref[1])
    w1 = cw_ref[:, 0:1]
    w2 = cw_ref[:, 1:2]
    o_lo = h_ref[:, :dh] + (w1 * lo1 + w2 * lo2)
    o_hi = h_ref[:, dh:] + (w1 * hi1 + w2 * hi2)
    ms = (jnp.sum(o_lo * o_lo, axis=-1, keepdims=True) + jnp.sum(o_hi * o_hi, axis=-1, keepdims=True)) / d
    r = lax.rsqrt(ms + NORM_EPS)
    o_ref[:, :dh] = o_lo * r * gain_ref[:, :dh]
    o_ref[:, dh:] = o_hi * r * gain_ref[:, dh:]


def _combine(ys, dest, hmid, cw_t, gain, *, tm, row_off, n_rows):
    d = hmid.shape[1]
    off = row_off // tm
    return pl.pallas_call(
        _combine_body,
        grid=(n_rows // tm,),
        in_specs=[
            pl.BlockSpec((1, TOP_K, tm), lambda i: (off + i, 0, 0), memory_space=pltpu.SMEM),
            pl.BlockSpec((tm, d), lambda i: (off + i, 0)),
            pl.BlockSpec((tm, TOP_K), lambda i: (off + i, 0)),
            pl.BlockSpec((1, d), lambda i: (0, 0)),
            pl.BlockSpec(memory_space=pl.ANY),
        ],
        out_specs=pl.BlockSpec((tm, d), lambda i: (i, 0)),
        out_shape=jax.ShapeDtypeStruct((n_rows, d), F32),
        scratch_shapes=[pltpu.VMEM((TOP_K, tm, d // 2), U32), pltpu.SemaphoreType.DMA(())],
        compiler_params=_params(1),
        name="combine",
    )(dest, hmid, cw_t, gain, ys)


def _rope_tables(n_tokens):
    rows = n_tokens // GRID_W
    row = jnp.repeat(jnp.arange(rows, dtype=F32), GRID_W)
    col = jnp.tile(jnp.arange(GRID_W, dtype=F32), rows)
    inv_freq = ROPE_THETA ** (-jnp.arange(ROPE_HALF, dtype=F32) / ROPE_HALF)
    ar, ac = row[:, None] * inv_freq, col[:, None] * inv_freq
    cos = jnp.concatenate([jnp.cos(ar), jnp.cos(ar), jnp.cos(ac), jnp.cos(ac)], axis=1)
    sin = jnp.concatenate([-jnp.sin(ar), jnp.sin(ar), -jnp.sin(ac), jnp.sin(ac)], axis=1)
    return cos, sin


def _tile(n, pref):
    return min(n, pref)


def kernel(x_prompt, x_sample, meta_tokens, mix_norm, w_in, q_gain, k_gain, w_attn_o, w_fourier_o, w_out,
           moe_norm, w_router_group, w_router_expert, w_expert_gate, w_expert_up, w_expert_down, final_norm):
    bp, n_p, d = x_prompt.shape
    bs, n_s, _ = x_sample.shape
    n_meta = meta_tokens.shape[0]
    t_p, t_s = bp * n_p, bs * n_s
    t = t_p + t_s
    f_dim = d // 2
    cg = f_dim // N_FOURIER_GROUPS
    n_groups = w_router_group.shape[-1]
    n_exp = w_router_expert.shape[-1]
    assert n_exp == n_groups * EXPERTS_PER_GROUP and n_groups <= 8

    tm = _tile(math.gcd(n_p, n_s), 512)
    bm = _tile(t, 512)

    w_in_b = w_in[0].astype(BF16)
    wa_b, wf_b, wo_b = w_attn_o[0].astype(BF16), w_fourier_o[0].astype(BF16), w_out[0].astype(BF16)
    wg_b, wu_b, wd_b = (w_expert_gate[0].astype(BF16), w_expert_up[0].astype(BF16),
                        w_expert_down[0].astype(BF16))
    qg = (q_gain[0] * (HEAD_DIM ** -0.5 * LOG2E)).reshape(1, HEAD_DIM)
    kg = k_gain[0].reshape(1, HEAD_DIM)
    wr_t = jnp.zeros((8 + n_exp, d), F32)
    wr_t = wr_t.at[:n_groups].set(w_router_group[0].T).at[8:].set(w_router_expert[0].T).astype(BF16)
    tri = (jnp.arange(tm)[:, None] < jnp.arange(tm)[None, :]).astype(BF16)
    cos, sin = _rope_tables(max(n_p, n_s))
    cdc, cds = _chan_dft(cg)
    gain_in = mix_norm[0].reshape(1, d)

    xp, xs_in = x_prompt.reshape(t_p, d), x_sample.reshape(t_s, d)
    uf, q, k, v, g = _inproj(xp, xs_in, gain_in, w_in_b, tm=tm)
    ufm, _, km, vm, _ = _inproj(meta_tokens, meta_tokens, gain_in, w_in_b, tm=n_meta)
    ufm, km, vm = ufm[:n_meta], km[:n_meta], vm[:n_meta]

    attn, z = None, None
    for row_off, n_seq, n_tok in ((0, bp, n_p), (t_p, bs, n_s)):
        attn = _attention(q, k, v, km, vm, qg, kg, cos, sin, attn, row_off=row_off, n_seq=n_seq,
                          n_tok=n_tok, tq=_tile(n_tok, 256))
        z = _fourier(uf, ufm, _seq_dft(n_tok, n_meta), cdc, cds, z, row_off=row_off, n_seq=n_seq,
                     n_tok=n_tok, tl=_tile(n_tok, 512), tf=_tile(f_dim, 512))

    merged = _merge(attn, z, g, wa_b, wf_b, tm=tm)
    hmid, hn2p, eid, cw, rank, counts = _outproj_route(
        merged, xp, xs_in, wo_b, moe_norm[0].reshape(1, d), wr_t, tri, tm=tm, n_groups=n_groups, n_exp=n_exp)

    cnt = counts[:, 0].astype(I32)
    padded = (cnt + bm - 1) // bm * bm
    pad_end = jnp.cumsum(padded).astype(I32)
    pad_start = pad_end - padded
    onehot = eid[..., None] == jnp.arange(n_exp, dtype=I32)
    dest = jnp.sum(jnp.where(onehot, pad_start, 0), axis=-1) + rank
    nb = (t * TOP_K) // bm + n_exp
    n_valid = (pad_end[-1:] // bm).astype(I32)
    blk_start = jnp.arange(nb, dtype=I32) * bm
    block_expert = jnp.minimum(jnp.sum(pad_end[None, :] <= blk_start[:, None], axis=1), n_exp - 1).astype(I32)

    xs = _dispatch(hn2p, dest, pad_end, tm=tm, bm=bm, n_rows=nb * bm)
    ys = _expert_ffn(xs, wg_b, wu_b, wd_b, block_expert, n_valid, bm=bm)
    cw_t = jnp.transpose(cw, (0, 2, 1)).reshape(t, TOP_K)
    fgain = final_norm.reshape(1, d)
    y_p = _combine(ys, dest, hmid, cw_t, fgain, tm=tm, row_off=0, n_rows=t_p)
    y_s = _combine(ys, dest, hmid, cw_t, fgain, tm=tm, row_off=t_p, n_rows=t_s)
    return y_p.reshape(bp, n_p, d), y_s.reshape(bs, n_s, d)
```

```python
import functools
import math

import jax
import jax.numpy as jnp
from jax import lax
from jax.experimental import pallas as pl
from jax.experimental.pallas import tpu as pltpu

F32, BF16, I32, U32 = jnp.float32, jnp.bfloat16, jnp.int32, jnp.uint32

GRID_W = 64
HEAD_DIM = 128
ROPE_HALF = 32
Q_PER_KV = 4
N_FOURIER_GROUPS = 4
ROPE_THETA = 10000.0
NORM_EPS = 1e-6
TOP_K = 2
EXPERTS_PER_GROUP = 8
LOG2E = 1.4426950408889634
ATTN_KV_CHUNK = 512

VMEM_LIMIT_BYTES = 56 * 1024 * 1024


def _params(n_axes):
    return pltpu.CompilerParams(dimension_semantics=("arbitrary",) * n_axes,
                                vmem_limit_bytes=VMEM_LIMIT_BYTES)


def _resident(shape):
    return pl.BlockSpec(shape, lambda *_: (0,) * len(shape), pipeline_mode=pl.Buffered(1))


def _pack_bf16_pair(lo, hi):
    lo_bits = lax.bitcast_convert_type(lo.astype(BF16).astype(F32), U32) >> 16
    hi_bits = lax.bitcast_convert_type(hi.astype(BF16).astype(F32), U32) & jnp.uint32(0xFFFF0000)
    return lo_bits | hi_bits


def _unpack_bf16_pair(w):
    lo = lax.bitcast_convert_type(w << 16, F32)
    hi = lax.bitcast_convert_type(w & jnp.uint32(0xFFFF0000), F32)
    return lo, hi


def _inproj_body(xa_ref, xb_ref, gain_ref, w_ref, uf_ref, q_ref, k_ref, v_ref, g_ref, hn_ref, *,
                 n_a_tiles, kv_dim):
    i = pl.program_id(0)
    n = pl.program_id(1)

    def norm(x_ref):
        x = x_ref[...]
        y = x * lax.rsqrt(jnp.mean(x * x, axis=-1, keepdims=True) + NORM_EPS)
        hn_ref[...] = (y * gain_ref[...]).astype(BF16)

    pl.when((n == 0) & (i < n_a_tiles))(lambda: norm(xa_ref))
    pl.when((n == 0) & (i >= n_a_tiles))(lambda: norm(xb_ref))

    acc = jnp.dot(hn_ref[...], w_ref[...], preferred_element_type=F32)

    @pl.when(n == 0)
    def _fourier():
        uf_ref[...] = acc.astype(BF16)

    @pl.when((n == 1) | (n == 2))
    def _q():
        q_ref[...] = acc

    @pl.when(n == 3)
    def _kv():
        k_ref[...] = acc[:, :kv_dim]
        v_ref[...] = acc[:, kv_dim:].astype(BF16)

    @pl.when(n >= 4)
    def _gates():
        g_ref[...] = acc.astype(BF16)


def _inproj(xa, xb, gain, w_in, *, tm):
    (ta, d), tb = xa.shape, xb.shape[0]
    t = ta + tb
    tn = d // 2
    f_dim, q_dim, kv_dim = d // 2, d, d // 4
    assert w_in.shape == (d, 8 * tn) and ta % tm == 0 and tb % tm == 0
    na = ta // tm
    body = functools.partial(_inproj_body, n_a_tiles=na, kv_dim=kv_dim)
    row = lambda i, n: (i, 0)
    return pl.pallas_call(
        body,
        grid=(t // tm, 8),
        in_specs=[
            pl.BlockSpec((tm, d), lambda i, n: (jnp.minimum(i, na - 1), 0)),
            pl.BlockSpec((tm, d), lambda i, n: (jnp.maximum(i - na, 0), 0)),
            pl.BlockSpec((1, d), lambda i, n: (0, 0)),
            pl.BlockSpec((d, tn), lambda i, n: (0, n)),
        ],
        out_specs=[
            pl.BlockSpec((tm, f_dim), row),
            pl.BlockSpec((tm, tn), lambda i, n: (i, jnp.clip(n - 1, 0, 1))),
            pl.BlockSpec((tm, kv_dim), row),
            pl.BlockSpec((tm, kv_dim), row),
            pl.BlockSpec((tm, tn), lambda i, n: (i, jnp.clip(n - 4, 0, 3))),
        ],
        out_shape=[
            jax.ShapeDtypeStruct((t, f_dim), BF16),
            jax.ShapeDtypeStruct((t, q_dim), F32),
            jax.ShapeDtypeStruct((t, kv_dim), F32),
            jax.ShapeDtypeStruct((t, kv_dim), BF16),
            jax.ShapeDtypeStruct((t, 2 * d), BF16),
        ],
        scratch_shapes=[pltpu.VMEM((tm, d), BF16)],
        compiler_params=_params(2),
        name="inproj",
    )(xa, xb, gain, w_in)


def _head_norm(a, gain):
    return a * lax.rsqrt(jnp.mean(a * a, axis=-1, keepdims=True) + NORM_EPS) * gain


def _rope(y, cos, sin):
    lo_half = (lax.broadcasted_iota(I32, y.shape, 1) & ROPE_HALF) == 0
    partner = jnp.where(lo_half, pltpu.roll(y, HEAD_DIM - ROPE_HALF, 1), pltpu.roll(y, ROPE_HALF, 1))
    return y * cos + partner * sin


def _attn_body(q_ref, k_ref, v_ref, km_ref, vm_ref, qg_ref, kg_ref, cq_ref, sq_ref, ck_ref, sk_ref, *rest):
    o_ref, kb_ref, kmb_ref, ve_ref, vme_ref, s_ref = rest[-6:]
    tq, n_tok = q_ref.shape[0], k_ref.shape[0]
    tk = min(n_tok, ATTN_KV_CHUNK)
    n_chunks, lanes = n_tok // tk, tk // HEAD_DIM
    nt = (((1,), (1,)), ((), ()))

    @pl.when(pl.program_id(2) == 0)
    def _prepare_keys():
        kb_ref[...] = _rope(_head_norm(k_ref[...], kg_ref[...]), ck_ref[...], sk_ref[...]).astype(BF16)
        kmb_ref[...] = _head_norm(km_ref[...], kg_ref[...]).astype(BF16)
        ve_ref[:, :HEAD_DIM] = v_ref[...]
        ve_ref[:, HEAD_DIM:] = jnp.ones((n_tok, HEAD_DIM), BF16)
        vme_ref[:, :HEAD_DIM] = vm_ref[...]
        vme_ref[:, HEAD_DIM:] = jnp.ones((vm_ref.shape[0], HEAD_DIM), BF16)

    cq, sq, qg = cq_ref[...], sq_ref[...], qg_ref[...]

    def scores(h):
        qh = _rope(_head_norm(q_ref[:, h * HEAD_DIM:(h + 1) * HEAD_DIM], qg), cq, sq).astype(BF16)
        mp = None
        for c in range(n_chunks):
            s = lax.dot_general(qh, kb_ref[c * tk:(c + 1) * tk, :], nt, preferred_element_type=F32)
            s_ref[h, :, c * tk:(c + 1) * tk] = s
            for j in range(lanes):
                t = s[:, j * HEAD_DIM:(j + 1) * HEAD_DIM]
                mp = t if mp is None else jnp.maximum(mp, t)
            yield None
        sm = lax.dot_general(qh, kmb_ref[...], nt, preferred_element_type=F32)
        m = jnp.maximum(jnp.max(mp, axis=-1, keepdims=True), jnp.max(sm, axis=-1, keepdims=True))
        yield m, sm

    def outputs(h, m, sm):
        mb = jnp.broadcast_to(m, (tq, HEAD_DIM))
        pm = jnp.exp2(sm - m)
        acc = jnp.dot(pm.astype(BF16), vme_ref[...], preferred_element_type=F32)
        for c in range(n_chunks):
            tiles = []
            for j in range(lanes):
                col = c * tk + j * HEAD_DIM
                tiles.append(jnp.exp2(s_ref[h, :, col:col + HEAD_DIM] - mb).astype(BF16))
            acc = acc + jnp.dot(jnp.concatenate(tiles, axis=1), ve_ref[c * tk:(c + 1) * tk, :],
                                preferred_element_type=F32)
            yield None
        o_ref[:, h * HEAD_DIM:(h + 1) * HEAD_DIM] = (acc[:, :HEAD_DIM] / acc[:, HEAD_DIM:]).astype(o_ref.dtype)
        yield None

    prev_out = None
    for h in range(Q_PER_KV + 1):
        cur = scores(h) if h < Q_PER_KV else None
        stats = None
        for _ in range(n_chunks + 1):
            if cur is not None:
                stats = next(cur)
            if prev_out is not None:
                next(prev_out)
        prev_out = outputs(h, *stats) if cur is not None else None


def _attention(q, k, v, km, vm, qg, kg, cos, sin, prev, *, row_off, n_seq, n_tok, tq):
    t, q_dim = q.shape
    n_kv = k.shape[1] // HEAD_DIM
    n_meta = km.shape[0]
    gw = Q_PER_KV * HEAD_DIM
    assert row_off % n_tok == 0 and n_tok % tq == 0
    qmap = lambda b, kh, qi: ((row_off + b * n_tok) // tq + qi, kh)
    kmap = lambda b, kh, qi: (row_off // n_tok + b, kh)
    mmap = lambda b, kh, qi: (0, kh)
    const = lambda b, kh, qi: (0, 0)
    qpos = lambda b, kh, qi: (qi, 0)
    in_specs = [
        pl.BlockSpec((tq, gw), qmap),
        pl.BlockSpec((n_tok, HEAD_DIM), kmap),
        pl.BlockSpec((n_tok, HEAD_DIM), kmap),
        pl.BlockSpec((n_meta, HEAD_DIM), mmap),
        pl.BlockSpec((n_meta, HEAD_DIM), mmap),
        pl.BlockSpec((1, HEAD_DIM), const),
        pl.BlockSpec((1, HEAD_DIM), const),
        pl.BlockSpec((tq, HEAD_DIM), qpos),
        pl.BlockSpec((tq, HEAD_DIM), qpos),
        pl.BlockSpec((n_tok, HEAD_DIM), const),
        pl.BlockSpec((n_tok, HEAD_DIM), const),
    ]
    args = [q, k, v, km, vm, qg, kg, cos, sin, cos, sin]
    aliases = {}
    if prev is not None:
        in_specs.append(pl.BlockSpec(memory_space=pl.ANY))
        args.append(prev)
        aliases = {len(args) - 1: 0}
    return pl.pallas_call(
        _attn_body,
        grid=(n_seq, n_kv, n_tok // tq),
        in_specs=in_specs,
        out_specs=pl.BlockSpec((tq, gw), qmap),
        out_shape=jax.ShapeDtypeStruct((t, q_dim), BF16),
        scratch_shapes=[
            pltpu.VMEM((n_tok, HEAD_DIM), BF16),
            pltpu.VMEM((n_meta, HEAD_DIM), BF16),
            pltpu.VMEM((n_tok, 2 * HEAD_DIM), BF16),
            pltpu.VMEM((n_meta, 2 * HEAD_DIM), BF16),
            pltpu.VMEM((Q_PER_KV, tq, n_tok), F32),
        ],
        input_output_aliases=aliases,
        compiler_params=_params(3),
        name="attention",
    )(*args)


def _dftgen_body(tac_ref, tas_ref, tbc_ref, tbs_ref, c_ref, ns_ref):
    ac, asn = tac_ref[0], tas_ref[0]
    bc, bsn = tbc_ref[...], tbs_ref[...]
    c_ref[...] = (ac * bc - asn * bsn).astype(BF16)
    ns_ref[...] = (-(asn * bc + ac * bsn)).astype(BF16)


def _seq_dft(n_tok, n_meta):
    length = n_tok + n_meta
    rb = min(n_tok, HEAD_DIM)
    w = 2.0 * math.pi / length
    scale = 1.0 / math.sqrt(length)
    p_real = n_meta + jnp.arange(n_tok, dtype=I32)

    def angles(p_rows, p_cols):
        return ((p_rows[:, None] * p_cols[None, :]) % length).astype(F32) * w

    ang_a = angles(n_meta + rb * jnp.arange(n_tok // rb, dtype=I32), p_real)
    ang_b = angles(jnp.arange(rb, dtype=I32), p_real)
    ang_m = angles(p_real, jnp.arange(n_meta, dtype=I32))
    tac = (jnp.cos(ang_a) * scale).reshape(n_tok // rb, 1, n_tok)
    tas = (jnp.sin(ang_a) * scale).reshape(n_tok // rb, 1, n_tok)
    blk = pl.BlockSpec((1, 1, n_tok), lambda a: (a, 0, 0))
    full = pl.BlockSpec((rb, n_tok), lambda a: (0, 0))
    cmat, nsmat = pl.pallas_call(
        _dftgen_body,
        grid=(n_tok // rb,),
        in_specs=[blk, blk, full, full],
        out_specs=[pl.BlockSpec((rb, n_tok), lambda a: (a, 0))] * 2,
        out_shape=[jax.ShapeDtypeStruct((n_tok, n_tok), BF16)] * 2,
        compiler_params=_params(1),
        name="dft_matrices",
    )(tac, tas, jnp.cos(ang_b), jnp.sin(ang_b))
    return cmat, nsmat, (jnp.cos(ang_m) * scale).astype(BF16), (-jnp.sin(ang_m) * scale).astype(BF16)


def _chan_dft(cg):
    idx = jnp.arange(cg, dtype=I32)
    ang = ((idx[:, None] * idx[None, :]) % cg).astype(F32) * (2.0 * math.pi / cg)
    scale = 1.0 / math.sqrt(cg)
    return (jnp.cos(ang) * scale).astype(BF16), (jnp.sin(ang) * scale).astype(BF16)


def _fourier_body(c_ref, ns_ref, cm_ref, nsm_ref, u_ref, um_ref, cdc_ref, cds_ref, *rest):
    z_ref = rest[-1]
    u, um = u_ref[...], um_ref[...]
    a = jnp.dot(c_ref[...], u, preferred_element_type=F32) + jnp.dot(cm_ref[...], um, preferred_element_type=F32)
    b = jnp.dot(ns_ref[...], u, preferred_element_type=F32) + jnp.dot(nsm_ref[...], um, preferred_element_type=F32)
    a, b = a.astype(BF16), b.astype(BF16)
    cg = cdc_ref.shape[0]
    for g in range(z_ref.shape[1] // cg):
        sl = slice(g * cg, (g + 1) * cg)
        z = jnp.dot(a[:, sl], cdc_ref[...], preferred_element_type=F32)
        z = z + jnp.dot(b[:, sl], cds_ref[...], preferred_element_type=F32)
        z_ref[:, sl] = z.astype(z_ref.dtype)


def _fourier(uf, ufm, mats, cdc, cds, prev, *, row_off, n_seq, n_tok, tl, tf):
    cmat, nsmat, cmeta, nsmeta = mats
    t, f_dim = uf.shape
    n_meta = ufm.shape[0]
    cg = cdc.shape[0]
    assert row_off % n_tok == 0 and n_tok % tl == 0 and f_dim % tf == 0 and tf % cg == 0
    cmap = lambda b, j, i: (i, 0)
    const = lambda b, j, i: (0, 0)
    in_specs = [
        pl.BlockSpec((tl, n_tok), cmap),
        pl.BlockSpec((tl, n_tok), cmap),
        pl.BlockSpec((tl, n_meta), cmap),
        pl.BlockSpec((tl, n_meta), cmap),
        pl.BlockSpec((n_tok, tf), lambda b, j, i: (row_off // n_tok + b, j)),
        pl.BlockSpec((n_meta, tf), lambda b, j, i: (0, j)),
        pl.BlockSpec((cg, cg), const),
        pl.BlockSpec((cg, cg), const),
    ]
    args = [cmat, nsmat, cmeta, nsmeta, uf, ufm, cdc, cds]
    aliases = {}
    if prev is not None:
        in_specs.append(pl.BlockSpec(memory_space=pl.ANY))
        args.append(prev)
        aliases = {len(args) - 1: 0}
    return pl.pallas_call(
        _fourier_body,
        grid=(n_seq, f_dim // tf, n_tok // tl),
        in_specs=in_specs,
        out_specs=pl.BlockSpec((tl, tf), lambda b, j, i: ((row_off + b * n_tok) // tl + i, j)),
        out_shape=jax.ShapeDtypeStruct((t, f_dim), BF16),
        input_output_aliases=aliases,
        compiler_params=_params(3),
        name="fourier",
    )(*args)


def _merge_body(a_ref, z_ref, ga_ref, gf_ref, wa_ref, wf_ref, o_ref):
    a = jnp.dot(a_ref[...], wa_ref[...], preferred_element_type=F32)
    f = jnp.dot(z_ref[...], wf_ref[...], preferred_element_type=F32)
    sga = jax.nn.sigmoid(ga_ref[...].astype(F32))
    sgf = jax.nn.sigmoid(gf_ref[...].astype(F32))
    o_ref[...] = (sga * a + sgf * f).astype(o_ref.dtype)


def _merge(attn, z, g, wa, wf, *, tm):
    t, d = attn.shape
    f_dim = z.shape[1]
    return pl.pallas_call(
        _merge_body,
        grid=(t // tm,),
        in_specs=[
            pl.BlockSpec((tm, d), lambda i: (i, 0)),
            pl.BlockSpec((tm, f_dim), lambda i: (i, 0)),
            pl.BlockSpec((tm, d), lambda i: (i, 0)),
            pl.BlockSpec((tm, d), lambda i: (i, 1)),
            _resident((d, d)),
            _resident((f_dim, d)),
        ],
        out_specs=pl.BlockSpec((tm, d), lambda i: (i, 0)),
        out_shape=jax.ShapeDtypeStruct((t, d), BF16),
        compiler_params=_params(1),
        name="merge",
    )(attn, z, g, g, wa, wf)


def _route_body(m_ref, xa_ref, xb_ref, wo_ref, gain_ref, wr_ref, tri_ref,
                hmid_ref, hn_ref, eid_ref, cw_ref, rank_ref, cnt_ref, *, n_a_tiles, n_groups):
    i = pl.program_id(0)
    tm, d = xa_ref.shape
    dh = d // 2
    n_exp = cnt_ref.shape[0]

    @pl.when(i == 0)
    def _init():
        cnt_ref[...] = jnp.zeros_like(cnt_ref)

    x = jnp.where(i < n_a_tiles, xa_ref[...], xb_ref[...])
    h = x + jnp.dot(m_ref[...], wo_ref[...], preferred_element_type=F32)
    hmid_ref[...] = h
    hn = h * lax.rsqrt(jnp.mean(h * h, axis=-1, keepdims=True) + NORM_EPS) * gain_ref[...]
    hb = hn.astype(BF16)
    hn_ref[...] = _pack_bf16_pair(hn[:, :dh], hn[:, dh:])

    lt = lax.dot_general(wr_ref[...], hb, (((1,), (1,)), ((), ())), preferred_element_type=F32)
    row8 = lax.broadcasted_iota(I32, (EXPERTS_PER_GROUP, tm), 0)
    neg = jnp.float32(-jnp.inf)
    lg = jnp.where(row8 < n_groups, lt[0:8], neg)
    gmax = jnp.max(lg, axis=0, keepdims=True)
    gidx = jnp.min(jnp.where(lg == gmax, row8, 8), axis=0, keepdims=True)
    p_g = 1.0 / jnp.sum(jnp.exp(lg - gmax), axis=0, keepdims=True)
    sel = lt[8:8 + EXPERTS_PER_GROUP]
    for g in range(1, n_groups):
        sel = jnp.where(gidx == g, lt[8 + g * EXPERTS_PER_GROUP:8 + (g + 1) * EXPERTS_PER_GROUP], sel)
    m1 = jnp.max(sel, axis=0, keepdims=True)
    i1 = jnp.min(jnp.where(sel == m1, row8, 8), axis=0, keepdims=True)
    sel2 = jnp.where(row8 == i1, neg, sel)
    m2 = jnp.max(sel2, axis=0, keepdims=True)
    i2 = jnp.min(jnp.where(sel2 == m2, row8, 8), axis=0, keepdims=True)
    e21 = jnp.exp(m2 - m1)
    p1 = 1.0 / (1.0 + e21)
    p2 = e21 * p1
    e1 = gidx * EXPERTS_PER_GROUP + i1
    e2 = gidx * EXPERTS_PER_GROUP + i2
    eid_ref[0, 0:1, :] = e1
    eid_ref[0, 1:2, :] = e2
    cw_ref[0, 0:1, :] = p_g * p1
    cw_ref[0, 1:2, :] = p_g * p2

    row_e = lax.broadcasted_iota(I32, (n_exp, tm), 0)
    oh1 = (row_e == e1)
    oh2 = (row_e == e2)
    cs1 = jnp.dot(oh1.astype(BF16), tri_ref[...], preferred_element_type=F32)
    cs2 = jnp.dot(oh2.astype(BF16), tri_ref[...], preferred_element_type=F32)
    oh1f, oh2f = oh1.astype(F32), oh2.astype(F32)
    c1 = jnp.sum(oh1f, axis=1, keepdims=True)
    c2 = jnp.sum(oh2f, axis=1, keepdims=True)
    base = cnt_ref[:, 0:1]
    rank_ref[0, 0:1, :] = jnp.sum(oh1f * (base + cs1), axis=0, keepdims=True).astype(I32)
    rank_ref[0, 1:2, :] = jnp.sum(oh2f * (base + c1 + cs2), axis=0, keepdims=True).astype(I32)
    cnt_ref[...] = cnt_ref[...] + (c1 + c2)


def _outproj_route(merged, xa, xb, wout, gain, wr_t, tri, *, tm, n_groups, n_exp):
    t, d = merged.shape
    nt = t // tm
    na = xa.shape[0] // tm
    tok3 = lambda i: (i, 0, 0)
    body = functools.partial(_route_body, n_a_tiles=na, n_groups=n_groups)
    return pl.pallas_call(
        body,
        grid=(nt,),
        in_specs=[
            pl.BlockSpec((tm, d), lambda i: (i, 0)),
            pl.BlockSpec((tm, d), lambda i: (jnp.minimum(i, na - 1), 0)),
            pl.BlockSpec((tm, d), lambda i: (jnp.maximum(i - na, 0), 0)),
            _resident((d, d)),
            pl.BlockSpec((1, d), lambda i: (0, 0)),
            _resident(wr_t.shape),
            _resident((tm, tm)),
        ],
        out_specs=[
            pl.BlockSpec((tm, d), lambda i: (i, 0)),
            pl.BlockSpec((tm, d // 2), lambda i: (i, 0)),
            pl.BlockSpec((1, TOP_K, tm), tok3),
            pl.BlockSpec((1, TOP_K, tm), tok3),
            pl.BlockSpec((1, TOP_K, tm), tok3),
            pl.BlockSpec((n_exp, HEAD_DIM), lambda i: (0, 0)),
        ],
        out_shape=[
            jax.ShapeDtypeStruct((t, d), F32),
            jax.ShapeDtypeStruct((t, d // 2), U32),
            jax.ShapeDtypeStruct((nt, TOP_K, tm), I32),
            jax.ShapeDtypeStruct((nt, TOP_K, tm), F32),
            jax.ShapeDtypeStruct((nt, TOP_K, tm), I32),
            jax.ShapeDtypeStruct((n_exp, HEAD_DIM), F32),
        ],
        compiler_params=_params(1),
        name="outproj_route",
    )(merged, xa, xb, wout, gain, wr_t, tri)


def _dispatch_body(pad_end_ref, dest_ref, x_ref, xs_ref, zbuf_ref, zsem, sem, *, bm, n_exp):
    i = pl.program_id(0)
    tm = x_ref.shape[0]

    @pl.when(i == 0)
    def _zero_padding():
        zbuf_ref[...] = jnp.zeros_like(zbuf_ref)
        for e in range(n_exp):
            end = pad_end_ref[e]
            start = pad_end_ref[e - 1] if e > 0 else 0

            @pl.when(end > start)
            def _():
                cp = pltpu.make_async_copy(zbuf_ref, xs_ref.at[pl.ds(pl.multiple_of(end - bm, bm), bm)], zsem)
                cp.start()
                cp.wait()

    def issue(r, carry):
        for k in range(TOP_K):
            d = dest_ref[0, k, r]
            pltpu.make_async_copy(x_ref.at[pl.ds(r, 1)], xs_ref.at[pl.ds(d, 1)], sem).start()
        return carry

    lax.fori_loop(0, tm, issue, 0, unroll=8)
    for k in range(TOP_K):
        pltpu.make_async_copy(x_ref, xs_ref.at[pl.ds(0, tm)], sem).wait()


def _dispatch(hn2p, dest, pad_end, *, tm, bm, n_rows):
    t, dh = hn2p.shape
    n_exp = pad_end.shape[0]
    body = functools.partial(_dispatch_body, bm=bm, n_exp=n_exp)
    return pl.pallas_call(
        body,
        grid_spec=pltpu.PrefetchScalarGridSpec(
            num_scalar_prefetch=1,
            grid=(t // tm,),
            in_specs=[
                pl.BlockSpec((1, TOP_K, tm), lambda i, pe: (i, 0, 0), memory_space=pltpu.SMEM),
                pl.BlockSpec((tm, dh), lambda i, pe: (i, 0)),
            ],
            out_specs=pl.BlockSpec(memory_space=pl.ANY),
            scratch_shapes=[pltpu.VMEM((bm, dh), U32), pltpu.SemaphoreType.DMA(()), pltpu.SemaphoreType.DMA(())],
        ),
        out_shape=jax.ShapeDtypeStruct((n_rows, dh), U32),
        compiler_params=_params(1),
        name="dispatch",
    )(pad_end, dest, hn2p)


def _ffn_body(bexp_ref, nvalid_ref, xs_ref, wg_ref, wu_ref, wd_ref, ys_ref):
    b = pl.program_id(0)

    @pl.when(b < nvalid_ref[0])
    def _():
        dh = xs_ref.shape[1]
        lo, hi = _unpack_bf16_pair(xs_ref[...])
        lo, hi = lo.astype(BF16), hi.astype(BF16)
        g = jnp.dot(lo, wg_ref[0, :dh, :], preferred_element_type=F32)
        g = g + jnp.dot(hi, wg_ref[0, dh:, :], preferred_element_type=F32)
        u = jnp.dot(lo, wu_ref[0, :dh, :], preferred_element_type=F32)
        u = u + jnp.dot(hi, wu_ref[0, dh:, :], preferred_element_type=F32)
        hid = (jax.nn.silu(g) * u).astype(BF16)
        y = jnp.dot(hid, wd_ref[0], preferred_element_type=F32)
        ys_ref[...] = _pack_bf16_pair(y[:, :dh], y[:, dh:])


def _expert_ffn(xs, wg, wu, wd, block_expert, n_valid, *, bm):
    n_rows, dh = xs.shape
    n_exp, d, de = wg.shape
    nb = n_rows // bm
    blk = lambda b, be, nv: (jnp.minimum(b, nv[0] - 1), 0)
    wmap = lambda b, be, nv: (be[jnp.minimum(b, nv[0] - 1)], 0, 0)
    return pl.pallas_call(
        _ffn_body,
        grid_spec=pltpu.PrefetchScalarGridSpec(
            num_scalar_prefetch=2,
            grid=(nb,),
            in_specs=[
                pl.BlockSpec((bm, dh), blk),
                pl.BlockSpec((1, d, de), wmap),
                pl.BlockSpec((1, d, de), wmap),
                pl.BlockSpec((1, de, d), wmap),
            ],
            out_specs=pl.BlockSpec((bm, dh), blk),
        ),
        out_shape=jax.ShapeDtypeStruct((n_rows, dh), U32),
        compiler_params=_params(1),
        name="expert_ffn",
    )(block_expert, n_valid, xs, wg, wu, wd)


def _combine_body(dest_ref, h_ref, cw_ref, gain_ref, ys_ref, o_ref, buf_ref, sem):
    tm, d = h_ref.shape
    dh = d // 2

    def issue(r, carry):
        for k in range(TOP_K):
            s = dest_ref[0, k, r]
            pltpu.make_async_copy(ys_ref.at[pl.ds(s, 1)], buf_ref.at[k, pl.ds(r, 1)], sem).start()
        return carry

    lax.fori_loop(0, tm, issue, 0, unroll=8)
    for k in range(TOP_K):
        pltpu.make_async_copy(ys_ref.at[pl.ds(0, tm)], buf_ref.at[k], sem).wait()

    lo1, hi1 = _unpack_bf16_pair(buf_ref[0])
    lo2, hi2 = _unpack_bf16_pair(buf_ref[1])
    w1 = cw_ref[:, 0:1]
    w2 = cw_ref[:, 1:2]
    o_lo = h_ref[:, :dh] + (w1 * lo1 + w2 * lo2)
    o_hi = h_ref[:, dh:] + (w1 * hi1 + w2 * hi2)
    ms = (jnp.sum(o_lo * o_lo, axis=-1, keepdims=True) + jnp.sum(o_hi * o_hi, axis=-1, keepdims=True)) / d
    r = lax.rsqrt(ms + NORM_EPS)
    o_ref[:, :dh] = o_lo * r * gain_ref[:, :dh]
    o_ref[:, dh:] = o_hi * r * gain_ref[:, dh:]


def _combine(ys, dest, hmid, cw_t, gain, *, tm, row_off, n_rows):
    d = hmid.shape[1]
    off = row_off // tm
    return pl.pallas_call(
        _combine_body,
        grid=(n_rows // tm,),
        in_specs=[
            pl.BlockSpec((1, TOP_K, tm), lambda i: (off + i, 0, 0), memory_space=pltpu.SMEM),
            pl.BlockSpec((tm, d), lambda i: (off + i, 0)),
            pl.BlockSpec((tm, TOP_K), lambda i: (off + i, 0)),
            pl.BlockSpec((1, d), lambda i: (0, 0)),
            pl.BlockSpec(memory_space=pl.ANY),
        ],
        out_specs=pl.BlockSpec((tm, d), lambda i: (i, 0)),
        out_shape=jax.ShapeDtypeStruct((n_rows, d), F32),
        scratch_shapes=[pltpu.VMEM((TOP_K, tm, d // 2), U32), pltpu.SemaphoreType.DMA(())],
        compiler_params=_params(1),
        name="combine",
    )(dest, hmid, cw_t, gain, ys)


def _rope_tables(n_tokens):
    rows = n_tokens // GRID_W
    row = jnp.repeat(jnp.arange(rows, dtype=F32), GRID_W)
    col = jnp.tile(jnp.arange(GRID_W, dtype=F32), rows)
    inv_freq = ROPE_THETA ** (-jnp.arange(ROPE_HALF, dtype=F32) / ROPE_HALF)
    ar, ac = row[:, None] * inv_freq, col[:, None] * inv_freq
    cos = jnp.concatenate([jnp.cos(ar), jnp.cos(ar), jnp.cos(ac), jnp.cos(ac)], axis=1)
    sin = jnp.concatenate([-jnp.sin(ar), jnp.sin(ar), -jnp.sin(ac), jnp.sin(ac)], axis=1)
    return cos, sin


def _tile(n, pref):
    return min(n, pref)


def kernel(x_prompt, x_sample, meta_tokens, mix_norm, w_in, q_gain, k_gain, w_attn_o, w_fourier_o, w_out,
           moe_norm, w_router_group, w_router_expert, w_expert_gate, w_expert_up, w_expert_down, final_norm):
    bp, n_p, d = x_prompt.shape
    bs, n_s, _ = x_sample.shape
    n_meta = meta_tokens.shape[0]
    t_p, t_s = bp * n_p, bs * n_s
    t = t_p + t_s
    f_dim = d // 2
    cg = f_dim // N_FOURIER_GROUPS
    n_groups = w_router_group.shape[-1]
    n_exp = w_router_expert.shape[-1]
    assert n_exp == n_groups * EXPERTS_PER_GROUP and n_groups <= 8

    tm = _tile(math.gcd(n_p, n_s), 512)
    bm = _tile(t, 512)

    w_in_b = w_in[0].astype(BF16)
    wa_b, wf_b, wo_b = w_attn_o[0].astype(BF16), w_fourier_o[0].astype(BF16), w_out[0].astype(BF16)
    wg_b, wu_b, wd_b = (w_expert_gate[0].astype(BF16), w_expert_up[0].astype(BF16),
                        w_expert_down[0].astype(BF16))
    qg = (q_gain[0] * (HEAD_DIM ** -0.5 * LOG2E)).reshape(1, HEAD_DIM)
    kg = k_gain[0].reshape(1, HEAD_DIM)
    wr_t = jnp.zeros((8 + n_exp, d), F32)
    wr_t = wr_t.at[:n_groups].set(w_router_group[0].T).at[8:].set(w_router_expert[0].T).astype(BF16)
    tri = (jnp.arange(tm)[:, None] < jnp.arange(tm)[None, :]).astype(BF16)
    cos, sin = _rope_tables(max(n_p, n_s))
    cdc, cds = _chan_dft(cg)
    gain_in = mix_norm[0].reshape(1, d)

    xp, xs_in = x_prompt.reshape(t_p, d), x_sample.reshape(t_s, d)
    uf, q, k, v, g = _inproj(xp, xs_in, gain_in, w_in_b, tm=tm)
    ufm, _, km, vm, _ = _inproj(meta_tokens, meta_tokens, gain_in, w_in_b, tm=n_meta)
    ufm, km, vm = ufm[:n_meta], km[:n_meta], vm[:n_meta]

    attn, z = None, None
    for row_off, n_seq, n_tok in ((0, bp, n_p), (t_p, bs, n_s)):
        attn = _attention(q, k, v, km, vm, qg, kg, cos, sin, attn, row_off=row_off, n_seq=n_seq,
                          n_tok=n_tok, tq=_tile(n_tok, 256))
        z = _fourier(uf, ufm, _seq_dft(n_tok, n_meta), cdc, cds, z, row_off=row_off, n_seq=n_seq,
                     n_tok=n_tok, tl=_tile(n_tok, 512), tf=_tile(f_dim, 512))

    merged = _merge(attn, z, g, wa_b, wf_b, tm=tm)
    hmid, hn2p, eid, cw, rank, counts = _outproj_route(
        merged, xp, xs_in, wo_b, moe_norm[0].reshape(1, d), wr_t, tri, tm=tm, n_groups=n_groups, n_exp=n_exp)

    cnt = counts[:, 0].astype(I32)
    padded = (cnt + bm - 1) // bm * bm
    pad_end = jnp.cumsum(padded).astype(I32)
    pad_start = pad_end - padded
    onehot = eid[..., None] == jnp.arange(n_exp, dtype=I32)
    dest = jnp.sum(jnp.where(onehot, pad_start, 0), axis=-1) + rank
    nb = (t * TOP_K) // bm + n_exp
    n_valid = (pad_end[-1:] // bm).astype(I32)
    blk_start = jnp.arange(nb, dtype=I32) * bm
    block_expert = jnp.minimum(jnp.sum(pad_end[None, :] <= blk_start[:, None], axis=1), n_exp - 1).astype(I32)

    xs = _dispatch(hn2p, dest, pad_end, tm=tm, bm=bm, n_rows=nb * bm)
    ys = _expert_ffn(xs, wg_b, wu_b, wd_b, block_expert, n_valid, bm=bm)
    cw_t = jnp.transpose(cw, (0, 2, 1)).reshape(t, TOP_K)
    fgain = final_norm.reshape(1, d)
    y_p = _combine(ys, dest, hmid, cw_t, fgain, tm=tm, row_off=0, n_rows=t_p)
    y_s = _combine(ys, dest, hmid, cw_t, fgain, tm=tm, row_off=t_p, n_rows=t_s)
    return y_p.reshape(bp, n_p, d), y_s.reshape(bs, n_s, d)
```

```python
import functools
import math

import jax
import jax.numpy as jnp
from jax import lax
from jax.experimental import pallas as pl
from jax.experimental.pallas import tpu as pltpu

F32, BF16, I32, U32 = jnp.float32, jnp.bfloat16, jnp.int32, jnp.uint32

GRID_W = 64
HEAD_DIM = 128
ROPE_HALF = 32
Q_PER_KV = 4
N_FOURIER_GROUPS = 4
ROPE_THETA = 10000.0
NORM_EPS = 1e-6
TOP_K = 2
EXPERTS_PER_GROUP = 8
LOG2E = 1.4426950408889634
ATTN_KV_CHUNK = 512
ATTN_UNIT_ROWS = 256
ATTN_Q_TILE = 512
INPROJ_ROWS = 1024
SUBLANES = 8
ROW_DMA_UNROLL = 32

VMEM_LIMIT_BYTES = 56 * 1024 * 1024


def _params(n_axes):
    return pltpu.CompilerParams(dimension_semantics=("arbitrary",) * n_axes,
                                vmem_limit_bytes=VMEM_LIMIT_BYTES)


def _resident(shape):
    return pl.BlockSpec(shape, lambda *_: (0,) * len(shape), pipeline_mode=pl.Buffered(1))


def _pack_bf16_pair(lo, hi):
    lo_bits = lax.bitcast_convert_type(lo.astype(BF16).astype(F32), U32) >> 16
    hi_bits = lax.bitcast_convert_type(hi.astype(BF16).astype(F32), U32) & jnp.uint32(0xFFFF0000)
    return lo_bits | hi_bits


def _unpack_bf16_pair(w):
    lo = lax.bitcast_convert_type(w << 16, F32)
    hi = lax.bitcast_convert_type(w & jnp.uint32(0xFFFF0000), F32)
    return lo, hi


_F_TILES, _Q_TILES, _G_TILES = 2, 4, 8
_N_TILES = _F_TILES + _Q_TILES + 2 + _G_TILES


def _inproj_body(x_ref, gain_ref, w_ref, *rest):
    uf_ref, q_ref, k_ref, v_ref, g_ref, hn_ref = rest[-6:]
    n = pl.program_id(1)

    @pl.when(n == 0)
    def _norm():
        x = x_ref[...]
        y = x * lax.rsqrt(jnp.mean(x * x, axis=-1, keepdims=True) + NORM_EPS)
        hn_ref[...] = (y * gain_ref[...]).astype(BF16)

    acc = jnp.dot(hn_ref[...], w_ref[...], preferred_element_type=F32)
    q0 = _F_TILES
    k0 = q0 + _Q_TILES

    @pl.when(n < q0)
    def _fourier():
        uf_ref[...] = acc.astype(BF16)

    @pl.when((n >= q0) & (n < k0))
    def _q():
        q_ref[...] = acc

    @pl.when(n == k0)
    def _k():
        k_ref[...] = acc

    @pl.when(n == k0 + 1)
    def _v():
        v_ref[...] = acc.astype(BF16)

    @pl.when(n > k0 + 1)
    def _gates():
        g_ref[...] = acc.astype(BF16)


def _inproj(x, gain, w_in, prev, *, tm, row_off, t_total):
    tx, d = x.shape
    tn = d // 4
    f_dim, q_dim, kv_dim = d // 2, d, d // 4
    assert w_in.shape == (d, _N_TILES * tn) and tx % tm == 0 and row_off % tm == 0
    ob = row_off // tm
    q0, k0, g0 = _F_TILES, _F_TILES + _Q_TILES, _F_TILES + _Q_TILES + 2
    in_specs = [
        pl.BlockSpec((tm, d), lambda i, n: (i, 0)),
        pl.BlockSpec((1, d), lambda i, n: (0, 0)),
        pl.BlockSpec((d, tn), lambda i, n: (0, n)),
    ]
    args = [x, gain, w_in]
    aliases = {}
    if prev is not None:
        in_specs += [pl.BlockSpec(memory_space=pl.ANY)] * len(prev)
        aliases = {len(args) + j: j for j in range(len(prev))}
        args += list(prev)
    return pl.pallas_call(
        _inproj_body,
        grid=(tx // tm, _N_TILES),
        in_specs=in_specs,
        out_specs=[
            pl.BlockSpec((tm, tn), lambda i, n: (ob + i, jnp.clip(n, 0, _F_TILES - 1))),
            pl.BlockSpec((tm, tn), lambda i, n: (ob + i, jnp.clip(n - q0, 0, _Q_TILES - 1))),
            pl.BlockSpec((tm, tn), lambda i, n: (ob + i, 0)),
            pl.BlockSpec((tm, tn), lambda i, n: (ob + i, 0)),
            pl.BlockSpec((tm, tn), lambda i, n: (ob + i, jnp.clip(n - g0, 0, _G_TILES - 1))),
        ],
        out_shape=[
            jax.ShapeDtypeStruct((t_total, f_dim), BF16),
            jax.ShapeDtypeStruct((t_total, q_dim), F32),
            jax.ShapeDtypeStruct((t_total, kv_dim), F32),
            jax.ShapeDtypeStruct((t_total, kv_dim), BF16),
            jax.ShapeDtypeStruct((t_total, 2 * d), BF16),
        ],
        scratch_shapes=[pltpu.VMEM((tm, d), BF16)],
        input_output_aliases=aliases,
        compiler_params=_params(2),
        name="inproj",
    )(*args)


def _head_norm(a, gain):
    return a * lax.rsqrt(jnp.mean(a * a, axis=-1, keepdims=True) + NORM_EPS) * gain


def _rope(y, cos, sin):
    lo_half = (lax.broadcasted_iota(I32, y.shape, 1) & ROPE_HALF) == 0
    partner = jnp.where(lo_half, pltpu.roll(y, HEAD_DIM - ROPE_HALF, 1), pltpu.roll(y, ROPE_HALF, 1))
    return y * cos + partner * sin


def _attn_body(q_ref, k_ref, v_ref, km_ref, vm_ref, qg_ref, kg_ref, cq_ref, sq_ref, ck_ref, sk_ref, *rest):
    o_ref, kb_ref, kmb_ref, ve_ref, vme_ref, s_ref = rest[-6:]
    tq, n_tok = q_ref.shape[0], k_ref.shape[0]
    tk = min(n_tok, ATTN_KV_CHUNK)
    n_chunks, lanes = n_tok // tk, tk // HEAD_DIM
    nt = (((1,), (1,)), ((), ()))

    @pl.when(pl.program_id(2) == 0)
    def _prepare_keys():
        kb_ref[...] = _rope(_head_norm(k_ref[...], kg_ref[...]), ck_ref[...], sk_ref[...]).astype(BF16)
        kmb_ref[...] = _head_norm(km_ref[...], kg_ref[...]).astype(BF16)
        ve_ref[:, :HEAD_DIM] = v_ref[...]
        ve_ref[:, HEAD_DIM:] = jnp.ones((n_tok, HEAD_DIM), BF16)
        vme_ref[:, :HEAD_DIM] = vm_ref[...]
        vme_ref[:, HEAD_DIM:] = jnp.ones((vm_ref.shape[0], HEAD_DIM), BF16)

    qg = qg_ref[...]
    ru = min(tq, ATTN_UNIT_ROWS)
    units = [(r0, h) for r0 in range(0, tq, ru) for h in range(Q_PER_KV)]

    def scores(u):
        r0, h = units[u]
        rows = slice(r0, r0 + ru)
        qh = _rope(_head_norm(q_ref[rows, h * HEAD_DIM:(h + 1) * HEAD_DIM], qg), cq_ref[rows, :], sq_ref[rows, :])
        qh = qh.astype(BF16)
        mp = None
        for c in range(n_chunks):
            s = lax.dot_general(qh, kb_ref[c * tk:(c + 1) * tk, :], nt, preferred_element_type=F32)
            s_ref[u % 2, :, c * tk:(c + 1) * tk] = s
            for j in range(lanes):
                t = s[:, j * HEAD_DIM:(j + 1) * HEAD_DIM]
                mp = t if mp is None else jnp.maximum(mp, t)
            yield None
        sm = lax.dot_general(qh, kmb_ref[...], nt, preferred_element_type=F32)
        m = jnp.maximum(jnp.max(mp, axis=-1, keepdims=True), jnp.max(sm, axis=-1, keepdims=True))
        yield m, sm

    def outputs(u, m, sm):
        r0, h = units[u]
        mb = jnp.broadcast_to(m, (ru, HEAD_DIM))
        pm = jnp.exp2(sm - m)
        acc = jnp.dot(pm.astype(BF16), vme_ref[...], preferred_element_type=F32)
        for c in range(n_chunks):
            tiles = []
            for j in range(lanes):
                col = c * tk + j * HEAD_DIM
                tiles.append(jnp.exp2(s_ref[u % 2, :, col:col + HEAD_DIM] - mb).astype(BF16))
            acc = acc + jnp.dot(jnp.concatenate(tiles, axis=1), ve_ref[c * tk:(c + 1) * tk, :],
                                preferred_element_type=F32)
            yield None
        o = acc[:, :HEAD_DIM] / acc[:, HEAD_DIM:]
        o_ref[r0:r0 + ru, h * HEAD_DIM:(h + 1) * HEAD_DIM] = o.astype(o_ref.dtype)
        yield None

    prev_out = None
    for u in range(len(units) + 1):
        cur = scores(u) if u < len(units) else None
        stats = None
        for _ in range(n_chunks + 1):
            if cur is not None:
                stats = next(cur)
            if prev_out is not None:
                next(prev_out)
        prev_out = outputs(u, *stats) if cur is not None else None


def _attention(q, k, v, km, vm, qg, kg, cos, sin, prev, *, row_off, n_seq, n_tok, tq):
    t, q_dim = q.shape
    n_kv = k.shape[1] // HEAD_DIM
    n_meta = km.shape[0]
    gw = Q_PER_KV * HEAD_DIM
    assert row_off % n_tok == 0 and n_tok % tq == 0
    qmap = lambda b, kh, qi: ((row_off + b * n_tok) // tq + qi, kh)
    kmap = lambda b, kh, qi: (row_off // n_tok + b, kh)
    mmap = lambda b, kh, qi: (0, kh)
    const = lambda b, kh, qi: (0, 0)
    qpos = lambda b, kh, qi: (qi, 0)
    in_specs = [
        pl.BlockSpec((tq, gw), qmap),
        pl.BlockSpec((n_tok, HEAD_DIM), kmap),
        pl.BlockSpec((n_tok, HEAD_DIM), kmap),
        pl.BlockSpec((n_meta, HEAD_DIM), mmap),
        pl.BlockSpec((n_meta, HEAD_DIM), mmap),
        pl.BlockSpec((1, HEAD_DIM), const),
        pl.BlockSpec((1, HEAD_DIM), const),
        pl.BlockSpec((tq, HEAD_DIM), qpos),
        pl.BlockSpec((tq, HEAD_DIM), qpos),
        pl.BlockSpec((n_tok, HEAD_DIM), const),
        pl.BlockSpec((n_tok, HEAD_DIM), const),
    ]
    args = [q, k, v, km, vm, qg, kg, cos, sin, cos, sin]
    aliases = {}
    if prev is not None:
        in_specs.append(pl.BlockSpec(memory_space=pl.ANY))
        args.append(prev)
        aliases = {len(args) - 1: 0}
    return pl.pallas_call(
        _attn_body,
        grid=(n_seq, n_kv, n_tok // tq),
        in_specs=in_specs,
        out_specs=pl.BlockSpec((tq, gw), qmap),
        out_shape=jax.ShapeDtypeStruct((t, q_dim), BF16),
        scratch_shapes=[
            pltpu.VMEM((n_tok, HEAD_DIM), BF16),
            pltpu.VMEM((n_meta, HEAD_DIM), BF16),
            pltpu.VMEM((n_tok, 2 * HEAD_DIM), BF16),
            pltpu.VMEM((n_meta, 2 * HEAD_DIM), BF16),
            pltpu.VMEM((2, min(tq, ATTN_UNIT_ROWS), n_tok), F32),
        ],
        input_output_aliases=aliases,
        compiler_params=_params(3),
        name="attention",
    )(*args)


def _dftgen_body(tac_ref, tas_ref, tbc_ref, tbs_ref, c_ref, ns_ref):
    ac, asn = tac_ref[0], tas_ref[0]
    bc, bsn = tbc_ref[...], tbs_ref[...]
    c_ref[...] = (ac * bc - asn * bsn).astype(BF16)
    ns_ref[...] = (-(asn * bc + ac * bsn)).astype(BF16)


def _seq_dft(n_tok, n_meta):
    length = n_tok + n_meta
    rb = min(n_tok, HEAD_DIM)
    w = 2.0 * math.pi / length
    scale = 1.0 / math.sqrt(length)
    p_real = n_meta + jnp.arange(n_tok, dtype=I32)

    def angles(p_rows, p_cols):
        return ((p_rows[:, None] * p_cols[None, :]) % length).astype(F32) * w

    ang_a = angles(n_meta + rb * jnp.arange(n_tok // rb, dtype=I32), p_real)
    ang_b = angles(jnp.arange(rb, dtype=I32), p_real)
    ang_m = angles(p_real, jnp.arange(n_meta, dtype=I32))
    tac = (jnp.cos(ang_a) * scale).reshape(n_tok // rb, 1, n_tok)
    tas = (jnp.sin(ang_a) * scale).reshape(n_tok // rb, 1, n_tok)
    blk = pl.BlockSpec((1, 1, n_tok), lambda a: (a, 0, 0))
    full = pl.BlockSpec((rb, n_tok), lambda a: (0, 0))
    cmat, nsmat = pl.pallas_call(
        _dftgen_body,
        grid=(n_tok // rb,),
        in_specs=[blk, blk, full, full],
        out_specs=[pl.BlockSpec((rb, n_tok), lambda a: (a, 0))] * 2,
        out_shape=[jax.ShapeDtypeStruct((n_tok, n_tok), BF16)] * 2,
        compiler_params=_params(1),
        name="dft_matrices",
    )(tac, tas, jnp.cos(ang_b), jnp.sin(ang_b))
    return cmat, nsmat, (jnp.cos(ang_m) * scale).astype(BF16), (-jnp.sin(ang_m) * scale).astype(BF16)


def _chan_dft(cg):
    idx = jnp.arange(cg, dtype=I32)
    ang = ((idx[:, None] * idx[None, :]) % cg).astype(F32) * (2.0 * math.pi / cg)
    scale = 1.0 / math.sqrt(cg)
    return (jnp.cos(ang) * scale).astype(BF16), (jnp.sin(ang) * scale).astype(BF16)


def _fourier_body(c_ref, ns_ref, cm_ref, nsm_ref, u_ref, um_ref, cdc_ref, cds_ref, *rest):
    z_ref = rest[-1]
    u, um = u_ref[...], um_ref[...]
    a = jnp.dot(c_ref[...], u, preferred_element_type=F32) + jnp.dot(cm_ref[...], um, preferred_element_type=F32)
    b = jnp.dot(ns_ref[...], u, preferred_element_type=F32) + jnp.dot(nsm_ref[...], um, preferred_element_type=F32)
    a, b = a.astype(BF16), b.astype(BF16)
    cg = cdc_ref.shape[0]
    for g in range(z_ref.shape[1] // cg):
        sl = slice(g * cg, (g + 1) * cg)
        z = jnp.dot(a[:, sl], cdc_ref[...], preferred_element_type=F32)
        z = z + jnp.dot(b[:, sl], cds_ref[...], preferred_element_type=F32)
        z_ref[:, sl] = z.astype(z_ref.dtype)


def _fourier(uf, ufm, mats, cdc, cds, prev, *, row_off, n_seq, n_tok, tl, tf):
    cmat, nsmat, cmeta, nsmeta = mats
    t, f_dim = uf.shape
    n_meta = ufm.shape[0]
    cg = cdc.shape[0]
    assert row_off % n_tok == 0 and n_tok % tl == 0 and f_dim % tf == 0 and tf % cg == 0
    cmap = lambda b, j, i: (i, 0)
    const = lambda b, j, i: (0, 0)
    in_specs = [
        pl.BlockSpec((tl, n_tok), cmap),
        pl.BlockSpec((tl, n_tok), cmap),
        pl.BlockSpec((tl, n_meta), cmap),
        pl.BlockSpec((tl, n_meta), cmap),
        pl.BlockSpec((n_tok, tf), lambda b, j, i: (row_off // n_tok + b, j)),
        pl.BlockSpec((n_meta, tf), lambda b, j, i: (0, j)),
        pl.BlockSpec((cg, cg), const),
        pl.BlockSpec((cg, cg), const),
    ]
    args = [cmat, nsmat, cmeta, nsmeta, uf, ufm, cdc, cds]
    aliases = {}
    if prev is not None:
        in_specs.append(pl.BlockSpec(memory_space=pl.ANY))
        args.append(prev)
        aliases = {len(args) - 1: 0}
    return pl.pallas_call(
        _fourier_body,
        grid=(n_seq, f_dim // tf, n_tok // tl),
        in_specs=in_specs,
        out_specs=pl.BlockSpec((tl, tf), lambda b, j, i: ((row_off + b * n_tok) // tl + i, j)),
        out_shape=jax.ShapeDtypeStruct((t, f_dim), BF16),
        input_output_aliases=aliases,
        compiler_params=_params(3),
        name="fourier",
    )(*args)


def _merge_body(a_ref, z_ref, ga_ref, gf_ref, wa_ref, wf_ref, o_ref):
    a = jnp.dot(a_ref[...], wa_ref[...], preferred_element_type=F32)
    f = jnp.dot(z_ref[...], wf_ref[...], preferred_element_type=F32)
    sga = jax.nn.sigmoid(ga_ref[...].astype(F32))
    sgf = jax.nn.sigmoid(gf_ref[...].astype(F32))
    o_ref[...] = (sga * a + sgf * f).astype(o_ref.dtype)


def _merge(attn, z, g, wa, wf, *, tm):
    t, d = attn.shape
    f_dim = z.shape[1]
    return pl.pallas_call(
        _merge_body,
        grid=(t // tm,),
        in_specs=[
            pl.BlockSpec((tm, d), lambda i: (i, 0)),
            pl.BlockSpec((tm, f_dim), lambda i: (i, 0)),
            pl.BlockSpec((tm, d), lambda i: (i, 0)),
            pl.BlockSpec((tm, d), lambda i: (i, 1)),
            _resident((d, d)),
            _resident((f_dim, d)),
        ],
        out_specs=pl.BlockSpec((tm, d), lambda i: (i, 0)),
        out_shape=jax.ShapeDtypeStruct((t, d), BF16),
        compiler_params=_params(1),
        name="merge",
    )(attn, z, g, g, wa, wf)


def _route_body(m_ref, xa_ref, xb_ref, wo_ref, gain_ref, wr_ref, tri_ref,
                hmid_ref, hn_ref, eid_ref, cw_ref, rank_ref, cnt_ref, *, n_a_tiles, n_groups):
    i = pl.program_id(0)
    tm, d = xa_ref.shape
    dh = d // 2
    n_exp = cnt_ref.shape[0]

    @pl.when(i == 0)
    def _init():
        cnt_ref[...] = jnp.zeros_like(cnt_ref)

    x = jnp.where(i < n_a_tiles, xa_ref[...], xb_ref[...])
    h = x + jnp.dot(m_ref[...], wo_ref[...], preferred_element_type=F32)
    hmid_ref[...] = h
    hn = h * lax.rsqrt(jnp.mean(h * h, axis=-1, keepdims=True) + NORM_EPS) * gain_ref[...]
    hb = hn.astype(BF16)
    hn_ref[...] = _pack_bf16_pair(hn[:, :dh], hn[:, dh:])

    lt = lax.dot_general(wr_ref[...], hb, (((1,), (1,)), ((), ())), preferred_element_type=F32)
    row8 = lax.broadcasted_iota(I32, (EXPERTS_PER_GROUP, tm), 0)
    neg = jnp.float32(-jnp.inf)
    lg = jnp.where(row8 < n_groups, lt[0:8], neg)
    gmax = jnp.max(lg, axis=0, keepdims=True)
    gidx = jnp.min(jnp.where(lg == gmax, row8, 8), axis=0, keepdims=True)
    p_g = 1.0 / jnp.sum(jnp.exp(lg - gmax), axis=0, keepdims=True)
    sel = lt[8:8 + EXPERTS_PER_GROUP]
    for g in range(1, n_groups):
        sel = jnp.where(gidx == g, lt[8 + g * EXPERTS_PER_GROUP:8 + (g + 1) * EXPERTS_PER_GROUP], sel)
    m1 = jnp.max(sel, axis=0, keepdims=True)
    i1 = jnp.min(jnp.where(sel == m1, row8, 8), axis=0, keepdims=True)
    sel2 = jnp.where(row8 == i1, neg, sel)
    m2 = jnp.max(sel2, axis=0, keepdims=True)
    i2 = jnp.min(jnp.where(sel2 == m2, row8, 8), axis=0, keepdims=True)
    e21 = jnp.exp(m2 - m1)
    p1 = 1.0 / (1.0 + e21)
    p2 = e21 * p1
    e1 = gidx * EXPERTS_PER_GROUP + i1
    e2 = gidx * EXPERTS_PER_GROUP + i2
    eid_ref[0, 0:1, :] = e1
    eid_ref[0, 1:2, :] = e2
    cw_ref[0, 0:1, :] = p_g * p1
    cw_ref[0, 1:2, :] = p_g * p2

    row_e = lax.broadcasted_iota(I32, (n_exp, tm), 0)
    oh1 = (row_e == e1)
    oh2 = (row_e == e2)
    cs1 = jnp.dot(oh1.astype(BF16), tri_ref[...], preferred_element_type=F32)
    cs2 = jnp.dot(oh2.astype(BF16), tri_ref[...], preferred_element_type=F32)
    oh1f, oh2f = oh1.astype(F32), oh2.astype(F32)
    c1 = jnp.sum(oh1f, axis=1, keepdims=True)
    c2 = jnp.sum(oh2f, axis=1, keepdims=True)
    base = cnt_ref[:, 0:1]
    rank_ref[0, 0:1, :] = jnp.sum(oh1f * (base + cs1), axis=0, keepdims=True).astype(I32)
    rank_ref[0, 1:2, :] = jnp.sum(oh2f * (base + c1 + cs2), axis=0, keepdims=True).astype(I32)
    cnt_ref[...] = cnt_ref[...] + (c1 + c2)


def _outproj_route(merged, xa, xb, wout, gain, wr_t, tri, *, tm, n_groups, n_exp):
    t, d = merged.shape
    nt = t // tm
    na = xa.shape[0] // tm
    tok3 = lambda i: (i, 0, 0)
    body = functools.partial(_route_body, n_a_tiles=na, n_groups=n_groups)
    return pl.pallas_call(
        body,
        grid=(nt,),
        in_specs=[
            pl.BlockSpec((tm, d), lambda i: (i, 0)),
            pl.BlockSpec((tm, d), lambda i: (jnp.minimum(i, na - 1), 0)),
            pl.BlockSpec((tm, d), lambda i: (jnp.maximum(i - na, 0), 0)),
            _resident((d, d)),
            pl.BlockSpec((1, d), lambda i: (0, 0)),
            _resident(wr_t.shape),
            _resident((tm, tm)),
        ],
        out_specs=[
            pl.BlockSpec((tm, d), lambda i: (i, 0)),
            pl.BlockSpec((tm, d // 2), lambda i: (i, 0)),
            pl.BlockSpec((1, TOP_K, tm), tok3),
            pl.BlockSpec((1, TOP_K, tm), tok3),
            pl.BlockSpec((1, TOP_K, tm), tok3),
            pl.BlockSpec((n_exp, HEAD_DIM), lambda i: (0, 0)),
        ],
        out_shape=[
            jax.ShapeDtypeStruct((t, d), F32),
            jax.ShapeDtypeStruct((t, d // 2), U32),
            jax.ShapeDtypeStruct((nt, TOP_K, tm), I32),
            jax.ShapeDtypeStruct((nt, TOP_K, tm), F32),
            jax.ShapeDtypeStruct((nt, TOP_K, tm), I32),
            jax.ShapeDtypeStruct((n_exp, HEAD_DIM), F32),
        ],
        compiler_params=_params(1),
        name="outproj_route",
    )(merged, xa, xb, wout, gain, wr_t, tri)


def _issue_row_copies(idx_ref, n_rows, copy_for):
    per = ROW_DMA_UNROLL

    def issue(a, carry):
        base = a * per
        for j in range(per):
            for k in range(TOP_K):
                copy_for(k, a * (per // SUBLANES) + j // SUBLANES, j % SUBLANES,
                         idx_ref[k * n_rows + base + j]).start()
        return carry

    lax.fori_loop(0, n_rows // per, issue, 0)


def _dispatch_body(pad_end_ref, dest_ref, x_ref, xs_ref, zbuf_ref, zsem, sem, *, bm, n_exp):
    i = pl.program_id(0)
    tm = x_ref.shape[0] * SUBLANES

    @pl.when(i == 0)
    def _zero_padding():
        zbuf_ref[...] = jnp.zeros_like(zbuf_ref)
        for e in range(n_exp):
            end = pad_end_ref[e]
            start = pad_end_ref[e - 1] if e > 0 else 0

            @pl.when(end > start)
            def _():
                cp = pltpu.make_async_copy(zbuf_ref, xs_ref.at[pl.ds(pl.multiple_of(end - bm, bm), bm)], zsem)
                cp.start()
                cp.wait()

    def copy_for(k, g, s, d):
        return pltpu.make_async_copy(x_ref.at[g, pl.ds(s, 1), :], xs_ref.at[pl.ds(d, 1), :], sem)

    _issue_row_copies(dest_ref, tm, copy_for)
    for k in range(TOP_K):
        pltpu.make_async_copy(x_ref, x_ref, sem).wait()


def _dispatch(hn2p, dest, pad_end, *, tm, bm, n_rows):
    t, dh = hn2p.shape
    n_exp = pad_end.shape[0]
    body = functools.partial(_dispatch_body, bm=bm, n_exp=n_exp)
    return pl.pallas_call(
        body,
        grid_spec=pltpu.PrefetchScalarGridSpec(
            num_scalar_prefetch=1,
            grid=(t // tm,),
            in_specs=[
                pl.BlockSpec((TOP_K * tm,), lambda i, pe: (i,), memory_space=pltpu.SMEM),
                pl.BlockSpec((tm // SUBLANES, SUBLANES, dh), lambda i, pe: (i, 0, 0)),
            ],
            out_specs=pl.BlockSpec(memory_space=pl.ANY),
            scratch_shapes=[pltpu.VMEM((bm, dh), U32), pltpu.SemaphoreType.DMA(()), pltpu.SemaphoreType.DMA(())],
        ),
        out_shape=jax.ShapeDtypeStruct((n_rows, dh), U32),
        compiler_params=_params(1),
        name="dispatch",
    )(pad_end, dest.reshape(-1), hn2p.reshape(t // SUBLANES, SUBLANES, dh))


def _ffn_body(bexp_ref, nvalid_ref, xs_ref, wg_ref, wu_ref, wd_ref, ys_ref):
    b = pl.program_id(0)

    @pl.when(b < nvalid_ref[0])
    def _():
        dh = xs_ref.shape[1]
        lo, hi = _unpack_bf16_pair(xs_ref[...])
        lo, hi = lo.astype(BF16), hi.astype(BF16)
        g = jnp.dot(lo, wg_ref[0, :dh, :], preferred_element_type=F32)
        g = g + jnp.dot(hi, wg_ref[0, dh:, :], preferred_element_type=F32)
        u = jnp.dot(lo, wu_ref[0, :dh, :], preferred_element_type=F32)
        u = u + jnp.dot(hi, wu_ref[0, dh:, :], preferred_element_type=F32)
        hid = (jax.nn.silu(g) * u).astype(BF16)
        y = jnp.dot(hid, wd_ref[0], preferred_element_type=F32)
        ys_ref[...] = _pack_bf16_pair(y[:, :dh], y[:, dh:])


def _expert_ffn(xs, wg, wu, wd, block_expert, n_valid, *, bm):
    n_rows, dh = xs.shape
    n_exp, d, de = wg.shape
    nb = n_rows // bm
    blk = lambda b, be, nv: (jnp.minimum(b, nv[0] - 1), 0)
    wmap = lambda b, be, nv: (be[jnp.minimum(b, nv[0] - 1)], 0, 0)
    return pl.pallas_call(
        _ffn_body,
        grid_spec=pltpu.PrefetchScalarGridSpec(
            num_scalar_prefetch=2,
            grid=(nb,),
            in_specs=[
                pl.BlockSpec((bm, dh), blk),
                pl.BlockSpec((1, d, de), wmap),
                pl.BlockSpec((1, d, de), wmap),
                pl.BlockSpec((1, de, d), wmap),
            ],
            out_specs=pl.BlockSpec((bm, dh), blk),
        ),
        out_shape=jax.ShapeDtypeStruct((n_rows, dh), U32),
        compiler_params=_params(1),
        name="expert_ffn",
    )(block_expert, n_valid, xs, wg, wu, wd)


def _combine_body(dest_ref, h_ref, cw_ref, gain_ref, ys_ref, o_ref, buf_ref, sem):
    tm, d = h_ref.shape
    dh = d // 2

    def copy_for(k, g, s, row):
        return pltpu.make_async_copy(ys_ref.at[pl.ds(row, 1), :], buf_ref.at[k, g, pl.ds(s, 1), :], sem)

    _issue_row_copies(dest_ref, tm, copy_for)
    for k in range(TOP_K):
        pltpu.make_async_copy(buf_ref.at[k], buf_ref.at[k], sem).wait()

    lo1, hi1 = _unpack_bf16_pair(buf_ref[0].reshape(tm, dh))
    lo2, hi2 = _unpack_bf16_pair(buf_ref[1].reshape(tm, dh))
    w1 = cw_ref[:, 0:1]
    w2 = cw_ref[:, 1:2]
    o_lo = h_ref[:, :dh] + (w1 * lo1 + w2 * lo2)
    o_hi = h_ref[:, dh:] + (w1 * hi1 + w2 * hi2)
    ms = (jnp.sum(o_lo * o_lo, axis=-1, keepdims=True) + jnp.sum(o_hi * o_hi, axis=-1, keepdims=True)) / d
    r = lax.rsqrt(ms + NORM_EPS)
    o_ref[:, :dh] = o_lo * r * gain_ref[:, :dh]
    o_ref[:, dh:] = o_hi * r * gain_ref[:, dh:]


def _combine(ys, dest, hmid, cw_t, gain, *, tm, row_off, n_rows):
    d = hmid.shape[1]
    dh = d // 2
    off = row_off // tm
    return pl.pallas_call(
        _combine_body,
        grid=(n_rows // tm,),
        in_specs=[
            pl.BlockSpec((TOP_K * tm,), lambda i: (off + i,), memory_space=pltpu.SMEM),
            pl.BlockSpec((tm, d), lambda i: (off + i, 0)),
            pl.BlockSpec((tm, TOP_K), lambda i: (off + i, 0)),
            pl.BlockSpec((1, d), lambda i: (0, 0)),
            pl.BlockSpec(memory_space=pl.ANY),
        ],
        out_specs=pl.BlockSpec((tm, d), lambda i: (i, 0)),
        out_shape=jax.ShapeDtypeStruct((n_rows, d), F32),
        scratch_shapes=[pltpu.VMEM((TOP_K, tm // SUBLANES, SUBLANES, dh), U32), pltpu.SemaphoreType.DMA(())],
        compiler_params=_params(1),
        name="combine",
    )(dest.reshape(-1), hmid, cw_t, gain, ys)


def _rope_tables(n_tokens):
    rows = n_tokens // GRID_W
    row = jnp.repeat(jnp.arange(rows, dtype=F32), GRID_W)
    col = jnp.tile(jnp.arange(GRID_W, dtype=F32), rows)
    inv_freq = ROPE_THETA ** (-jnp.arange(ROPE_HALF, dtype=F32) / ROPE_HALF)
    ar, ac = row[:, None] * inv_freq, col[:, None] * inv_freq
    cos = jnp.concatenate([jnp.cos(ar), jnp.cos(ar), jnp.cos(ac), jnp.cos(ac)], axis=1)
    sin = jnp.concatenate([-jnp.sin(ar), jnp.sin(ar), -jnp.sin(ac), jnp.sin(ac)], axis=1)
    return cos, sin


def _tile(n, pref):
    return min(n, pref)


def kernel(x_prompt, x_sample, meta_tokens, mix_norm, w_in, q_gain, k_gain, w_attn_o, w_fourier_o, w_out,
           moe_norm, w_router_group, w_router_expert, w_expert_gate, w_expert_up, w_expert_down, final_norm):
    bp, n_p, d = x_prompt.shape
    bs, n_s, _ = x_sample.shape
    n_meta = meta_tokens.shape[0]
    t_p, t_s = bp * n_p, bs * n_s
    t = t_p + t_s
    f_dim = d // 2
    cg = f_dim // N_FOURIER_GROUPS
    n_groups = w_router_group.shape[-1]
    n_exp = w_router_expert.shape[-1]
    assert n_exp == n_groups * EXPERTS_PER_GROUP and n_groups <= 8

    tm = _tile(math.gcd(n_p, n_s), 512)
    bm = _tile(t, 512)

    w_in_b = w_in[0].astype(BF16)
    wa_b, wf_b, wo_b = w_attn_o[0].astype(BF16), w_fourier_o[0].astype(BF16), w_out[0].astype(BF16)
    wg_b, wu_b, wd_b = (w_expert_gate[0].astype(BF16), w_expert_up[0].astype(BF16),
                        w_expert_down[0].astype(BF16))
    qg = (q_gain[0] * (HEAD_DIM ** -0.5 * LOG2E)).reshape(1, HEAD_DIM)
    kg = k_gain[0].reshape(1, HEAD_DIM)
    wr_t = jnp.zeros((8 + n_exp, d), F32)
    wr_t = wr_t.at[:n_groups].set(w_router_group[0].T).at[8:].set(w_router_expert[0].T).astype(BF16)
    tri = (jnp.arange(tm)[:, None] < jnp.arange(tm)[None, :]).astype(BF16)
    cos, sin = _rope_tables(max(n_p, n_s))
    cdc, cds = _chan_dft(cg)
    gain_in = mix_norm[0].reshape(1, d)

    xp, xs_in = x_prompt.reshape(t_p, d), x_sample.reshape(t_s, d)
    tm_in = _tile(math.gcd(t_p, t_s), INPROJ_ROWS)
    proj = _inproj(xp, gain_in, w_in_b, None, tm=tm_in, row_off=0, t_total=t)
    uf, q, k, v, g = _inproj(xs_in, gain_in, w_in_b, proj, tm=tm_in, row_off=t_p, t_total=t)
    ufm, _, km, vm, _ = _inproj(meta_tokens, gain_in, w_in_b, None, tm=n_meta, row_off=0, t_total=n_meta)

    attn, z = None, None
    for row_off, n_seq, n_tok in ((0, bp, n_p), (t_p, bs, n_s)):
        attn = _attention(q, k, v, km, vm, qg, kg, cos, sin, attn, row_off=row_off, n_seq=n_seq,
                          n_tok=n_tok, tq=_tile(n_tok, ATTN_Q_TILE))
        z = _fourier(uf, ufm, _seq_dft(n_tok, n_meta), cdc, cds, z, row_off=row_off, n_seq=n_seq,
                     n_tok=n_tok, tl=_tile(n_tok, 512), tf=_tile(f_dim, 512))

    merged = _merge(attn, z, g, wa_b, wf_b, tm=tm)
    hmid, hn2p, eid, cw, rank, counts = _outproj_route(
        merged, xp, xs_in, wo_b, moe_norm[0].reshape(1, d), wr_t, tri, tm=tm, n_groups=n_groups, n_exp=n_exp)

    cnt = counts[:, 0].astype(I32)
    padded = (cnt + bm - 1) // bm * bm
    pad_end = jnp.cumsum(padded).astype(I32)
    pad_start = pad_end - padded
    onehot = eid[..., None] == jnp.arange(n_exp, dtype=I32)
    dest = jnp.sum(jnp.where(onehot, pad_start, 0), axis=-1) + rank
    nb = (t * TOP_K) // bm + n_exp
    n_valid = (pad_end[-1:] // bm).astype(I32)
    blk_start = jnp.arange(nb, dtype=I32) * bm
    block_expert = jnp.minimum(jnp.sum(pad_end[None, :] <= blk_start[:, None], axis=1), n_exp - 1).astype(I32)

    xs = _dispatch(hn2p, dest, pad_end, tm=tm, bm=bm, n_rows=nb * bm)
    ys = _expert_ffn(xs, wg_b, wu_b, wd_b, block_expert, n_valid, bm=bm)
    cw_t = jnp.transpose(cw, (0, 2, 1)).reshape(t, TOP_K)
    fgain = final_norm.reshape(1, d)
    y_p = _combine(ys, dest, hmid, cw_t, fgain, tm=tm, row_off=0, n_rows=t_p)
    y_s = _combine(ys, dest, hmid, cw_t, fgain, tm=tm, row_off=t_p, n_rows=t_s)
    return y_p.reshape(bp, n_p, d), y_s.reshape(bs, n_s, d)
```

```python
import functools
import math

import jax
import jax.numpy as jnp
from jax import lax
from jax.experimental import pallas as pl
from jax.experimental.pallas import tpu as pltpu

F32, BF16, I32, U32 = jnp.float32, jnp.bfloat16, jnp.int32, jnp.uint32

GRID_W = 64
HEAD_DIM = 128
ROPE_HALF = 32
Q_PER_KV = 4
N_FOURIER_GROUPS = 4
ROPE_THETA = 10000.0
NORM_EPS = 1e-6
TOP_K = 2
EXPERTS_PER_GROUP = 8
LOG2E = 1.4426950408889634
ATTN_KV_CHUNK = 512
ATTN_UNIT_ROWS = 256
ATTN_Q_TILE = 512
INPROJ_ROWS = 1024
SUBLANES = 8
ROW_DMA_UNROLL = 32

VMEM_LIMIT_BYTES = 56 * 1024 * 1024


def _params(n_axes):
    return pltpu.CompilerParams(dimension_semantics=("arbitrary",) * n_axes,
                                vmem_limit_bytes=VMEM_LIMIT_BYTES)


def _resident(shape):
    return pl.BlockSpec(shape, lambda *_: (0,) * len(shape), pipeline_mode=pl.Buffered(1))


def _pack_bf16_pair(lo, hi):
    lo_bits = lax.bitcast_convert_type(lo.astype(BF16).astype(F32), U32) >> 16
    hi_bits = lax.bitcast_convert_type(hi.astype(BF16).astype(F32), U32) & jnp.uint32(0xFFFF0000)
    return lo_bits | hi_bits


def _unpack_bf16_pair(w):
    lo = lax.bitcast_convert_type(w << 16, F32)
    hi = lax.bitcast_convert_type(w & jnp.uint32(0xFFFF0000), F32)
    return lo, hi


_F_TILES, _Q_TILES, _G_TILES = 2, 4, 8
_N_TILES = _F_TILES + _Q_TILES + 2 + _G_TILES


def _inproj_body(x_ref, gain_ref, w_ref, *rest):
    uf_ref, q_ref, k_ref, v_ref, g_ref, hn_ref = rest[-6:]
    n = pl.program_id(1)

    @pl.when(n == 0)
    def _norm():
        x = x_ref[...]
        y = x * lax.rsqrt(jnp.mean(x * x, axis=-1, keepdims=True) + NORM_EPS)
        hn_ref[...] = (y * gain_ref[...]).astype(BF16)

    q0 = _F_TILES
    k0 = q0 + _Q_TILES

    def project_into(out_ref):
        def branch():
            acc = jnp.dot(hn_ref[...], w_ref[...], preferred_element_type=F32)
            out_ref[...] = acc.astype(out_ref.dtype)
        return branch

    pl.when(n < q0)(project_into(uf_ref))
    pl.when((n >= q0) & (n < k0))(project_into(q_ref))
    pl.when(n == k0)(project_into(k_ref))
    pl.when(n == k0 + 1)(project_into(v_ref))
    pl.when(n > k0 + 1)(project_into(g_ref))


def _inproj(x, gain, w_in, prev, *, tm, row_off, t_total):
    tx, d = x.shape
    tn = d // 4
    f_dim, q_dim, kv_dim = d // 2, d, d // 4
    assert w_in.shape == (d, _N_TILES * tn) and tx % tm == 0 and row_off % tm == 0
    ob = row_off // tm
    q0, k0, g0 = _F_TILES, _F_TILES + _Q_TILES, _F_TILES + _Q_TILES + 2
    in_specs = [
        pl.BlockSpec((tm, d), lambda i, n: (i, 0)),
        pl.BlockSpec((1, d), lambda i, n: (0, 0)),
        pl.BlockSpec((d, tn), lambda i, n: (0, n)),
    ]
    args = [x, gain, w_in]
    aliases = {}
    if prev is not None:
        in_specs += [pl.BlockSpec(memory_space=pl.ANY)] * len(prev)
        aliases = {len(args) + j: j for j in range(len(prev))}
        args += list(prev)
    return pl.pallas_call(
        _inproj_body,
        grid=(tx // tm, _N_TILES),
        in_specs=in_specs,
        out_specs=[
            pl.BlockSpec((tm, tn), lambda i, n: (ob + i, jnp.clip(n, 0, _F_TILES - 1))),
            pl.BlockSpec((tm, tn), lambda i, n: (ob + i, jnp.clip(n - q0, 0, _Q_TILES - 1))),
            pl.BlockSpec((tm, tn), lambda i, n: (ob + i, 0)),
            pl.BlockSpec((tm, tn), lambda i, n: (ob + i, 0)),
            pl.BlockSpec((tm, tn), lambda i, n: (ob + i, jnp.clip(n - g0, 0, _G_TILES - 1))),
        ],
        out_shape=[
            jax.ShapeDtypeStruct((t_total, f_dim), BF16),
            jax.ShapeDtypeStruct((t_total, q_dim), F32),
            jax.ShapeDtypeStruct((t_total, kv_dim), F32),
            jax.ShapeDtypeStruct((t_total, kv_dim), BF16),
            jax.ShapeDtypeStruct((t_total, 2 * d), BF16),
        ],
        scratch_shapes=[pltpu.VMEM((tm, d), BF16)],
        input_output_aliases=aliases,
        compiler_params=_params(2),
        name="inproj",
    )(*args)


def _head_norm(a, gain):
    return a * lax.rsqrt(jnp.mean(a * a, axis=-1, keepdims=True) + NORM_EPS) * gain


def _rope(y, cos, sin):
    lo_half = (lax.broadcasted_iota(I32, y.shape, 1) & ROPE_HALF) == 0
    partner = jnp.where(lo_half, pltpu.roll(y, HEAD_DIM - ROPE_HALF, 1), pltpu.roll(y, ROPE_HALF, 1))
    return y * cos + partner * sin


def _attn_body(q_ref, k_ref, v_ref, km_ref, vm_ref, qg_ref, kg_ref, cq_ref, sq_ref, ck_ref, sk_ref, *rest,
               n_cast):
    cast_in = rest[:n_cast]
    kb_ref, kmb_ref, ve_ref, vme_ref, s_ref = rest[-5:]
    o_ref = rest[-6 - n_cast]
    cast_out = rest[len(rest) - 5 - n_cast:len(rest) - 5]
    tq, n_tok = q_ref.shape[0], k_ref.shape[0]

    for w_ref, wb_ref in zip(cast_in, cast_out):
        wb_ref[...] = w_ref[...].astype(BF16)

    tk = min(n_tok, ATTN_KV_CHUNK)
    n_chunks, lanes = n_tok // tk, tk // HEAD_DIM
    nt = (((1,), (1,)), ((), ()))

    @pl.when(pl.program_id(2) == 0)
    def _prepare_keys():
        kb_ref[...] = _rope(_head_norm(k_ref[...], kg_ref[...]), ck_ref[...], sk_ref[...]).astype(BF16)
        kmb_ref[...] = _head_norm(km_ref[...], kg_ref[...]).astype(BF16)
        ve_ref[:, :HEAD_DIM] = v_ref[...]
        ve_ref[:, HEAD_DIM:] = jnp.ones((n_tok, HEAD_DIM), BF16)
        vme_ref[:, :HEAD_DIM] = vm_ref[...]
        vme_ref[:, HEAD_DIM:] = jnp.ones((vm_ref.shape[0], HEAD_DIM), BF16)

    qg = qg_ref[...]
    ru = min(tq, ATTN_UNIT_ROWS)
    units = [(r0, h) for r0 in range(0, tq, ru) for h in range(Q_PER_KV)]

    def scores(u):
        r0, h = units[u]
        rows = slice(r0, r0 + ru)
        qh = _rope(_head_norm(q_ref[rows, h * HEAD_DIM:(h + 1) * HEAD_DIM], qg), cq_ref[rows, :], sq_ref[rows, :])
        qh = qh.astype(BF16)
        mp = None
        for c in range(n_chunks):
            s = lax.dot_general(qh, kb_ref[c * tk:(c + 1) * tk, :], nt, preferred_element_type=F32)
            s_ref[u % 2, :, c * tk:(c + 1) * tk] = s
            for j in range(lanes):
                t = s[:, j * HEAD_DIM:(j + 1) * HEAD_DIM]
                mp = t if mp is None else jnp.maximum(mp, t)
            yield None
        sm = lax.dot_general(qh, kmb_ref[...], nt, preferred_element_type=F32)
        m = jnp.maximum(jnp.max(mp, axis=-1, keepdims=True), jnp.max(sm, axis=-1, keepdims=True))
        yield m, sm

    def outputs(u, m, sm):
        r0, h = units[u]
        mb = jnp.broadcast_to(m, (ru, HEAD_DIM))
        pm = jnp.exp2(sm - m)
        acc = jnp.dot(pm.astype(BF16), vme_ref[...], preferred_element_type=F32)
        for c in range(n_chunks):
            tiles = []
            for j in range(lanes):
                col = c * tk + j * HEAD_DIM
                tiles.append(jnp.exp2(s_ref[u % 2, :, col:col + HEAD_DIM] - mb).astype(BF16))
            acc = acc + jnp.dot(jnp.concatenate(tiles, axis=1), ve_ref[c * tk:(c + 1) * tk, :],
                                preferred_element_type=F32)
            yield None
        o = acc[:, :HEAD_DIM] / acc[:, HEAD_DIM:]
        o_ref[r0:r0 + ru, h * HEAD_DIM:(h + 1) * HEAD_DIM] = o.astype(o_ref.dtype)
        yield None

    prev_out = None
    for u in range(len(units) + 1):
        cur = scores(u) if u < len(units) else None
        stats = None
        for _ in range(n_chunks + 1):
            if cur is not None:
                stats = next(cur)
            if prev_out is not None:
                next(prev_out)
        prev_out = outputs(u, *stats) if cur is not None else None


def _attention(q, k, v, km, vm, qg, kg, cos, sin, prev, *, row_off, n_seq, n_tok, tq,
               cast_src=(), cast_prev=None, cast_rows=(), cast_row_off=()):
    t, q_dim = q.shape
    n_kv = k.shape[1] // HEAD_DIM
    n_meta = km.shape[0]
    gw = Q_PER_KV * HEAD_DIM
    assert row_off % n_tok == 0 and n_tok % tq == 0
    qmap = lambda b, kh, qi: ((row_off + b * n_tok) // tq + qi, kh)
    kmap = lambda b, kh, qi: (row_off // n_tok + b, kh)
    mmap = lambda b, kh, qi: (0, kh)
    const = lambda b, kh, qi: (0, 0)
    qpos = lambda b, kh, qi: (qi, 0)
    in_specs = [
        pl.BlockSpec((tq, gw), qmap),
        pl.BlockSpec((n_tok, HEAD_DIM), kmap),
        pl.BlockSpec((n_tok, HEAD_DIM), kmap),
        pl.BlockSpec((n_meta, HEAD_DIM), mmap),
        pl.BlockSpec((n_meta, HEAD_DIM), mmap),
        pl.BlockSpec((1, HEAD_DIM), const),
        pl.BlockSpec((1, HEAD_DIM), const),
        pl.BlockSpec((tq, HEAD_DIM), qpos),
        pl.BlockSpec((tq, HEAD_DIM), qpos),
        pl.BlockSpec((n_tok, HEAD_DIM), const),
        pl.BlockSpec((n_tok, HEAD_DIM), const),
    ]
    args = [q, k, v, km, vm, qg, kg, cos, sin, cos, sin]
    nq = n_tok // tq
    out_specs = [pl.BlockSpec((tq, gw), qmap)]
    out_shape = [jax.ShapeDtypeStruct((t, q_dim), BF16)]
    for w, rows, off in zip(cast_src, cast_rows, cast_row_off):
        assert off % rows == 0 and rows % SUBLANES == 0
        wmap = lambda b, kh, qi, rows=rows, off=off: (off // rows + (b * n_kv + kh) * nq + qi, 0)
        in_specs.append(pl.BlockSpec((rows, w.shape[1]), wmap))
        args.append(w)
        out_specs.append(pl.BlockSpec((rows, w.shape[1]), wmap))
        out_shape.append(jax.ShapeDtypeStruct(w.shape, BF16))
    aliases = {}
    if prev is not None:
        in_specs.append(pl.BlockSpec(memory_space=pl.ANY))
        args.append(prev)
        aliases[len(args) - 1] = 0
    for j, wprev in enumerate(cast_prev or ()):
        in_specs.append(pl.BlockSpec(memory_space=pl.ANY))
        args.append(wprev)
        aliases[len(args) - 1] = 1 + j
    return pl.pallas_call(
        functools.partial(_attn_body, n_cast=len(cast_src)),
        grid=(n_seq, n_kv, nq),
        in_specs=in_specs,
        out_specs=out_specs,
        out_shape=out_shape,
        scratch_shapes=[
            pltpu.VMEM((n_tok, HEAD_DIM), BF16),
            pltpu.VMEM((n_meta, HEAD_DIM), BF16),
            pltpu.VMEM((n_tok, 2 * HEAD_DIM), BF16),
            pltpu.VMEM((n_meta, 2 * HEAD_DIM), BF16),
            pltpu.VMEM((2, min(tq, ATTN_UNIT_ROWS), n_tok), F32),
        ],
        input_output_aliases=aliases,
        compiler_params=_params(3),
        name="attention",
    )(*args)


def _dftgen_body(tac_ref, tas_ref, tbc_ref, tbs_ref, c_ref, ns_ref):
    ac, asn = tac_ref[0], tas_ref[0]
    bc, bsn = tbc_ref[...], tbs_ref[...]
    c_ref[...] = (ac * bc - asn * bsn).astype(BF16)
    ns_ref[...] = (-(asn * bc + ac * bsn)).astype(BF16)


def _seq_dft(n_tok, n_meta):
    length = n_tok + n_meta
    rb = min(n_tok, HEAD_DIM)
    w = 2.0 * math.pi / length
    scale = 1.0 / math.sqrt(length)
    p_real = n_meta + jnp.arange(n_tok, dtype=I32)

    def angles(p_rows, p_cols):
        return ((p_rows[:, None] * p_cols[None, :]) % length).astype(F32) * w

    ang_a = angles(n_meta + rb * jnp.arange(n_tok // rb, dtype=I32), p_real)
    ang_b = angles(jnp.arange(rb, dtype=I32), p_real)
    ang_m = angles(p_real, jnp.arange(n_meta, dtype=I32))
    tac = (jnp.cos(ang_a) * scale).reshape(n_tok // rb, 1, n_tok)
    tas = (jnp.sin(ang_a) * scale).reshape(n_tok // rb, 1, n_tok)
    blk = pl.BlockSpec((1, 1, n_tok), lambda a: (a, 0, 0))
    full = pl.BlockSpec((rb, n_tok), lambda a: (0, 0))
    cmat, nsmat = pl.pallas_call(
        _dftgen_body,
        grid=(n_tok // rb,),
        in_specs=[blk, blk, full, full],
        out_specs=[pl.BlockSpec((rb, n_tok), lambda a: (a, 0))] * 2,
        out_shape=[jax.ShapeDtypeStruct((n_tok, n_tok), BF16)] * 2,
        compiler_params=_params(1),
        name="dft_matrices",
    )(tac, tas, jnp.cos(ang_b), jnp.sin(ang_b))
    return cmat, nsmat, (jnp.cos(ang_m) * scale).astype(BF16), (-jnp.sin(ang_m) * scale).astype(BF16)


def _chan_dft(cg):
    idx = jnp.arange(cg, dtype=I32)
    ang = ((idx[:, None] * idx[None, :]) % cg).astype(F32) * (2.0 * math.pi / cg)
    scale = 1.0 / math.sqrt(cg)
    return (jnp.cos(ang) * scale).astype(BF16), (jnp.sin(ang) * scale).astype(BF16)


def _fourier_body(c_ref, ns_ref, cm_ref, nsm_ref, u_ref, um_ref, cdc_ref, cds_ref, *rest):
    z_ref = rest[-1]
    u, um = u_ref[...], um_ref[...]
    a = jnp.dot(c_ref[...], u, preferred_element_type=F32) + jnp.dot(cm_ref[...], um, preferred_element_type=F32)
    b = jnp.dot(ns_ref[...], u, preferred_element_type=F32) + jnp.dot(nsm_ref[...], um, preferred_element_type=F32)
    a, b = a.astype(BF16), b.astype(BF16)
    cg = cdc_ref.shape[0]
    for g in range(z_ref.shape[1] // cg):
        sl = slice(g * cg, (g + 1) * cg)
        z = jnp.dot(a[:, sl], cdc_ref[...], preferred_element_type=F32)
        z = z + jnp.dot(b[:, sl], cds_ref[...], preferred_element_type=F32)
        z_ref[:, sl] = z.astype(z_ref.dtype)


def _fourier(uf, ufm, mats, cdc, cds, prev, *, row_off, n_seq, n_tok, tl, tf):
    cmat, nsmat, cmeta, nsmeta = mats
    t, f_dim = uf.shape
    n_meta = ufm.shape[0]
    cg = cdc.shape[0]
    assert row_off % n_tok == 0 and n_tok % tl == 0 and f_dim % tf == 0 and tf % cg == 0
    cmap = lambda b, j, i: (i, 0)
    const = lambda b, j, i: (0, 0)
    in_specs = [
        pl.BlockSpec((tl, n_tok), cmap),
        pl.BlockSpec((tl, n_tok), cmap),
        pl.BlockSpec((tl, n_meta), cmap),
        pl.BlockSpec((tl, n_meta), cmap),
        pl.BlockSpec((n_tok, tf), lambda b, j, i: (row_off // n_tok + b, j)),
        pl.BlockSpec((n_meta, tf), lambda b, j, i: (0, j)),
        pl.BlockSpec((cg, cg), const),
        pl.BlockSpec((cg, cg), const),
    ]
    args = [cmat, nsmat, cmeta, nsmeta, uf, ufm, cdc, cds]
    aliases = {}
    if prev is not None:
        in_specs.append(pl.BlockSpec(memory_space=pl.ANY))
        args.append(prev)
        aliases = {len(args) - 1: 0}
    return pl.pallas_call(
        _fourier_body,
        grid=(n_seq, f_dim // tf, n_tok // tl),
        in_specs=in_specs,
        out_specs=pl.BlockSpec((tl, tf), lambda b, j, i: ((row_off + b * n_tok) // tl + i, j)),
        out_shape=jax.ShapeDtypeStruct((t, f_dim), BF16),
        input_output_aliases=aliases,
        compiler_params=_params(3),
        name="fourier",
    )(*args)


def _merge_body(a_ref, z_ref, ga_ref, gf_ref, wa_ref, wf_ref, o_ref):
    a = jnp.dot(a_ref[...], wa_ref[...], preferred_element_type=F32)
    f = jnp.dot(z_ref[...], wf_ref[...], preferred_element_type=F32)
    sga = jax.nn.sigmoid(ga_ref[...].astype(F32))
    sgf = jax.nn.sigmoid(gf_ref[...].astype(F32))
    o_ref[...] = (sga * a + sgf * f).astype(o_ref.dtype)


def _merge(attn, z, g, wa, wf, *, tm):
    t, d = attn.shape
    f_dim = z.shape[1]
    return pl.pallas_call(
        _merge_body,
        grid=(t // tm,),
        in_specs=[
            pl.BlockSpec((tm, d), lambda i: (i, 0)),
            pl.BlockSpec((tm, f_dim), lambda i: (i, 0)),
            pl.BlockSpec((tm, d), lambda i: (i, 0)),
            pl.BlockSpec((tm, d), lambda i: (i, 1)),
            _resident((d, d)),
            _resident((f_dim, d)),
        ],
        out_specs=pl.BlockSpec((tm, d), lambda i: (i, 0)),
        out_shape=jax.ShapeDtypeStruct((t, d), BF16),
        compiler_params=_params(1),
        name="merge",
    )(attn, z, g, g, wa, wf)


def _route_body(m_ref, xa_ref, xb_ref, wo_ref, gain_ref, wr_ref, tri_ref,
                hmid_ref, hn_ref, eid_ref, cw_ref, rank_ref, cnt_ref, *, n_a_tiles, n_groups):
    i = pl.program_id(0)
    tm, d = xa_ref.shape
    dh = d // 2
    n_exp = cnt_ref.shape[0]

    @pl.when(i == 0)
    def _init():
        cnt_ref[...] = jnp.zeros_like(cnt_ref)

    x = jnp.where(i < n_a_tiles, xa_ref[...], xb_ref[...])
    h = x + jnp.dot(m_ref[...], wo_ref[...], preferred_element_type=F32)
    hmid_ref[...] = h
    hn = h * lax.rsqrt(jnp.mean(h * h, axis=-1, keepdims=True) + NORM_EPS) * gain_ref[...]
    hb = hn.astype(BF16)
    hn_ref[...] = _pack_bf16_pair(hn[:, :dh], hn[:, dh:])

    lt = lax.dot_general(wr_ref[...], hb, (((1,), (1,)), ((), ())), preferred_element_type=F32)
    row8 = lax.broadcasted_iota(I32, (EXPERTS_PER_GROUP, tm), 0)
    neg = jnp.float32(-jnp.inf)
    lg = jnp.where(row8 < n_groups, lt[0:8], neg)
    gmax = jnp.max(lg, axis=0, keepdims=True)
    gidx = jnp.min(jnp.where(lg == gmax, row8, 8), axis=0, keepdims=True)
    p_g = 1.0 / jnp.sum(jnp.exp(lg - gmax), axis=0, keepdims=True)
    sel = lt[8:8 + EXPERTS_PER_GROUP]
    for g in range(1, n_groups):
        sel = jnp.where(gidx == g, lt[8 + g * EXPERTS_PER_GROUP:8 + (g + 1) * EXPERTS_PER_GROUP], sel)
    m1 = jnp.max(sel, axis=0, keepdims=True)
    i1 = jnp.min(jnp.where(sel == m1, row8, 8), axis=0, keepdims=True)
    sel2 = jnp.where(row8 == i1, neg, sel)
    m2 = jnp.max(sel2, axis=0, keepdims=True)
    i2 = jnp.min(jnp.where(sel2 == m2, row8, 8), axis=0, keepdims=True)
    e21 = jnp.exp(m2 - m1)
    p1 = 1.0 / (1.0 + e21)
    p2 = e21 * p1
    e1 = gidx * EXPERTS_PER_GROUP + i1
    e2 = gidx * EXPERTS_PER_GROUP + i2
    eid_ref[0, 0:1, :] = e1
    eid_ref[0, 1:2, :] = e2
    cw_ref[0, 0:1, :] = p_g * p1
    cw_ref[0, 1:2, :] = p_g * p2

    row_e = lax.broadcasted_iota(I32, (n_exp, tm), 0)
    oh1 = (row_e == e1)
    oh2 = (row_e == e2)
    cs1 = jnp.dot(oh1.astype(BF16), tri_ref[...], preferred_element_type=F32)
    cs2 = jnp.dot(oh2.astype(BF16), tri_ref[...], preferred_element_type=F32)
    oh1f, oh2f = oh1.astype(F32), oh2.astype(F32)
    c1 = jnp.sum(oh1f, axis=1, keepdims=True)
    c2 = jnp.sum(oh2f, axis=1, keepdims=True)
    base = cnt_ref[:, 0:1]
    rank_ref[0, 0:1, :] = jnp.sum(oh1f * (base + cs1), axis=0, keepdims=True).astype(I32)
    rank_ref[0, 1:2, :] = jnp.sum(oh2f * (base + c1 + cs2), axis=0, keepdims=True).astype(I32)
    cnt_ref[...] = cnt_ref[...] + (c1 + c2)


def _outproj_route(merged, xa, xb, wout, gain, wr_t, tri, *, tm, n_groups, n_exp):
    t, d = merged.shape
    nt = t // tm
    na = xa.shape[0] // tm
    tok3 = lambda i: (i, 0, 0)
    body = functools.partial(_route_body, n_a_tiles=na, n_groups=n_groups)
    return pl.pallas_call(
        body,
        grid=(nt,),
        in_specs=[
            pl.BlockSpec((tm, d), lambda i: (i, 0)),
            pl.BlockSpec((tm, d), lambda i: (jnp.minimum(i, na - 1), 0)),
            pl.BlockSpec((tm, d), lambda i: (jnp.maximum(i - na, 0), 0)),
            _resident((d, d)),
            pl.BlockSpec((1, d), lambda i: (0, 0)),
            _resident(wr_t.shape),
            _resident((tm, tm)),
        ],
        out_specs=[
            pl.BlockSpec((tm, d), lambda i: (i, 0)),
            pl.BlockSpec((tm, d // 2), lambda i: (i, 0)),
            pl.BlockSpec((1, TOP_K, tm), tok3),
            pl.BlockSpec((1, TOP_K, tm), tok3),
            pl.BlockSpec((1, TOP_K, tm), tok3),
            pl.BlockSpec((n_exp, HEAD_DIM), lambda i: (0, 0)),
        ],
        out_shape=[
            jax.ShapeDtypeStruct((t, d), F32),
            jax.ShapeDtypeStruct((t, d // 2), U32),
            jax.ShapeDtypeStruct((nt, TOP_K, tm), I32),
            jax.ShapeDtypeStruct((nt, TOP_K, tm), F32),
            jax.ShapeDtypeStruct((nt, TOP_K, tm), I32),
            jax.ShapeDtypeStruct((n_exp, HEAD_DIM), F32),
        ],
        compiler_params=_params(1),
        name="outproj_route",
    )(merged, xa, xb, wout, gain, wr_t, tri)


def _issue_row_copies(idx_ref, n_rows, copy_for, per_trip=None):
    per = ROW_DMA_UNROLL

    def issue(a, carry):
        if per_trip is not None:
            per_trip(a)
        base = a * per
        for j in range(per):
            for k in range(TOP_K):
                copy_for(k, a * (per // SUBLANES) + j // SUBLANES, j % SUBLANES,
                         idx_ref[k * n_rows + base + j]).start()
        return carry

    lax.fori_loop(0, n_rows // per, issue, 0)


def _dispatch_body(pad_end_ref, dest_ref, x_ref, xs_ref, zbuf_ref, zsem, sem, *, bm, n_exp):
    i = pl.program_id(0)
    tm = x_ref.shape[0] * SUBLANES

    @pl.when(i == 0)
    def _zero_padding():
        zbuf_ref[...] = jnp.zeros_like(zbuf_ref)
        for e in range(n_exp):
            end = pad_end_ref[e]
            start = pad_end_ref[e - 1] if e > 0 else 0

            @pl.when(end > start)
            def _():
                cp = pltpu.make_async_copy(zbuf_ref, xs_ref.at[pl.ds(pl.multiple_of(end - bm, bm), bm)], zsem)
                cp.start()
                cp.wait()

    def copy_for(k, g, s, d):
        return pltpu.make_async_copy(x_ref.at[g, pl.ds(s, 1), :], xs_ref.at[pl.ds(d, 1), :], sem)

    _issue_row_copies(dest_ref, tm, copy_for)
    for k in range(TOP_K):
        pltpu.make_async_copy(x_ref, x_ref, sem).wait()


def _dispatch(hn2p, dest, pad_end, *, tm, bm, n_rows):
    t, dh = hn2p.shape
    n_exp = pad_end.shape[0]
    body = functools.partial(_dispatch_body, bm=bm, n_exp=n_exp)
    return pl.pallas_call(
        body,
        grid_spec=pltpu.PrefetchScalarGridSpec(
            num_scalar_prefetch=1,
            grid=(t // tm,),
            in_specs=[
                pl.BlockSpec((TOP_K * tm,), lambda i, pe: (i,), memory_space=pltpu.SMEM),
                pl.BlockSpec((tm // SUBLANES, SUBLANES, dh), lambda i, pe: (i, 0, 0)),
            ],
            out_specs=pl.BlockSpec(memory_space=pl.ANY),
            scratch_shapes=[pltpu.VMEM((bm, dh), U32), pltpu.SemaphoreType.DMA(()), pltpu.SemaphoreType.DMA(())],
        ),
        out_shape=jax.ShapeDtypeStruct((n_rows, dh), U32),
        compiler_params=_params(1),
        name="dispatch",
    )(pad_end, dest.reshape(-1), hn2p.reshape(t // SUBLANES, SUBLANES, dh))


def _ffn_body(bexp_ref, nvalid_ref, xs_ref, wg_ref, wu_ref, wd_ref, ys_ref):
    b = pl.program_id(0)

    @pl.when(b < nvalid_ref[0])
    def _():
        dh = xs_ref.shape[1]
        lo, hi = _unpack_bf16_pair(xs_ref[...])
        lo, hi = lo.astype(BF16), hi.astype(BF16)
        g = jnp.dot(lo, wg_ref[0, :dh, :], preferred_element_type=F32)
        g = g + jnp.dot(hi, wg_ref[0, dh:, :], preferred_element_type=F32)
        u = jnp.dot(lo, wu_ref[0, :dh, :], preferred_element_type=F32)
        u = u + jnp.dot(hi, wu_ref[0, dh:, :], preferred_element_type=F32)
        hid = (jax.nn.silu(g) * u).astype(BF16)
        y = jnp.dot(hid, wd_ref[0], preferred_element_type=F32)
        ys_ref[...] = _pack_bf16_pair(y[:, :dh], y[:, dh:])


def _expert_ffn(xs, wg, wu, wd, block_expert, n_valid, *, bm):
    n_rows, dh = xs.shape
    n_exp, d, de = wg.shape
    nb = n_rows // bm
    blk = lambda b, be, nv: (jnp.minimum(b, nv[0] - 1), 0)
    wmap = lambda b, be, nv: (be[jnp.minimum(b, nv[0] - 1)], 0, 0)
    return pl.pallas_call(
        _ffn_body,
        grid_spec=pltpu.PrefetchScalarGridSpec(
            num_scalar_prefetch=2,
            grid=(nb,),
            in_specs=[
                pl.BlockSpec((bm, dh), blk),
                pl.BlockSpec((1, d, de), wmap),
                pl.BlockSpec((1, d, de), wmap),
                pl.BlockSpec((1, de, d), wmap),
            ],
            out_specs=pl.BlockSpec((bm, dh), blk),
        ),
        out_shape=jax.ShapeDtypeStruct((n_rows, dh), U32),
        compiler_params=_params(1),
        name="expert_ffn",
    )(block_expert, n_valid, xs, wg, wu, wd)


def _combine_body(dest_ref, dest_next_ref, h_ref, cw_ref, gain_ref, ys_ref, o_ref, buf_ref, sem):
    i = pl.program_id(0)
    last = pl.num_programs(0) - 1
    tm, d = h_ref.shape
    dh = d // 2
    slot = i % 2
    per = ROW_DMA_UNROLL

    def gather_into(dst_slot):
        def copy_for(k, g, s, row):
            return pltpu.make_async_copy(ys_ref.at[pl.ds(row, 1), :], buf_ref.at[dst_slot, k, g, pl.ds(s, 1), :],
                                         sem.at[dst_slot])
        return copy_for

    @pl.when(i == 0)
    def _first_tile():
        _issue_row_copies(dest_ref, tm, gather_into(0))

    for k in range(TOP_K):
        pltpu.make_async_copy(buf_ref.at[slot, k], buf_ref.at[slot, k], sem.at[slot]).wait()

    def finish_rows(a):
        for g in range(per // SUBLANES):
            grp = a * (per // SUBLANES) + g
            rows = pl.ds(pl.multiple_of(grp * SUBLANES, SUBLANES), SUBLANES)
            lo1, hi1 = _unpack_bf16_pair(buf_ref[slot, 0, grp])
            lo2, hi2 = _unpack_bf16_pair(buf_ref[slot, 1, grp])
            w1 = cw_ref[rows, 0:1]
            w2 = cw_ref[rows, 1:2]
            o_lo = h_ref[rows, :dh] + (w1 * lo1 + w2 * lo2)
            o_hi = h_ref[rows, dh:] + (w1 * hi1 + w2 * hi2)
            ssq = jnp.sum(o_lo * o_lo, axis=-1, keepdims=True) + jnp.sum(o_hi * o_hi, axis=-1, keepdims=True)
            r = lax.rsqrt(ssq / d + NORM_EPS)
            o_ref[rows, :dh] = o_lo * r * gain_ref[:, :dh]
            o_ref[rows, dh:] = o_hi * r * gain_ref[:, dh:]

    @pl.when(i < last)
    def _overlapped():
        _issue_row_copies(dest_next_ref, tm, gather_into(1 - slot), per_trip=finish_rows)

    @pl.when(i == last)
    def _tail():
        def trip(a, carry):
            finish_rows(a)
            return carry
        lax.fori_loop(0, tm // per, trip, 0)


def _combine(ys, dest, hmid, cw_t, gain, *, tm, row_off, n_rows):
    d = hmid.shape[1]
    dh = d // 2
    off, nt = row_off // tm, n_rows // tm
    dest_flat = dest.reshape(-1)
    return pl.pallas_call(
        _combine_body,
        grid=(nt,),
        in_specs=[
            pl.BlockSpec((TOP_K * tm,), lambda i: (off + i,), memory_space=pltpu.SMEM),
            pl.BlockSpec((TOP_K * tm,), lambda i: (off + jnp.minimum(i + 1, nt - 1),), memory_space=pltpu.SMEM),
            pl.BlockSpec((tm, d), lambda i: (off + i, 0)),
            pl.BlockSpec((tm, TOP_K), lambda i: (off + i, 0)),
            pl.BlockSpec((1, d), lambda i: (0, 0)),
            pl.BlockSpec(memory_space=pl.ANY),
        ],
        out_specs=pl.BlockSpec((tm, d), lambda i: (i, 0)),
        out_shape=jax.ShapeDtypeStruct((n_rows, d), F32),
        scratch_shapes=[pltpu.VMEM((2, TOP_K, tm // SUBLANES, SUBLANES, dh), U32), pltpu.SemaphoreType.DMA((2,))],
        compiler_params=_params(1),
        name="combine",
    )(dest_flat, dest_flat, hmid, cw_t, gain, ys)


def _rope_tables(n_tokens):
    rows = n_tokens // GRID_W
    row = jnp.repeat(jnp.arange(rows, dtype=F32), GRID_W)
    col = jnp.tile(jnp.arange(GRID_W, dtype=F32), rows)
    inv_freq = ROPE_THETA ** (-jnp.arange(ROPE_HALF, dtype=F32) / ROPE_HALF)
    ar, ac = row[:, None] * inv_freq, col[:, None] * inv_freq
    cos = jnp.concatenate([jnp.cos(ar), jnp.cos(ar), jnp.cos(ac), jnp.cos(ac)], axis=1)
    sin = jnp.concatenate([-jnp.sin(ar), jnp.sin(ar), -jnp.sin(ac), jnp.sin(ac)], axis=1)
    return cos, sin


def _tile(n, pref):
    return min(n, pref)


def kernel(x_prompt, x_sample, meta_tokens, mix_norm, w_in, q_gain, k_gain, w_attn_o, w_fourier_o, w_out,
           moe_norm, w_router_group, w_router_expert, w_expert_gate, w_expert_up, w_expert_down, final_norm):
    bp, n_p, d = x_prompt.shape
    bs, n_s, _ = x_sample.shape
    n_meta = meta_tokens.shape[0]
    t_p, t_s = bp * n_p, bs * n_s
    t = t_p + t_s
    f_dim = d // 2
    cg = f_dim // N_FOURIER_GROUPS
    n_groups = w_router_group.shape[-1]
    n_exp = w_router_expert.shape[-1]
    assert n_exp == n_groups * EXPERTS_PER_GROUP and n_groups <= 8

    tm = _tile(math.gcd(n_p, n_s), 512)
    bm = _tile(t, 512)

    w_in_b = w_in[0].astype(BF16)
    wa_b, wf_b, wo_b = w_attn_o[0].astype(BF16), w_fourier_o[0].astype(BF16), w_out[0].astype(BF16)
    qg = (q_gain[0] * (HEAD_DIM ** -0.5 * LOG2E)).reshape(1, HEAD_DIM)
    kg = k_gain[0].reshape(1, HEAD_DIM)
    wr_t = jnp.zeros((8 + n_exp, d), F32)
    wr_t = wr_t.at[:n_groups].set(w_router_group[0].T).at[8:].set(w_router_expert[0].T).astype(BF16)
    tri = (jnp.arange(tm)[:, None] < jnp.arange(tm)[None, :]).astype(BF16)
    cos, sin = _rope_tables(max(n_p, n_s))
    cdc, cds = _chan_dft(cg)
    gain_in = mix_norm[0].reshape(1, d)

    xp, xs_in = x_prompt.reshape(t_p, d), x_sample.reshape(t_s, d)
    tm_in = _tile(math.gcd(t_p, t_s), INPROJ_ROWS)
    proj = _inproj(xp, gain_in, w_in_b, None, tm=tm_in, row_off=0, t_total=t)
    uf, q, k, v, g = _inproj(xs_in, gain_in, w_in_b, proj, tm=tm_in, row_off=t_p, t_total=t)
    ufm, _, km, vm, _ = _inproj(meta_tokens, gain_in, w_in_b, None, tm=n_meta, row_off=0, t_total=n_meta)

    kinds = ((0, bp, n_p), (t_p, bs, n_s))
    n_kv = k.shape[1] // HEAD_DIM
    steps = [n_seq * n_kv * (n_tok // _tile(n_tok, ATTN_Q_TILE)) for _, n_seq, n_tok in kinds]
    w_exp = [w_expert_gate[0], w_expert_up[0], w_expert_down[0]]
    w_flat = [w.reshape(-1, w.shape[-1]) for w in w_exp]
    fused_cast = all(w.shape[0] % (SUBLANES * sum(steps)) == 0 for w in w_flat)
    cast_rows = [w.shape[0] // sum(steps) for w in w_flat] if fused_cast else []

    attn, z, w_b = None, None, None
    for (row_off, n_seq, n_tok), first_step in zip(kinds, (0, steps[0])):
        res = _attention(q, k, v, km, vm, qg, kg, cos, sin, attn, row_off=row_off, n_seq=n_seq,
                         n_tok=n_tok, tq=_tile(n_tok, ATTN_Q_TILE),
                         cast_src=w_flat if fused_cast else (), cast_prev=w_b, cast_rows=cast_rows,
                         cast_row_off=[first_step * r for r in cast_rows])
        attn, w_b = res[0], (res[1:] if fused_cast else None)
        z = _fourier(uf, ufm, _seq_dft(n_tok, n_meta), cdc, cds, z, row_off=row_off, n_seq=n_seq,
                     n_tok=n_tok, tl=_tile(n_tok, 512), tf=_tile(f_dim, 512))
    if fused_cast:
        wg_b, wu_b, wd_b = [wb.reshape(w.shape) for wb, w in zip(w_b, w_exp)]
    else:
        wg_b, wu_b, wd_b = [w.astype(BF16) for w in w_exp]

    merged = _merge(attn, z, g, wa_b, wf_b, tm=tm)
    hmid, hn2p, eid, cw, rank, counts = _outproj_route(
        merged, xp, xs_in, wo_b, moe_norm[0].reshape(1, d), wr_t, tri, tm=tm, n_groups=n_groups, n_exp=n_exp)

    cnt = counts[:, 0].astype(I32)
    padded = (cnt + bm - 1) // bm * bm
    pad_end = jnp.cumsum(padded).astype(I32)
    pad_start = pad_end - padded
    onehot = eid[..., None] == jnp.arange(n_exp, dtype=I32)
    dest = jnp.sum(jnp.where(onehot, pad_start, 0), axis=-1) + rank
    nb = (t * TOP_K) // bm + n_exp
    n_valid = (pad_end[-1:] // bm).astype(I32)
    blk_start = jnp.arange(nb, dtype=I32) * bm
    block_expert = jnp.minimum(jnp.sum(pad_end[None, :] <= blk_start[:, None], axis=1), n_exp - 1).astype(I32)

    xs = _dispatch(hn2p, dest, pad_end, tm=tm, bm=bm, n_rows=nb * bm)
    ys = _expert_ffn(xs, wg_b, wu_b, wd_b, block_expert, n_valid, bm=bm)
    cw_t = jnp.transpose(cw, (0, 2, 1)).reshape(t, TOP_K)
    fgain = final_norm.reshape(1, d)
    y_p = _combine(ys, dest, hmid, cw_t, fgain, tm=tm, row_off=0, n_rows=t_p)
    y_s = _combine(ys, dest, hmid, cw_t, fgain, tm=tm, row_off=t_p, n_rows=t_s)
    return y_p.reshape(bp, n_p, d), y_s.reshape(bs, n_s, d)
```

```python
import functools
import math

import jax
import jax.numpy as jnp
from jax import lax
from jax.experimental import pallas as pl
from jax.experimental.pallas import tpu as pltpu

F32, BF16, I32, U32 = jnp.float32, jnp.bfloat16, jnp.int32, jnp.uint32

GRID_W = 64
HEAD_DIM = 128
ROPE_HALF = 32
Q_PER_KV = 4
N_FOURIER_GROUPS = 4
ROPE_THETA = 10000.0
NORM_EPS = 1e-6
TOP_K = 2
EXPERTS_PER_GROUP = 8
LOG2E = 1.4426950408889634
ATTN_KV_CHUNK = 512
ATTN_UNIT_ROWS = 256
ATTN_Q_TILE = 512
INPROJ_ROWS = 1024
FOURIER_TILE_ELEMS = 2 * 1024 * 1024
SUBLANES = 8
ROW_DMA_UNROLL = 32

VMEM_LIMIT_BYTES = 56 * 1024 * 1024


def _params(n_axes):
    return pltpu.CompilerParams(dimension_semantics=("arbitrary",) * n_axes,
                                vmem_limit_bytes=VMEM_LIMIT_BYTES)


def _resident(shape):
    return pl.BlockSpec(shape, lambda *_: (0,) * len(shape), pipeline_mode=pl.Buffered(1))


def _pack_bf16_pair(lo, hi):
    lo_bits = lax.bitcast_convert_type(lo.astype(BF16).astype(F32), U32) >> 16
    hi_bits = lax.bitcast_convert_type(hi.astype(BF16).astype(F32), U32) & jnp.uint32(0xFFFF0000)
    return lo_bits | hi_bits


def _unpack_bf16_pair(w):
    lo = lax.bitcast_convert_type(w << 16, F32)
    hi = lax.bitcast_convert_type(w & jnp.uint32(0xFFFF0000), F32)
    return lo, hi


_F_TILES, _Q_TILES, _G_TILES = 2, 4, 8
_N_TILES = _F_TILES + _Q_TILES + 2 + _G_TILES


def _inproj_body(x_ref, gain_ref, w_ref, *rest):
    uf_ref, q_ref, k_ref, v_ref, g_ref, hn_ref = rest[-6:]
    n = pl.program_id(1)

    @pl.when(n == 0)
    def _norm():
        x = x_ref[...]
        y = x * lax.rsqrt(jnp.mean(x * x, axis=-1, keepdims=True) + NORM_EPS)
        hn_ref[...] = (y * gain_ref[...]).astype(BF16)

    q0 = _F_TILES
    k0 = q0 + _Q_TILES

    def project_into(out_ref):
        def branch():
            acc = jnp.dot(hn_ref[...], w_ref[...], preferred_element_type=F32)
            out_ref[...] = acc.astype(out_ref.dtype)
        return branch

    pl.when(n < q0)(project_into(uf_ref))
    pl.when((n >= q0) & (n < k0))(project_into(q_ref))
    pl.when(n == k0)(project_into(k_ref))
    pl.when(n == k0 + 1)(project_into(v_ref))
    pl.when(n > k0 + 1)(project_into(g_ref))


def _inproj(x, gain, w_in, prev, *, tm, row_off, t_total):
    tx, d = x.shape
    tn = d // 4
    f_dim, q_dim, kv_dim = d // 2, d, d // 4
    assert w_in.shape == (d, _N_TILES * tn) and tx % tm == 0 and row_off % tm == 0
    ob = row_off // tm
    q0, k0, g0 = _F_TILES, _F_TILES + _Q_TILES, _F_TILES + _Q_TILES + 2
    in_specs = [
        pl.BlockSpec((tm, d), lambda i, n: (i, 0)),
        pl.BlockSpec((1, d), lambda i, n: (0, 0)),
        pl.BlockSpec((d, tn), lambda i, n: (0, n)),
    ]
    args = [x, gain, w_in]
    aliases = {}
    if prev is not None:
        in_specs += [pl.BlockSpec(memory_space=pl.ANY)] * len(prev)
        aliases = {len(args) + j: j for j in range(len(prev))}
        args += list(prev)
    return pl.pallas_call(
        _inproj_body,
        grid=(tx // tm, _N_TILES),
        in_specs=in_specs,
        out_specs=[
            pl.BlockSpec((tm, tn), lambda i, n: (ob + i, jnp.clip(n, 0, _F_TILES - 1))),
            pl.BlockSpec((tm, tn), lambda i, n: (ob + i, jnp.clip(n - q0, 0, _Q_TILES - 1))),
            pl.BlockSpec((tm, tn), lambda i, n: (ob + i, 0)),
            pl.BlockSpec((tm, tn), lambda i, n: (ob + i, 0)),
            pl.BlockSpec((tm, tn), lambda i, n: (ob + i, jnp.clip(n - g0, 0, _G_TILES - 1))),
        ],
        out_shape=[
            jax.ShapeDtypeStruct((t_total, f_dim), BF16),
            jax.ShapeDtypeStruct((t_total, q_dim), F32),
            jax.ShapeDtypeStruct((t_total, kv_dim), F32),
            jax.ShapeDtypeStruct((t_total, kv_dim), BF16),
            jax.ShapeDtypeStruct((t_total, 2 * d), BF16),
        ],
        scratch_shapes=[pltpu.VMEM((tm, d), BF16)],
        input_output_aliases=aliases,
        compiler_params=_params(2),
        name="inproj",
    )(*args)


def _head_norm(a, gain):
    return a * lax.rsqrt(jnp.mean(a * a, axis=-1, keepdims=True) + NORM_EPS) * gain


def _pair_major(a):
    lead = a.shape[:-1]
    a = a.reshape(lead + (a.shape[-1] // HEAD_DIM, 2, 2, ROPE_HALF))
    return jnp.swapaxes(a, -3, -2).reshape(lead + (-1,))


def _rope(y, cos, sin):
    return y * cos + pltpu.roll(y, HEAD_DIM // 2, 1) * sin


def _attn_body(q_ref, k_ref, v_ref, km_ref, vm_ref, qg_ref, kg_ref, cq_ref, sq_ref, ck_ref, sk_ref, *rest,
               n_cast):
    cast_in = rest[:n_cast]
    kb_ref, kmb_ref, ve_ref, vme_ref, s_ref = rest[-5:]
    o_ref = rest[-6 - n_cast]
    cast_out = rest[len(rest) - 5 - n_cast:len(rest) - 5]
    tq, n_tok = q_ref.shape[0], k_ref.shape[0]

    for w_ref, wb_ref in zip(cast_in, cast_out):
        wb_ref[...] = w_ref[...].astype(BF16)

    tk = min(n_tok, ATTN_KV_CHUNK)
    n_chunks, lanes = n_tok // tk, tk // HEAD_DIM
    nt = (((1,), (1,)), ((), ()))

    @pl.when(pl.program_id(2) == 0)
    def _prepare_keys():
        kb_ref[...] = _rope(_head_norm(k_ref[...], kg_ref[...]), ck_ref[...], sk_ref[...]).astype(BF16)
        kmb_ref[...] = _head_norm(km_ref[...], kg_ref[...]).astype(BF16)
        ve_ref[:, :HEAD_DIM] = v_ref[...]
        ve_ref[:, HEAD_DIM:] = jnp.ones((n_tok, HEAD_DIM), BF16)
        vme_ref[:, :HEAD_DIM] = vm_ref[...]
        vme_ref[:, HEAD_DIM:] = jnp.ones((vm_ref.shape[0], HEAD_DIM), BF16)

    qg = qg_ref[...]
    ru = min(tq, ATTN_UNIT_ROWS)
    units = [(r0, h) for r0 in range(0, tq, ru) for h in range(Q_PER_KV)]

    def scores(u):
        r0, h = units[u]
        rows = slice(r0, r0 + ru)
        qh = _rope(_head_norm(q_ref[rows, h * HEAD_DIM:(h + 1) * HEAD_DIM], qg), cq_ref[rows, :], sq_ref[rows, :])
        qh = qh.astype(BF16)
        mp = None
        for c in range(n_chunks):
            s = lax.dot_general(qh, kb_ref[c * tk:(c + 1) * tk, :], nt, preferred_element_type=F32)
            s_ref[u % 2, :, c * tk:(c + 1) * tk] = s
            for j in range(lanes):
                t = s[:, j * HEAD_DIM:(j + 1) * HEAD_DIM]
                mp = t if mp is None else jnp.maximum(mp, t)
            yield None
        sm = lax.dot_general(qh, kmb_ref[...], nt, preferred_element_type=F32)
        m = jnp.maximum(jnp.max(mp, axis=-1, keepdims=True), jnp.max(sm, axis=-1, keepdims=True))
        yield m, sm

    def outputs(u, m, sm):
        r0, h = units[u]
        mb = jnp.broadcast_to(m, (ru, HEAD_DIM))
        pm = jnp.exp2(sm - m)
        acc = jnp.dot(pm.astype(BF16), vme_ref[...], preferred_element_type=F32)
        for c in range(n_chunks):
            tiles = []
            for j in range(lanes):
                col = c * tk + j * HEAD_DIM
                tiles.append(jnp.exp2(s_ref[u % 2, :, col:col + HEAD_DIM] - mb).astype(BF16))
            acc = acc + jnp.dot(jnp.concatenate(tiles, axis=1), ve_ref[c * tk:(c + 1) * tk, :],
                                preferred_element_type=F32)
            yield None
        o = acc[:, :HEAD_DIM] / acc[:, HEAD_DIM:]
        o_ref[r0:r0 + ru, h * HEAD_DIM:(h + 1) * HEAD_DIM] = o.astype(o_ref.dtype)
        yield None

    prev_out = None
    for u in range(len(units) + 1):
        cur = scores(u) if u < len(units) else None
        stats = None
        for _ in range(n_chunks + 1):
            if cur is not None:
                stats = next(cur)
            if prev_out is not None:
                next(prev_out)
        prev_out = outputs(u, *stats) if cur is not None else None


def _attention(q, k, v, km, vm, qg, kg, cos, sin, prev, *, row_off, n_seq, n_tok, tq,
               cast_src=(), cast_prev=None, cast_rows=(), cast_row_off=()):
    t, q_dim = q.shape
    n_kv = k.shape[1] // HEAD_DIM
    n_meta = km.shape[0]
    gw = Q_PER_KV * HEAD_DIM
    assert row_off % n_tok == 0 and n_tok % tq == 0
    qmap = lambda b, kh, qi: ((row_off + b * n_tok) // tq + qi, kh)
    kmap = lambda b, kh, qi: (row_off // n_tok + b, kh)
    mmap = lambda b, kh, qi: (0, kh)
    const = lambda b, kh, qi: (0, 0)
    qpos = lambda b, kh, qi: (qi, 0)
    in_specs = [
        pl.BlockSpec((tq, gw), qmap),
        pl.BlockSpec((n_tok, HEAD_DIM), kmap),
        pl.BlockSpec((n_tok, HEAD_DIM), kmap),
        pl.BlockSpec((n_meta, HEAD_DIM), mmap),
        pl.BlockSpec((n_meta, HEAD_DIM), mmap),
        pl.BlockSpec((1, HEAD_DIM), const),
        pl.BlockSpec((1, HEAD_DIM), const),
        pl.BlockSpec((tq, HEAD_DIM), qpos),
        pl.BlockSpec((tq, HEAD_DIM), qpos),
        pl.BlockSpec((n_tok, HEAD_DIM), const),
        pl.BlockSpec((n_tok, HEAD_DIM), const),
    ]
    args = [q, k, v, km, vm, qg, kg, cos, sin, cos, sin]
    nq = n_tok // tq
    out_specs = [pl.BlockSpec((tq, gw), qmap)]
    out_shape = [jax.ShapeDtypeStruct((t, q_dim), BF16)]
    for w, rows, off in zip(cast_src, cast_rows, cast_row_off):
        assert off % rows == 0 and rows % SUBLANES == 0
        wmap = lambda b, kh, qi, rows=rows, off=off: (off // rows + (b * n_kv + kh) * nq + qi, 0)
        in_specs.append(pl.BlockSpec((rows, w.shape[1]), wmap))
        args.append(w)
        out_specs.append(pl.BlockSpec((rows, w.shape[1]), wmap))
        out_shape.append(jax.ShapeDtypeStruct(w.shape, BF16))
    aliases = {}
    if prev is not None:
        in_specs.append(pl.BlockSpec(memory_space=pl.ANY))
        args.append(prev)
        aliases[len(args) - 1] = 0
    for j, wprev in enumerate(cast_prev or ()):
        in_specs.append(pl.BlockSpec(memory_space=pl.ANY))
        args.append(wprev)
        aliases[len(args) - 1] = 1 + j
    return pl.pallas_call(
        functools.partial(_attn_body, n_cast=len(cast_src)),
        grid=(n_seq, n_kv, nq),
        in_specs=in_specs,
        out_specs=out_specs,
        out_shape=out_shape,
        scratch_shapes=[
            pltpu.VMEM((n_tok, HEAD_DIM), BF16),
            pltpu.VMEM((n_meta, HEAD_DIM), BF16),
            pltpu.VMEM((n_tok, 2 * HEAD_DIM), BF16),
            pltpu.VMEM((n_meta, 2 * HEAD_DIM), BF16),
            pltpu.VMEM((2, min(tq, ATTN_UNIT_ROWS), n_tok), F32),
        ],
        input_output_aliases=aliases,
        compiler_params=_params(3),
        name="attention",
    )(*args)


def _dftgen_body(tac_ref, tas_ref, tbc_ref, tbs_ref, c_ref, ns_ref):
    ac, asn = tac_ref[0], tas_ref[0]
    bc, bsn = tbc_ref[...], tbs_ref[...]
    c_ref[...] = (ac * bc - asn * bsn).astype(BF16)
    ns_ref[...] = (-(asn * bc + ac * bsn)).astype(BF16)


def _seq_dft(n_tok, n_meta):
    length = n_tok + n_meta
    rb = min(n_tok, HEAD_DIM)
    w = 2.0 * math.pi / length
    scale = 1.0 / math.sqrt(length)
    p_real = n_meta + jnp.arange(n_tok, dtype=I32)

    def angles(p_rows, p_cols):
        return ((p_rows[:, None] * p_cols[None, :]) % length).astype(F32) * w

    ang_a = angles(n_meta + rb * jnp.arange(n_tok // rb, dtype=I32), p_real)
    ang_b = angles(jnp.arange(rb, dtype=I32), p_real)
    ang_m = angles(p_real, jnp.arange(n_meta, dtype=I32))
    tac = (jnp.cos(ang_a) * scale).reshape(n_tok // rb, 1, n_tok)
    tas = (jnp.sin(ang_a) * scale).reshape(n_tok // rb, 1, n_tok)
    blk = pl.BlockSpec((1, 1, n_tok), lambda a: (a, 0, 0))
    full = pl.BlockSpec((rb, n_tok), lambda a: (0, 0))
    cmat, nsmat = pl.pallas_call(
        _dftgen_body,
        grid=(n_tok // rb,),
        in_specs=[blk, blk, full, full],
        out_specs=[pl.BlockSpec((rb, n_tok), lambda a: (a, 0))] * 2,
        out_shape=[jax.ShapeDtypeStruct((n_tok, n_tok), BF16)] * 2,
        compiler_params=_params(1),
        name="dft_matrices",
    )(tac, tas, jnp.cos(ang_b), jnp.sin(ang_b))
    return cmat, nsmat, (jnp.cos(ang_m) * scale).astype(BF16), (-jnp.sin(ang_m) * scale).astype(BF16)


def _chan_dft(cg):
    idx = jnp.arange(cg, dtype=I32)
    ang = ((idx[:, None] * idx[None, :]) % cg).astype(F32) * (2.0 * math.pi / cg)
    scale = 1.0 / math.sqrt(cg)
    return (jnp.cos(ang) * scale).astype(BF16), (jnp.sin(ang) * scale).astype(BF16)


def _fourier_body(c_ref, ns_ref, cm_ref, nsm_ref, u_ref, um_ref, cdc_ref, cds_ref, *rest):
    z_ref = rest[-1]
    u, um = u_ref[...], um_ref[...]
    a = jnp.dot(c_ref[...], u, preferred_element_type=F32) + jnp.dot(cm_ref[...], um, preferred_element_type=F32)
    b = jnp.dot(ns_ref[...], u, preferred_element_type=F32) + jnp.dot(nsm_ref[...], um, preferred_element_type=F32)
    a, b = a.astype(BF16), b.astype(BF16)
    cg = cdc_ref.shape[0]
    for g in range(z_ref.shape[1] // cg):
        sl = slice(g * cg, (g + 1) * cg)
        z = jnp.dot(a[:, sl], cdc_ref[...], preferred_element_type=F32)
        z = z + jnp.dot(b[:, sl], cds_ref[...], preferred_element_type=F32)
        z_ref[:, sl] = z.astype(z_ref.dtype)


def _fourier(uf, ufm, mats, cdc, cds, prev, *, row_off, n_seq, n_tok, tl, tf):
    cmat, nsmat, cmeta, nsmeta = mats
    t, f_dim = uf.shape
    n_meta = ufm.shape[0]
    cg = cdc.shape[0]
    assert row_off % n_tok == 0 and n_tok % tl == 0 and f_dim % tf == 0 and tf % cg == 0
    cmap = lambda b, j, i: (i, 0)
    const = lambda b, j, i: (0, 0)
    in_specs = [
        pl.BlockSpec((tl, n_tok), cmap),
        pl.BlockSpec((tl, n_tok), cmap),
        pl.BlockSpec((tl, n_meta), cmap),
        pl.BlockSpec((tl, n_meta), cmap),
        pl.BlockSpec((n_tok, tf), lambda b, j, i: (row_off // n_tok + b, j)),
        pl.BlockSpec((n_meta, tf), lambda b, j, i: (0, j)),
        pl.BlockSpec((cg, cg), const),
        pl.BlockSpec((cg, cg), const),
    ]
    args = [cmat, nsmat, cmeta, nsmeta, uf, ufm, cdc, cds]
    aliases = {}
    if prev is not None:
        in_specs.append(pl.BlockSpec(memory_space=pl.ANY))
        args.append(prev)
        aliases = {len(args) - 1: 0}
    return pl.pallas_call(
        _fourier_body,
        grid=(n_seq, f_dim // tf, n_tok // tl),
        in_specs=in_specs,
        out_specs=pl.BlockSpec((tl, tf), lambda b, j, i: ((row_off + b * n_tok) // tl + i, j)),
        out_shape=jax.ShapeDtypeStruct((t, f_dim), BF16),
        input_output_aliases=aliases,
        compiler_params=_params(3),
        name="fourier",
    )(*args)


def _merge_body(a_ref, z_ref, ga_ref, gf_ref, wa_ref, wf_ref, o_ref):
    a = jnp.dot(a_ref[...], wa_ref[...], preferred_element_type=F32)
    f = jnp.dot(z_ref[...], wf_ref[...], preferred_element_type=F32)
    sga = jax.nn.sigmoid(ga_ref[...].astype(F32))
    sgf = jax.nn.sigmoid(gf_ref[...].astype(F32))
    o_ref[...] = (sga * a + sgf * f).astype(o_ref.dtype)


def _merge(attn, z, g, wa, wf, *, tm):
    t, d = attn.shape
    f_dim = z.shape[1]
    return pl.pallas_call(
        _merge_body,
        grid=(t // tm,),
        in_specs=[
            pl.BlockSpec((tm, d), lambda i: (i, 0)),
            pl.BlockSpec((tm, f_dim), lambda i: (i, 0)),
            pl.BlockSpec((tm, d), lambda i: (i, 0)),
            pl.BlockSpec((tm, d), lambda i: (i, 1)),
            _resident((d, d)),
            _resident((f_dim, d)),
        ],
        out_specs=pl.BlockSpec((tm, d), lambda i: (i, 0)),
        out_shape=jax.ShapeDtypeStruct((t, d), BF16),
        compiler_params=_params(1),
        name="merge",
    )(attn, z, g, g, wa, wf)


def _route_body(m_ref, xa_ref, xb_ref, wo_ref, gain_ref, wr_ref, tri_ref,
                hmid_ref, hn_ref, eid_ref, cw_ref, rank_ref, cnt_ref, *, n_a_tiles, n_groups):
    i = pl.program_id(0)
    tm, d = xa_ref.shape
    dh = d // 2
    n_exp = cnt_ref.shape[0]

    @pl.when(i == 0)
    def _init():
        cnt_ref[...] = jnp.zeros_like(cnt_ref)

    x = jnp.where(i < n_a_tiles, xa_ref[...], xb_ref[...])
    h = x + jnp.dot(m_ref[...], wo_ref[...], preferred_element_type=F32)
    hmid_ref[...] = h
    hn = h * lax.rsqrt(jnp.mean(h * h, axis=-1, keepdims=True) + NORM_EPS) * gain_ref[...]
    hb = hn.astype(BF16)
    hn_ref[...] = _pack_bf16_pair(hn[:, :dh], hn[:, dh:])

    lt = lax.dot_general(wr_ref[...], hb, (((1,), (1,)), ((), ())), preferred_element_type=F32)
    row8 = lax.broadcasted_iota(I32, (EXPERTS_PER_GROUP, tm), 0)
    neg = jnp.float32(-jnp.inf)
    lg = jnp.where(row8 < n_groups, lt[0:8], neg)
    gmax = jnp.max(lg, axis=0, keepdims=True)
    gidx = jnp.min(jnp.where(lg == gmax, row8, 8), axis=0, keepdims=True)
    p_g = 1.0 / jnp.sum(jnp.exp(lg - gmax), axis=0, keepdims=True)
    sel = lt[8:8 + EXPERTS_PER_GROUP]
    for g in range(1, n_groups):
        sel = jnp.where(gidx == g, lt[8 + g * EXPERTS_PER_GROUP:8 + (g + 1) * EXPERTS_PER_GROUP], sel)
    m1 = jnp.max(sel, axis=0, keepdims=True)
    i1 = jnp.min(jnp.where(sel == m1, row8, 8), axis=0, keepdims=True)
    sel2 = jnp.where(row8 == i1, neg, sel)
    m2 = jnp.max(sel2, axis=0, keepdims=True)
    i2 = jnp.min(jnp.where(sel2 == m2, row8, 8), axis=0, keepdims=True)
    e21 = jnp.exp(m2 - m1)
    p1 = 1.0 / (1.0 + e21)
    p2 = e21 * p1
    e1 = gidx * EXPERTS_PER_GROUP + i1
    e2 = gidx * EXPERTS_PER_GROUP + i2
    eid_ref[0, 0:1, :] = e1
    eid_ref[0, 1:2, :] = e2
    cw_ref[0, 0:1, :] = p_g * p1
    cw_ref[0, 1:2, :] = p_g * p2

    row_e = lax.broadcasted_iota(I32, (n_exp, tm), 0)
    oh1 = (row_e == e1)
    oh2 = (row_e == e2)
    cs1 = jnp.dot(oh1.astype(BF16), tri_ref[...], preferred_element_type=F32)
    cs2 = jnp.dot(oh2.astype(BF16), tri_ref[...], preferred_element_type=F32)
    oh1f, oh2f = oh1.astype(F32), oh2.astype(F32)
    c1 = jnp.sum(oh1f, axis=1, keepdims=True)
    c2 = jnp.sum(oh2f, axis=1, keepdims=True)
    base = cnt_ref[:, 0:1]
    rank_ref[0, 0:1, :] = jnp.sum(oh1f * (base + cs1), axis=0, keepdims=True).astype(I32)
    rank_ref[0, 1:2, :] = jnp.sum(oh2f * (base + c1 + cs2), axis=0, keepdims=True).astype(I32)
    cnt_ref[...] = cnt_ref[...] + (c1 + c2)


def _outproj_route(merged, xa, xb, wout, gain, wr_t, tri, *, tm, n_groups, n_exp):
    t, d = merged.shape
    nt = t // tm
    na = xa.shape[0] // tm
    tok3 = lambda i: (i, 0, 0)
    body = functools.partial(_route_body, n_a_tiles=na, n_groups=n_groups)
    return pl.pallas_call(
        body,
        grid=(nt,),
        in_specs=[
            pl.BlockSpec((tm, d), lambda i: (i, 0)),
            pl.BlockSpec((tm, d), lambda i: (jnp.minimum(i, na - 1), 0)),
            pl.BlockSpec((tm, d), lambda i: (jnp.maximum(i - na, 0), 0)),
            _resident((d, d)),
            pl.BlockSpec((1, d), lambda i: (0, 0)),
            _resident(wr_t.shape),
            _resident((tm, tm)),
        ],
        out_specs=[
            pl.BlockSpec((tm, d), lambda i: (i, 0)),
            pl.BlockSpec((tm, d // 2), lambda i: (i, 0)),
            pl.BlockSpec((1, TOP_K, tm), tok3),
            pl.BlockSpec((1, TOP_K, tm), tok3),
            pl.BlockSpec((1, TOP_K, tm), tok3),
            pl.BlockSpec((n_exp, HEAD_DIM), lambda i: (0, 0)),
        ],
        out_shape=[
            jax.ShapeDtypeStruct((t, d), F32),
            jax.ShapeDtypeStruct((t, d // 2), U32),
            jax.ShapeDtypeStruct((nt, TOP_K, tm), I32),
            jax.ShapeDtypeStruct((nt, TOP_K, tm), F32),
            jax.ShapeDtypeStruct((nt, TOP_K, tm), I32),
            jax.ShapeDtypeStruct((n_exp, HEAD_DIM), F32),
        ],
        compiler_params=_params(1),
        name="outproj_route",
    )(merged, xa, xb, wout, gain, wr_t, tri)


def _issue_row_copies(idx_ref, n_rows, copy_for, per_trip=None):
    per = ROW_DMA_UNROLL

    def issue(a, carry):
        if per_trip is not None:
            per_trip(a)
        base = a * per
        for j in range(per):
            for k in range(TOP_K):
                copy_for(k, a * (per // SUBLANES) + j // SUBLANES, j % SUBLANES,
                         idx_ref[k * n_rows + base + j]).start()
        return carry

    lax.fori_loop(0, n_rows // per, issue, 0)


def _dispatch_body(pad_end_ref, dest_ref, x_ref, xs_ref, zbuf_ref, zsem, sem, *, bm, n_exp):
    i = pl.program_id(0)
    tm = x_ref.shape[0] * SUBLANES

    @pl.when(i == 0)
    def _zero_padding():
        zbuf_ref[...] = jnp.zeros_like(zbuf_ref)

        def last_block_copy(e):
            end = pad_end_ref[e]
            start = pad_end_ref[e - 1] if e > 0 else 0
            cp = pltpu.make_async_copy(zbuf_ref, xs_ref.at[pl.ds(pl.multiple_of(end - bm, bm), bm)], zsem)
            return end > start, cp

        for e in range(n_exp):
            nonempty, cp = last_block_copy(e)
            pl.when(nonempty)(cp.start)
        for e in range(n_exp):
            nonempty, cp = last_block_copy(e)
            pl.when(nonempty)(cp.wait)

    def copy_for(k, g, s, d):
        return pltpu.make_async_copy(x_ref.at[g, pl.ds(s, 1), :], xs_ref.at[pl.ds(d, 1), :], sem)

    _issue_row_copies(dest_ref, tm, copy_for)
    for k in range(TOP_K):
        pltpu.make_async_copy(x_ref, x_ref, sem).wait()


def _dispatch(hn2p, dest, pad_end, *, tm, bm, n_rows):
    t, dh = hn2p.shape
    n_exp = pad_end.shape[0]
    body = functools.partial(_dispatch_body, bm=bm, n_exp=n_exp)
    return pl.pallas_call(
        body,
        grid_spec=pltpu.PrefetchScalarGridSpec(
            num_scalar_prefetch=1,
            grid=(t // tm,),
            in_specs=[
                pl.BlockSpec((TOP_K * tm,), lambda i, pe: (i,), memory_space=pltpu.SMEM),
                pl.BlockSpec((tm // SUBLANES, SUBLANES, dh), lambda i, pe: (i, 0, 0)),
            ],
            out_specs=pl.BlockSpec(memory_space=pl.ANY),
            scratch_shapes=[pltpu.VMEM((bm, dh), U32), pltpu.SemaphoreType.DMA(()), pltpu.SemaphoreType.DMA(())],
        ),
        out_shape=jax.ShapeDtypeStruct((n_rows, dh), U32),
        compiler_params=_params(1),
        name="dispatch",
    )(pad_end, dest.reshape(-1), hn2p.reshape(t // SUBLANES, SUBLANES, dh))


def _ffn_body(bexp_ref, nvalid_ref, xs_ref, wg_ref, wu_ref, wd_ref, ys_ref):
    b = pl.program_id(0)

    @pl.when(b < nvalid_ref[0])
    def _():
        dh = xs_ref.shape[1]
        lo, hi = _unpack_bf16_pair(xs_ref[...])
        lo, hi = lo.astype(BF16), hi.astype(BF16)
        g = jnp.dot(lo, wg_ref[0, :dh, :], preferred_element_type=F32)
        g = g + jnp.dot(hi, wg_ref[0, dh:, :], preferred_element_type=F32)
        u = jnp.dot(lo, wu_ref[0, :dh, :], preferred_element_type=F32)
        u = u + jnp.dot(hi, wu_ref[0, dh:, :], preferred_element_type=F32)
        hid = (jax.nn.silu(g) * u).astype(BF16)
        y = jnp.dot(hid, wd_ref[0], preferred_element_type=F32)
        ys_ref[...] = _pack_bf16_pair(y[:, :dh], y[:, dh:])


def _expert_ffn(xs, wg, wu, wd, block_expert, n_valid, *, bm):
    n_rows, dh = xs.shape
    n_exp, d, de = wg.shape
    nb = n_rows // bm
    blk = lambda b, be, nv: (jnp.minimum(b, nv[0] - 1), 0)
    wmap = lambda b, be, nv: (be[jnp.minimum(b, nv[0] - 1)], 0, 0)
    return pl.pallas_call(
        _ffn_body,
        grid_spec=pltpu.PrefetchScalarGridSpec(
            num_scalar_prefetch=2,
            grid=(nb,),
            in_specs=[
                pl.BlockSpec((bm, dh), blk),
                pl.BlockSpec((1, d, de), wmap),
                pl.BlockSpec((1, d, de), wmap),
                pl.BlockSpec((1, de, d), wmap),
            ],
            out_specs=pl.BlockSpec((bm, dh), blk),
        ),
        out_shape=jax.ShapeDtypeStruct((n_rows, dh), U32),
        compiler_params=_params(1),
        name="expert_ffn",
    )(block_expert, n_valid, xs, wg, wu, wd)


def _combine_body(dest_ref, dest_next_ref, h_ref, cw_ref, gain_ref, ys_ref, o_ref, buf_ref, sem):
    i = pl.program_id(0)
    last = pl.num_programs(0) - 1
    tm, d = h_ref.shape
    dh = d // 2
    slot = i % 2
    per = ROW_DMA_UNROLL

    def gather_into(dst_slot):
        def copy_for(k, g, s, row):
            return pltpu.make_async_copy(ys_ref.at[pl.ds(row, 1), :], buf_ref.at[dst_slot, k, g, pl.ds(s, 1), :],
                                         sem.at[dst_slot])
        return copy_for

    @pl.when(i == 0)
    def _first_tile():
        _issue_row_copies(dest_ref, tm, gather_into(0))

    for k in range(TOP_K):
        pltpu.make_async_copy(buf_ref.at[slot, k], buf_ref.at[slot, k], sem.at[slot]).wait()

    def finish_rows(a):
        for g in range(per // SUBLANES):
            grp = a * (per // SUBLANES) + g
            rows = pl.ds(pl.multiple_of(grp * SUBLANES, SUBLANES), SUBLANES)
            lo1, hi1 = _unpack_bf16_pair(buf_ref[slot, 0, grp])
            lo2, hi2 = _unpack_bf16_pair(buf_ref[slot, 1, grp])
            w1 = cw_ref[rows, 0:1]
            w2 = cw_ref[rows, 1:2]
            o_lo = h_ref[rows, :dh] + (w1 * lo1 + w2 * lo2)
            o_hi = h_ref[rows, dh:] + (w1 * hi1 + w2 * hi2)
            ssq = jnp.sum(o_lo * o_lo, axis=-1, keepdims=True) + jnp.sum(o_hi * o_hi, axis=-1, keepdims=True)
            r = lax.rsqrt(ssq / d + NORM_EPS)
            o_ref[rows, :dh] = o_lo * r * gain_ref[:, :dh]
            o_ref[rows, dh:] = o_hi * r * gain_ref[:, dh:]

    @pl.when(i < last)
    def _overlapped():
        _issue_row_copies(dest_next_ref, tm, gather_into(1 - slot), per_trip=finish_rows)

    @pl.when(i == last)
    def _tail():
        def trip(a, carry):
            finish_rows(a)
            return carry
        lax.fori_loop(0, tm // per, trip, 0)


def _combine(ys, dest, hmid, cw_t, gain, *, tm, row_off, n_rows):
    d = hmid.shape[1]
    dh = d // 2
    off, nt = row_off // tm, n_rows // tm
    dest_flat = dest.reshape(-1)
    return pl.pallas_call(
        _combine_body,
        grid=(nt,),
        in_specs=[
            pl.BlockSpec((TOP_K * tm,), lambda i: (off + i,), memory_space=pltpu.SMEM),
            pl.BlockSpec((TOP_K * tm,), lambda i: (off + jnp.minimum(i + 1, nt - 1),), memory_space=pltpu.SMEM),
            pl.BlockSpec((tm, d), lambda i: (off + i, 0)),
            pl.BlockSpec((tm, TOP_K), lambda i: (off + i, 0)),
            pl.BlockSpec((1, d), lambda i: (0, 0)),
            pl.BlockSpec(memory_space=pl.ANY),
        ],
        out_specs=pl.BlockSpec((tm, d), lambda i: (i, 0)),
        out_shape=jax.ShapeDtypeStruct((n_rows, d), F32),
        scratch_shapes=[pltpu.VMEM((2, TOP_K, tm // SUBLANES, SUBLANES, dh), U32), pltpu.SemaphoreType.DMA((2,))],
        compiler_params=_params(1),
        name="combine",
    )(dest_flat, dest_flat, hmid, cw_t, gain, ys)


def _rope_tables(n_tokens):
    rows = n_tokens // GRID_W
    row = jnp.repeat(jnp.arange(rows, dtype=F32), GRID_W)
    col = jnp.tile(jnp.arange(GRID_W, dtype=F32), rows)
    inv_freq = ROPE_THETA ** (-jnp.arange(ROPE_HALF, dtype=F32) / ROPE_HALF)
    ar, ac = row[:, None] * inv_freq, col[:, None] * inv_freq
    cos = jnp.concatenate([jnp.cos(ar), jnp.cos(ac), jnp.cos(ar), jnp.cos(ac)], axis=1)
    sin = jnp.concatenate([-jnp.sin(ar), -jnp.sin(ac), jnp.sin(ar), jnp.sin(ac)], axis=1)
    return cos, sin


def _tile(n, pref):
    return min(n, pref)


def kernel(x_prompt, x_sample, meta_tokens, mix_norm, w_in, q_gain, k_gain, w_attn_o, w_fourier_o, w_out,
           moe_norm, w_router_group, w_router_expert, w_expert_gate, w_expert_up, w_expert_down, final_norm):
    bp, n_p, d = x_prompt.shape
    bs, n_s, _ = x_sample.shape
    n_meta = meta_tokens.shape[0]
    t_p, t_s = bp * n_p, bs * n_s
    t = t_p + t_s
    f_dim = d // 2
    cg = f_dim // N_FOURIER_GROUPS
    n_groups = w_router_group.shape[-1]
    n_exp = w_router_expert.shape[-1]
    assert n_exp == n_groups * EXPERTS_PER_GROUP and n_groups <= 8

    tm = _tile(math.gcd(n_p, n_s), 512)
    bm = _tile(t, 512)

    q_lo, k_lo, v_lo = f_dim, f_dim + d, f_dim + d + d // 4
    w0 = w_in[0]
    w_in_b = jnp.concatenate([w0[:, :q_lo], _pair_major(w0[:, q_lo:k_lo]), _pair_major(w0[:, k_lo:v_lo]),
                              w0[:, v_lo:]], axis=1).astype(BF16)
    wa_b, wf_b, wo_b = w_attn_o[0].astype(BF16), w_fourier_o[0].astype(BF16), w_out[0].astype(BF16)
    qg = (_pair_major(q_gain[0]) * (HEAD_DIM ** -0.5 * LOG2E)).reshape(1, HEAD_DIM)
    kg = _pair_major(k_gain[0]).reshape(1, HEAD_DIM)
    wr_t = jnp.zeros((8 + n_exp, d), F32)
    wr_t = wr_t.at[:n_groups].set(w_router_group[0].T).at[8:].set(w_router_expert[0].T).astype(BF16)
    tri = (jnp.arange(tm)[:, None] < jnp.arange(tm)[None, :]).astype(BF16)
    cos, sin = _rope_tables(max(n_p, n_s))
    cdc, cds = _chan_dft(cg)
    gain_in = mix_norm[0].reshape(1, d)

    xp, xs_in = x_prompt.reshape(t_p, d), x_sample.reshape(t_s, d)
    tm_in = _tile(math.gcd(t_p, t_s), INPROJ_ROWS)
    proj = _inproj(xp, gain_in, w_in_b, None, tm=tm_in, row_off=0, t_total=t)
    uf, q, k, v, g = _inproj(xs_in, gain_in, w_in_b, proj, tm=tm_in, row_off=t_p, t_total=t)
    ufm, _, km, vm, _ = _inproj(meta_tokens, gain_in, w_in_b, None, tm=n_meta, row_off=0, t_total=n_meta)

    kinds = ((0, bp, n_p), (t_p, bs, n_s))
    n_kv = k.shape[1] // HEAD_DIM
    steps = [n_seq * n_kv * (n_tok // _tile(n_tok, ATTN_Q_TILE)) for _, n_seq, n_tok in kinds]
    w_exp = [w_expert_gate[0], w_expert_up[0], w_expert_down[0]]
    w_flat = [w.reshape(-1, w.shape[-1]) for w in w_exp]
    fused_cast = all(w.shape[0] % (SUBLANES * sum(steps)) == 0 for w in w_flat)
    cast_rows = [w.shape[0] // sum(steps) for w in w_flat] if fused_cast else []

    attn, z, w_b = None, None, None
    for (row_off, n_seq, n_tok), first_step in zip(kinds, (0, steps[0])):
        res = _attention(q, k, v, km, vm, qg, kg, cos, sin, attn, row_off=row_off, n_seq=n_seq,
                         n_tok=n_tok, tq=_tile(n_tok, ATTN_Q_TILE),
                         cast_src=w_flat if fused_cast else (), cast_prev=w_b, cast_rows=cast_rows,
                         cast_row_off=[first_step * r for r in cast_rows])
        attn, w_b = res[0], (res[1:] if fused_cast else None)
        z = _fourier(uf, ufm, _seq_dft(n_tok, n_meta), cdc, cds, z, row_off=row_off, n_seq=n_seq,
                     n_tok=n_tok, tl=_tile(n_tok, max(512, FOURIER_TILE_ELEMS // n_tok)), tf=_tile(f_dim, 512))
    if fused_cast:
        wg_b, wu_b, wd_b = [wb.reshape(w.shape) for wb, w in zip(w_b, w_exp)]
    else:
        wg_b, wu_b, wd_b = [w.astype(BF16) for w in w_exp]

    merged = _merge(attn, z, g, wa_b, wf_b, tm=tm)
    hmid, hn2p, eid, cw, rank, counts = _outproj_route(
        merged, xp, xs_in, wo_b, moe_norm[0].reshape(1, d), wr_t, tri, tm=tm, n_groups=n_groups, n_exp=n_exp)

    cnt = counts[:, 0].astype(I32)
    padded = (cnt + bm - 1) // bm * bm
    pad_end = jnp.cumsum(padded).astype(I32)
    pad_start = pad_end - padded
    onehot = eid[..., None] == jnp.arange(n_exp, dtype=I32)
    dest = jnp.sum(jnp.where(onehot, pad_start, 0), axis=-1) + rank
    nb = (t * TOP_K) // bm + n_exp
    n_valid = (pad_end[-1:] // bm).astype(I32)
    blk_start = jnp.arange(nb, dtype=I32) * bm
    block_expert = jnp.minimum(jnp.sum(pad_end[None, :] <= blk_start[:, None], axis=1), n_exp - 1).astype(I32)

    xs = _dispatch(hn2p, dest, pad_end, tm=tm, bm=bm, n_rows=nb * bm)
    ys = _expert_ffn(xs, wg_b, wu_b, wd_b, block_expert, n_valid, bm=bm)
    cw_t = jnp.transpose(cw, (0, 2, 1)).reshape(t, TOP_K)
    fgain = final_norm.reshape(1, d)
    y_p = _combine(ys, dest, hmid, cw_t, fgain, tm=tm, row_off=0, n_rows=t_p)
    y_s = _combine(ys, dest, hmid, cw_t, fgain, tm=tm, row_off=t_p, n_rows=t_s)
    return y_p.reshape(bp, n_p, d), y_s.reshape(bs, n_s, d)
```

```python
import functools
import math

import jax
import jax.numpy as jnp
from jax import lax
from jax.experimental import pallas as pl
from jax.experimental.pallas import tpu as pltpu

F32, BF16, I32, U32 = jnp.float32, jnp.bfloat16, jnp.int32, jnp.uint32

GRID_W = 64
HEAD_DIM = 128
ROPE_HALF = 32
Q_PER_KV = 4
N_FOURIER_GROUPS = 4
ROPE_THETA = 10000.0
NORM_EPS = 1e-6
TOP_K = 2
EXPERTS_PER_GROUP = 8
LOG2E = 1.4426950408889634
ATTN_KV_CHUNK = 512
ATTN_UNIT_ROWS = 256
ATTN_Q_TILE = 512
INPROJ_ROWS = 1024
FOURIER_TILE_ELEMS = 2 * 1024 * 1024
SUBLANES = 8
ROW_DMA_UNROLL = 32

VMEM_LIMIT_BYTES = 56 * 1024 * 1024


def _params(n_axes):
    return pltpu.CompilerParams(dimension_semantics=("arbitrary",) * n_axes,
                                vmem_limit_bytes=VMEM_LIMIT_BYTES)


def _resident(shape):
    return pl.BlockSpec(shape, lambda *_: (0,) * len(shape), pipeline_mode=pl.Buffered(1))


def _pack_bf16_pair(lo, hi):
    lo_bits = lax.bitcast_convert_type(lo.astype(BF16).astype(F32), U32) >> 16
    hi_bits = lax.bitcast_convert_type(hi.astype(BF16).astype(F32), U32) & jnp.uint32(0xFFFF0000)
    return lo_bits | hi_bits


def _unpack_bf16_pair(w):
    lo = lax.bitcast_convert_type(w << 16, F32)
    hi = lax.bitcast_convert_type(w & jnp.uint32(0xFFFF0000), F32)
    return lo, hi


_F_TILES, _Q_TILES, _G_TILES = 2, 4, 8
_N_TILES = _F_TILES + _Q_TILES + 2 + _G_TILES


def _inproj_body(x_ref, gain_ref, w_ref, *rest):
    uf_ref, q_ref, k_ref, v_ref, g_ref, hn_ref, xbuf_ref, xsem = rest[-8:]
    i, n = pl.program_id(0), pl.program_id(1)
    tm = hn_ref.shape[0]
    slot = i % 2

    def x_copy(tile, dst_slot):
        rows = pl.ds(pl.multiple_of(tile * tm, tm), tm)
        return pltpu.make_async_copy(x_ref.at[rows], xbuf_ref.at[dst_slot], xsem.at[dst_slot])

    @pl.when(n == 0)
    def _norm():
        pl.when(i == 0)(x_copy(0, 0).start)
        x_copy(i, slot).wait()
        pl.when(i + 1 < pl.num_programs(0))(x_copy(i + 1, 1 - slot).start)
        x = xbuf_ref[slot]
        y = x * lax.rsqrt(jnp.mean(x * x, axis=-1, keepdims=True) + NORM_EPS)
        hn_ref[...] = (y * gain_ref[...]).astype(BF16)

    q0 = _F_TILES
    k0 = q0 + _Q_TILES

    def project_into(out_ref):
        def branch():
            acc = jnp.dot(hn_ref[...], w_ref[...], preferred_element_type=F32)
            out_ref[...] = acc.astype(out_ref.dtype)
        return branch

    pl.when(n < q0)(project_into(uf_ref))
    pl.when((n >= q0) & (n < k0))(project_into(q_ref))
    pl.when(n == k0)(project_into(k_ref))
    pl.when(n == k0 + 1)(project_into(v_ref))
    pl.when(n > k0 + 1)(project_into(g_ref))


def _inproj(x, gain, w_in, prev, *, tm, row_off, t_total):
    tx, d = x.shape
    tn = d // 4
    f_dim, q_dim, kv_dim = d // 2, d, d // 4
    assert w_in.shape == (d, _N_TILES * tn) and tx % tm == 0 and row_off % tm == 0
    ob = row_off // tm
    q0, k0, g0 = _F_TILES, _F_TILES + _Q_TILES, _F_TILES + _Q_TILES + 2
    in_specs = [
        pl.BlockSpec(memory_space=pl.ANY),
        pl.BlockSpec((1, d), lambda i, n: (0, 0)),
        pl.BlockSpec((d, tn), lambda i, n: (0, n)),
    ]
    args = [x, gain, w_in]
    aliases = {}
    if prev is not None:
        in_specs += [pl.BlockSpec(memory_space=pl.ANY)] * len(prev)
        aliases = {len(args) + j: j for j in range(len(prev))}
        args += list(prev)
    return pl.pallas_call(
        _inproj_body,
        grid=(tx // tm, _N_TILES),
        in_specs=in_specs,
        out_specs=[
            pl.BlockSpec((tm, tn), lambda i, n: (ob + i, jnp.clip(n, 0, _F_TILES - 1))),
            pl.BlockSpec((tm, tn), lambda i, n: (ob + i, jnp.clip(n - q0, 0, _Q_TILES - 1))),
            pl.BlockSpec((tm, tn), lambda i, n: (ob + i, 0)),
            pl.BlockSpec((tm, tn), lambda i, n: (ob + i, 0)),
            pl.BlockSpec((tm, tn), lambda i, n: (ob + i, jnp.clip(n - g0, 0, _G_TILES - 1))),
        ],
        out_shape=[
            jax.ShapeDtypeStruct((t_total, f_dim), BF16),
            jax.ShapeDtypeStruct((t_total, q_dim), F32),
            jax.ShapeDtypeStruct((t_total, kv_dim), F32),
            jax.ShapeDtypeStruct((t_total, kv_dim), BF16),
            jax.ShapeDtypeStruct((t_total, 2 * d), BF16),
        ],
        scratch_shapes=[pltpu.VMEM((tm, d), BF16), pltpu.VMEM((2, tm, d), F32), pltpu.SemaphoreType.DMA((2,))],
        input_output_aliases=aliases,
        compiler_params=_params(2),
        name="inproj",
    )(*args)


def _head_norm(a, gain):
    return a * lax.rsqrt(jnp.mean(a * a, axis=-1, keepdims=True) + NORM_EPS) * gain


def _pair_major(a):
    lead = a.shape[:-1]
    a = a.reshape(lead + (a.shape[-1] // HEAD_DIM, 2, 2, ROPE_HALF))
    return jnp.swapaxes(a, -3, -2).reshape(lead + (-1,))


def _rope(y, cos, sin):
    return y * cos + pltpu.roll(y, HEAD_DIM // 2, 1) * sin


def _attn_body(q_ref, k_ref, v_ref, km_ref, vm_ref, qg_ref, kg_ref, cq_ref, sq_ref, ck_ref, sk_ref, *rest,
               n_cast):
    cast_in = rest[:n_cast]
    kb_ref, kmb_ref, ve_ref, vme_ref, s_ref = rest[-5:]
    o_ref = rest[-6 - n_cast]
    cast_out = rest[len(rest) - 5 - n_cast:len(rest) - 5]
    tq, n_tok = q_ref.shape[0], k_ref.shape[0]

    for w_ref, wb_ref in zip(cast_in, cast_out):
        wb_ref[...] = w_ref[...].astype(BF16)

    tk = min(n_tok, ATTN_KV_CHUNK)
    n_chunks, lanes = n_tok // tk, tk // HEAD_DIM
    nt = (((1,), (1,)), ((), ()))

    @pl.when(pl.program_id(2) == 0)
    def _prepare_keys():
        kb_ref[...] = _rope(_head_norm(k_ref[...], kg_ref[...]), ck_ref[...], sk_ref[...]).astype(BF16)
        kmb_ref[...] = _head_norm(km_ref[...], kg_ref[...]).astype(BF16)
        ve_ref[:, :HEAD_DIM] = v_ref[...]
        ve_ref[:, HEAD_DIM:] = jnp.ones((n_tok, HEAD_DIM), BF16)
        vme_ref[:, :HEAD_DIM] = vm_ref[...]
        vme_ref[:, HEAD_DIM:] = jnp.ones((vm_ref.shape[0], HEAD_DIM), BF16)

    qg = qg_ref[...]
    ru = min(tq, ATTN_UNIT_ROWS)
    units = [(r0, h) for r0 in range(0, tq, ru) for h in range(Q_PER_KV)]

    def scores(u):
        r0, h = units[u]
        rows = slice(r0, r0 + ru)
        qh = _rope(_head_norm(q_ref[rows, h * HEAD_DIM:(h + 1) * HEAD_DIM], qg), cq_ref[rows, :], sq_ref[rows, :])
        qh = qh.astype(BF16)
        mp = None
        for c in range(n_chunks):
            s = lax.dot_general(qh, kb_ref[c * tk:(c + 1) * tk, :], nt, preferred_element_type=F32)
            s_ref[u % 2, :, c * tk:(c + 1) * tk] = s
            for j in range(lanes):
                t = s[:, j * HEAD_DIM:(j + 1) * HEAD_DIM]
                mp = t if mp is None else jnp.maximum(mp, t)
            yield None
        sm = lax.dot_general(qh, kmb_ref[...], nt, preferred_element_type=F32)
        m = jnp.maximum(jnp.max(mp, axis=-1, keepdims=True), jnp.max(sm, axis=-1, keepdims=True))
        yield m, sm

    def outputs(u, m, sm):
        r0, h = units[u]
        mb = jnp.broadcast_to(m, (ru, HEAD_DIM))
        pm = jnp.exp2(sm - m)
        acc = jnp.dot(pm.astype(BF16), vme_ref[...], preferred_element_type=F32)
        for c in range(n_chunks):
            tiles = []
            for j in range(lanes):
                col = c * tk + j * HEAD_DIM
                tiles.append(jnp.exp2(s_ref[u % 2, :, col:col + HEAD_DIM] - mb).astype(BF16))
            acc = acc + jnp.dot(jnp.concatenate(tiles, axis=1), ve_ref[c * tk:(c + 1) * tk, :],
                                preferred_element_type=F32)
            yield None
        o = acc[:, :HEAD_DIM] / acc[:, HEAD_DIM:]
        o_ref[r0:r0 + ru, h * HEAD_DIM:(h + 1) * HEAD_DIM] = o.astype(o_ref.dtype)
        yield None

    prev_out = None
    for u in range(len(units) + 1):
        cur = scores(u) if u < len(units) else None
        stats = None
        for _ in range(n_chunks + 1):
            if cur is not None:
                stats = next(cur)
            if prev_out is not None:
                next(prev_out)
        prev_out = outputs(u, *stats) if cur is not None else None


def _attention(q, k, v, km, vm, qg, kg, cos, sin, prev, *, row_off, n_seq, n_tok, tq,
               cast_src=(), cast_prev=None, cast_rows=(), cast_row_off=()):
    t, q_dim = q.shape
    n_kv = k.shape[1] // HEAD_DIM
    n_meta = km.shape[0]
    gw = Q_PER_KV * HEAD_DIM
    assert row_off % n_tok == 0 and n_tok % tq == 0
    qmap = lambda b, kh, qi: ((row_off + b * n_tok) // tq + qi, kh)
    kmap = lambda b, kh, qi: (row_off // n_tok + b, kh)
    mmap = lambda b, kh, qi: (0, kh)
    const = lambda b, kh, qi: (0, 0)
    qpos = lambda b, kh, qi: (qi, 0)
    in_specs = [
        pl.BlockSpec((tq, gw), qmap),
        pl.BlockSpec((n_tok, HEAD_DIM), kmap),
        pl.BlockSpec((n_tok, HEAD_DIM), kmap),
        pl.BlockSpec((n_meta, HEAD_DIM), mmap),
        pl.BlockSpec((n_meta, HEAD_DIM), mmap),
        pl.BlockSpec((1, HEAD_DIM), const),
        pl.BlockSpec((1, HEAD_DIM), const),
        pl.BlockSpec((tq, HEAD_DIM), qpos),
        pl.BlockSpec((tq, HEAD_DIM), qpos),
        pl.BlockSpec((n_tok, HEAD_DIM), const),
        pl.BlockSpec((n_tok, HEAD_DIM), const),
    ]
    args = [q, k, v, km, vm, qg, kg, cos, sin, cos, sin]
    nq = n_tok // tq
    out_specs = [pl.BlockSpec((tq, gw), qmap)]
    out_shape = [jax.ShapeDtypeStruct((t, q_dim), BF16)]
    for w, rows, off in zip(cast_src, cast_rows, cast_row_off):
        assert off % rows == 0 and rows % SUBLANES == 0
        wmap = lambda b, kh, qi, rows=rows, off=off: (off // rows + (b * n_kv + kh) * nq + qi, 0)
        in_specs.append(pl.BlockSpec((rows, w.shape[1]), wmap))
        args.append(w)
        out_specs.append(pl.BlockSpec((rows, w.shape[1]), wmap))
        out_shape.append(jax.ShapeDtypeStruct(w.shape, BF16))
    aliases = {}
    if prev is not None:
        in_specs.append(pl.BlockSpec(memory_space=pl.ANY))
        args.append(prev)
        aliases[len(args) - 1] = 0
    for j, wprev in enumerate(cast_prev or ()):
        in_specs.append(pl.BlockSpec(memory_space=pl.ANY))
        args.append(wprev)
        aliases[len(args) - 1] = 1 + j
    return pl.pallas_call(
        functools.partial(_attn_body, n_cast=len(cast_src)),
        grid=(n_seq, n_kv, nq),
        in_specs=in_specs,
        out_specs=out_specs,
        out_shape=out_shape,
        scratch_shapes=[
            pltpu.VMEM((n_tok, HEAD_DIM), BF16),
            pltpu.VMEM((n_meta, HEAD_DIM), BF16),
            pltpu.VMEM((n_tok, 2 * HEAD_DIM), BF16),
            pltpu.VMEM((n_meta, 2 * HEAD_DIM), BF16),
            pltpu.VMEM((2, min(tq, ATTN_UNIT_ROWS), n_tok), F32),
        ],
        input_output_aliases=aliases,
        compiler_params=_params(3),
        name="attention",
    )(*args)


def _dftgen_body(tac_ref, tas_ref, tbc_ref, tbs_ref, c_ref, ns_ref):
    ac, asn = tac_ref[0], tas_ref[0]
    bc, bsn = tbc_ref[...], tbs_ref[...]
    c_ref[...] = (ac * bc - asn * bsn).astype(BF16)
    ns_ref[...] = (-(asn * bc + ac * bsn)).astype(BF16)


def _seq_dft(n_tok, n_meta):
    length = n_tok + n_meta
    rb = min(n_tok, HEAD_DIM)
    w = 2.0 * math.pi / length
    scale = 1.0 / math.sqrt(length)
    p_real = n_meta + jnp.arange(n_tok, dtype=I32)

    def angles(p_rows, p_cols):
        return ((p_rows[:, None] * p_cols[None, :]) % length).astype(F32) * w

    ang_a = angles(n_meta + rb * jnp.arange(n_tok // rb, dtype=I32), p_real)
    ang_b = angles(jnp.arange(rb, dtype=I32), p_real)
    ang_m = angles(p_real, jnp.arange(n_meta, dtype=I32))
    tac = (jnp.cos(ang_a) * scale).reshape(n_tok // rb, 1, n_tok)
    tas = (jnp.sin(ang_a) * scale).reshape(n_tok // rb, 1, n_tok)
    blk = pl.BlockSpec((1, 1, n_tok), lambda a: (a, 0, 0))
    full = pl.BlockSpec((rb, n_tok), lambda a: (0, 0))
    cmat, nsmat = pl.pallas_call(
        _dftgen_body,
        grid=(n_tok // rb,),
        in_specs=[blk, blk, full, full],
        out_specs=[pl.BlockSpec((rb, n_tok), lambda a: (a, 0))] * 2,
        out_shape=[jax.ShapeDtypeStruct((n_tok, n_tok), BF16)] * 2,
        compiler_params=_params(1),
        name="dft_matrices",
    )(tac, tas, jnp.cos(ang_b), jnp.sin(ang_b))
    return cmat, nsmat, (jnp.cos(ang_m) * scale).astype(BF16), (-jnp.sin(ang_m) * scale).astype(BF16)


def _chan_dft(cg):
    idx = jnp.arange(cg, dtype=I32)
    ang = ((idx[:, None] * idx[None, :]) % cg).astype(F32) * (2.0 * math.pi / cg)
    scale = 1.0 / math.sqrt(cg)
    return (jnp.cos(ang) * scale).astype(BF16), (jnp.sin(ang) * scale).astype(BF16)


def _fourier_body(c_ref, ns_ref, cm_ref, nsm_ref, u_ref, um_ref, cdc_ref, cds_ref, *rest):
    z_ref = rest[-1]
    u, um = u_ref[...], um_ref[...]
    a = jnp.dot(c_ref[...], u, preferred_element_type=F32) + jnp.dot(cm_ref[...], um, preferred_element_type=F32)
    b = jnp.dot(ns_ref[...], u, preferred_element_type=F32) + jnp.dot(nsm_ref[...], um, preferred_element_type=F32)
    a, b = a.astype(BF16), b.astype(BF16)
    cg = cdc_ref.shape[0]
    for g in range(z_ref.shape[1] // cg):
        sl = slice(g * cg, (g + 1) * cg)
        z = jnp.dot(a[:, sl], cdc_ref[...], preferred_element_type=F32)
        z = z + jnp.dot(b[:, sl], cds_ref[...], preferred_element_type=F32)
        z_ref[:, sl] = z.astype(z_ref.dtype)


def _fourier(uf, ufm, mats, cdc, cds, prev, *, row_off, n_seq, n_tok, tl, tf):
    cmat, nsmat, cmeta, nsmeta = mats
    t, f_dim = uf.shape
    n_meta = ufm.shape[0]
    cg = cdc.shape[0]
    assert row_off % n_tok == 0 and n_tok % tl == 0 and f_dim % tf == 0 and tf % cg == 0
    cmap = lambda b, j, i: (i, 0)
    const = lambda b, j, i: (0, 0)
    in_specs = [
        pl.BlockSpec((tl, n_tok), cmap),
        pl.BlockSpec((tl, n_tok), cmap),
        pl.BlockSpec((tl, n_meta), cmap),
        pl.BlockSpec((tl, n_meta), cmap),
        pl.BlockSpec((n_tok, tf), lambda b, j, i: (row_off // n_tok + b, j)),
        pl.BlockSpec((n_meta, tf), lambda b, j, i: (0, j)),
        pl.BlockSpec((cg, cg), const),
        pl.BlockSpec((cg, cg), const),
    ]
    args = [cmat, nsmat, cmeta, nsmeta, uf, ufm, cdc, cds]
    aliases = {}
    if prev is not None:
        in_specs.append(pl.BlockSpec(memory_space=pl.ANY))
        args.append(prev)
        aliases = {len(args) - 1: 0}
    return pl.pallas_call(
        _fourier_body,
        grid=(n_seq, f_dim // tf, n_tok // tl),
        in_specs=in_specs,
        out_specs=pl.BlockSpec((tl, tf), lambda b, j, i: ((row_off + b * n_tok) // tl + i, j)),
        out_shape=jax.ShapeDtypeStruct((t, f_dim), BF16),
        input_output_aliases=aliases,
        compiler_params=_params(3),
        name="fourier",
    )(*args)


def _merge_body(a_ref, z_ref, ga_ref, gf_ref, wa_ref, wf_ref, o_ref):
    a = jnp.dot(a_ref[...], wa_ref[...], preferred_element_type=F32)
    f = jnp.dot(z_ref[...], wf_ref[...], preferred_element_type=F32)
    sga = jax.nn.sigmoid(ga_ref[...].astype(F32))
    sgf = jax.nn.sigmoid(gf_ref[...].astype(F32))
    o_ref[...] = (sga * a + sgf * f).astype(o_ref.dtype)


def _merge(attn, z, g, wa, wf, *, tm):
    t, d = attn.shape
    f_dim = z.shape[1]
    return pl.pallas_call(
        _merge_body,
        grid=(t // tm,),
        in_specs=[
            pl.BlockSpec((tm, d), lambda i: (i, 0)),
            pl.BlockSpec((tm, f_dim), lambda i: (i, 0)),
            pl.BlockSpec((tm, d), lambda i: (i, 0)),
            pl.BlockSpec((tm, d), lambda i: (i, 1)),
            _resident((d, d)),
            _resident((f_dim, d)),
        ],
        out_specs=pl.BlockSpec((tm, d), lambda i: (i, 0)),
        out_shape=jax.ShapeDtypeStruct((t, d), BF16),
        compiler_params=_params(1),
        name="merge",
    )(attn, z, g, g, wa, wf)


def _route_body(m_ref, xa_ref, xb_ref, wo_ref, gain_ref, wr_ref, tri_ref,
                hmid_ref, hn_ref, eid_ref, cw_ref, rank_ref, cnt_ref, *, n_a_tiles, n_groups):
    i = pl.program_id(0)
    tm, d = xa_ref.shape
    dh = d // 2
    n_exp = cnt_ref.shape[0]

    @pl.when(i == 0)
    def _init():
        cnt_ref[...] = jnp.zeros_like(cnt_ref)

    x = jnp.where(i < n_a_tiles, xa_ref[...], xb_ref[...])
    h = x + jnp.dot(m_ref[...], wo_ref[...], preferred_element_type=F32)
    hmid_ref[...] = h
    hn = h * lax.rsqrt(jnp.mean(h * h, axis=-1, keepdims=True) + NORM_EPS) * gain_ref[...]
    hb = hn.astype(BF16)
    hn_ref[...] = _pack_bf16_pair(hn[:, :dh], hn[:, dh:])

    lt = lax.dot_general(wr_ref[...], hb, (((1,), (1,)), ((), ())), preferred_element_type=F32)
    row8 = lax.broadcasted_iota(I32, (EXPERTS_PER_GROUP, tm), 0)
    neg = jnp.float32(-jnp.inf)
    lg = jnp.where(row8 < n_groups, lt[0:8], neg)
    gmax = jnp.max(lg, axis=0, keepdims=True)
    gidx = jnp.min(jnp.where(lg == gmax, row8, 8), axis=0, keepdims=True)
    p_g = 1.0 / jnp.sum(jnp.exp(lg - gmax), axis=0, keepdims=True)
    sel = lt[8:8 + EXPERTS_PER_GROUP]
    for g in range(1, n_groups):
        sel = jnp.where(gidx == g, lt[8 + g * EXPERTS_PER_GROUP:8 + (g + 1) * EXPERTS_PER_GROUP], sel)
    m1 = jnp.max(sel, axis=0, keepdims=True)
    i1 = jnp.min(jnp.where(sel == m1, row8, 8), axis=0, keepdims=True)
    sel2 = jnp.where(row8 == i1, neg, sel)
    m2 = jnp.max(sel2, axis=0, keepdims=True)
    i2 = jnp.min(jnp.where(sel2 == m2, row8, 8), axis=0, keepdims=True)
    e21 = jnp.exp(m2 - m1)
    p1 = 1.0 / (1.0 + e21)
    p2 = e21 * p1
    e1 = gidx * EXPERTS_PER_GROUP + i1
    e2 = gidx * EXPERTS_PER_GROUP + i2
    eid_ref[0, 0:1, :] = e1
    eid_ref[0, 1:2, :] = e2
    cw_ref[0, 0:1, :] = p_g * p1
    cw_ref[0, 1:2, :] = p_g * p2

    row_e = lax.broadcasted_iota(I32, (n_exp, tm), 0)
    oh1 = (row_e == e1)
    oh2 = (row_e == e2)
    cs1 = jnp.dot(oh1.astype(BF16), tri_ref[...], preferred_element_type=F32)
    cs2 = jnp.dot(oh2.astype(BF16), tri_ref[...], preferred_element_type=F32)
    oh1f, oh2f = oh1.astype(F32), oh2.astype(F32)
    c1 = jnp.sum(oh1f, axis=1, keepdims=True)
    c2 = jnp.sum(oh2f, axis=1, keepdims=True)
    base = cnt_ref[:, 0:1]
    rank_ref[0, 0:1, :] = jnp.sum(oh1f * (base + cs1), axis=0, keepdims=True).astype(I32)
    rank_ref[0, 1:2, :] = jnp.sum(oh2f * (base + c1 + cs2), axis=0, keepdims=True).astype(I32)
    cnt_ref[...] = cnt_ref[...] + (c1 + c2)


def _outproj_route(merged, xa, xb, wout, gain, wr_t, tri, *, tm, n_groups, n_exp):
    t, d = merged.shape
    nt = t // tm
    na = xa.shape[0] // tm
    tok3 = lambda i: (i, 0, 0)
    body = functools.partial(_route_body, n_a_tiles=na, n_groups=n_groups)
    return pl.pallas_call(
        body,
        grid=(nt,),
        in_specs=[
            pl.BlockSpec((tm, d), lambda i: (i, 0)),
            pl.BlockSpec((tm, d), lambda i: (jnp.minimum(i, na - 1), 0)),
            pl.BlockSpec((tm, d), lambda i: (jnp.maximum(i - na, 0), 0)),
            _resident((d, d)),
            pl.BlockSpec((1, d), lambda i: (0, 0)),
            _resident(wr_t.shape),
            _resident((tm, tm)),
        ],
        out_specs=[
            pl.BlockSpec((tm, d), lambda i: (i, 0)),
            pl.BlockSpec((tm, d // 2), lambda i: (i, 0)),
            pl.BlockSpec((1, TOP_K, tm), tok3),
            pl.BlockSpec((1, TOP_K, tm), tok3),
            pl.BlockSpec((1, TOP_K, tm), tok3),
            pl.BlockSpec((n_exp, HEAD_DIM), lambda i: (0, 0)),
        ],
        out_shape=[
            jax.ShapeDtypeStruct((t, d), F32),
            jax.ShapeDtypeStruct((t, d // 2), U32),
            jax.ShapeDtypeStruct((nt, TOP_K, tm), I32),
            jax.ShapeDtypeStruct((nt, TOP_K, tm), F32),
            jax.ShapeDtypeStruct((nt, TOP_K, tm), I32),
            jax.ShapeDtypeStruct((n_exp, HEAD_DIM), F32),
        ],
        compiler_params=_params(1),
        name="outproj_route",
    )(merged, xa, xb, wout, gain, wr_t, tri)


def _issue_row_copies(idx_ref, n_rows, copy_for, per_trip=None):
    per = ROW_DMA_UNROLL

    def issue(a, carry):
        if per_trip is not None:
            per_trip(a)
        base = a * per
        for j in range(per):
            for k in range(TOP_K):
                copy_for(k, a * (per // SUBLANES) + j // SUBLANES, j % SUBLANES,
                         idx_ref[k * n_rows + base + j]).start()
        return carry

    lax.fori_loop(0, n_rows // per, issue, 0)


def _dispatch_body(pad_end_ref, dest_ref, x_ref, xs_ref, zbuf_ref, zsem, sem, *, bm, n_exp):
    i = pl.program_id(0)
    tm = x_ref.shape[0] * SUBLANES

    @pl.when(i == 0)
    def _zero_padding():
        zbuf_ref[...] = jnp.zeros_like(zbuf_ref)

        def last_block_copy(e):
            end = pad_end_ref[e]
            start = pad_end_ref[e - 1] if e > 0 else 0
            cp = pltpu.make_async_copy(zbuf_ref, xs_ref.at[pl.ds(pl.multiple_of(end - bm, bm), bm)], zsem)
            return end > start, cp

        for e in range(n_exp):
            nonempty, cp = last_block_copy(e)
            pl.when(nonempty)(cp.start)
        for e in range(n_exp):
            nonempty, cp = last_block_copy(e)
            pl.when(nonempty)(cp.wait)

    def copy_for(k, g, s, d):
        return pltpu.make_async_copy(x_ref.at[g, pl.ds(s, 1), :], xs_ref.at[pl.ds(d, 1), :], sem)

    _issue_row_copies(dest_ref, tm, copy_for)
    for k in range(TOP_K):
        pltpu.make_async_copy(x_ref, x_ref, sem).wait()


def _dispatch(hn2p, dest, pad_end, *, tm, bm, n_rows):
    t, dh = hn2p.shape
    n_exp = pad_end.shape[0]
    body = functools.partial(_dispatch_body, bm=bm, n_exp=n_exp)
    return pl.pallas_call(
        body,
        grid_spec=pltpu.PrefetchScalarGridSpec(
            num_scalar_prefetch=1,
            grid=(t // tm,),
            in_specs=[
                pl.BlockSpec((TOP_K * tm,), lambda i, pe: (i,), memory_space=pltpu.SMEM),
                pl.BlockSpec((tm // SUBLANES, SUBLANES, dh), lambda i, pe: (i, 0, 0)),
            ],
            out_specs=pl.BlockSpec(memory_space=pl.ANY),
            scratch_shapes=[pltpu.VMEM((bm, dh), U32), pltpu.SemaphoreType.DMA(()), pltpu.SemaphoreType.DMA(())],
        ),
        out_shape=jax.ShapeDtypeStruct((n_rows, dh), U32),
        compiler_params=_params(1),
        name="dispatch",
    )(pad_end, dest.reshape(-1), hn2p.reshape(t // SUBLANES, SUBLANES, dh))


def _ffn_body(bexp_ref, nvalid_ref, xs_ref, wg_ref, wu_ref, wd_ref, ys_ref):
    b = pl.program_id(0)

    @pl.when(b < nvalid_ref[0])
    def _():
        dh = xs_ref.shape[1]
        lo, hi = _unpack_bf16_pair(xs_ref[...])
        lo, hi = lo.astype(BF16), hi.astype(BF16)
        g = jnp.dot(lo, wg_ref[0, :dh, :], preferred_element_type=F32)
        g = g + jnp.dot(hi, wg_ref[0, dh:, :], preferred_element_type=F32)
        u = jnp.dot(lo, wu_ref[0, :dh, :], preferred_element_type=F32)
        u = u + jnp.dot(hi, wu_ref[0, dh:, :], preferred_element_type=F32)
        hid = (jax.nn.silu(g) * u).astype(BF16)
        y = jnp.dot(hid, wd_ref[0], preferred_element_type=F32)
        ys_ref[...] = _pack_bf16_pair(y[:, :dh], y[:, dh:])


def _expert_ffn(xs, wg, wu, wd, block_expert, n_valid, *, bm):
    n_rows, dh = xs.shape
    n_exp, d, de = wg.shape
    nb = n_rows // bm
    blk = lambda b, be, nv: (jnp.minimum(b, nv[0] - 1), 0)
    wmap = lambda b, be, nv: (be[jnp.minimum(b, nv[0] - 1)], 0, 0)
    return pl.pallas_call(
        _ffn_body,
        grid_spec=pltpu.PrefetchScalarGridSpec(
            num_scalar_prefetch=2,
            grid=(nb,),
            in_specs=[
                pl.BlockSpec((bm, dh), blk),
                pl.BlockSpec((1, d, de), wmap),
                pl.BlockSpec((1, d, de), wmap),
                pl.BlockSpec((1, de, d), wmap),
            ],
            out_specs=pl.BlockSpec((bm, dh), blk),
        ),
        out_shape=jax.ShapeDtypeStruct((n_rows, dh), U32),
        compiler_params=_params(1),
        name="expert_ffn",
    )(block_expert, n_valid, xs, wg, wu, wd)


def _combine_body(dest_ref, dest_next_ref, h_ref, cw_ref, gain_ref, ys_ref, o_ref, buf_ref, sem):
    i = pl.program_id(0)
    last = pl.num_programs(0) - 1
    tm, d = h_ref.shape
    dh = d // 2
    slot = i % 2
    per = ROW_DMA_UNROLL

    def gather_into(dst_slot):
        def copy_for(k, g, s, row):
            return pltpu.make_async_copy(ys_ref.at[pl.ds(row, 1), :], buf_ref.at[dst_slot, k, g, pl.ds(s, 1), :],
                                         sem.at[dst_slot])
        return copy_for

    @pl.when(i == 0)
    def _first_tile():
        _issue_row_copies(dest_ref, tm, gather_into(0))

    for k in range(TOP_K):
        pltpu.make_async_copy(buf_ref.at[slot, k], buf_ref.at[slot, k], sem.at[slot]).wait()

    def finish_rows(a):
        for g in range(per // SUBLANES):
            grp = a * (per // SUBLANES) + g
            rows = pl.ds(pl.multiple_of(grp * SUBLANES, SUBLANES), SUBLANES)
            lo1, hi1 = _unpack_bf16_pair(buf_ref[slot, 0, grp])
            lo2, hi2 = _unpack_bf16_pair(buf_ref[slot, 1, grp])
            w1 = cw_ref[rows, 0:1]
            w2 = cw_ref[rows, 1:2]
            o_lo = h_ref[rows, :dh] + (w1 * lo1 + w2 * lo2)
            o_hi = h_ref[rows, dh:] + (w1 * hi1 + w2 * hi2)
            ssq = jnp.sum(o_lo * o_lo, axis=-1, keepdims=True) + jnp.sum(o_hi * o_hi, axis=-1, keepdims=True)
            r = lax.rsqrt(ssq / d + NORM_EPS)
            o_ref[rows, :dh] = o_lo * r * gain_ref[:, :dh]
            o_ref[rows, dh:] = o_hi * r * gain_ref[:, dh:]

    @pl.when(i < last)
    def _overlapped():
        _issue_row_copies(dest_next_ref, tm, gather_into(1 - slot), per_trip=finish_rows)

    @pl.when(i == last)
    def _tail():
        def trip(a, carry):
            finish_rows(a)
            return carry
        lax.fori_loop(0, tm // per, trip, 0)


def _combine(ys, dest, hmid, cw_t, gain, *, tm, row_off, n_rows):
    d = hmid.shape[1]
    dh = d // 2
    off, nt = row_off // tm, n_rows // tm
    dest_flat = dest.reshape(-1)
    return pl.pallas_call(
        _combine_body,
        grid=(nt,),
        in_specs=[
            pl.BlockSpec((TOP_K * tm,), lambda i: (off + i,), memory_space=pltpu.SMEM),
            pl.BlockSpec((TOP_K * tm,), lambda i: (off + jnp.minimum(i + 1, nt - 1),), memory_space=pltpu.SMEM),
            pl.BlockSpec((tm, d), lambda i: (off + i, 0)),
            pl.BlockSpec((tm, TOP_K), lambda i: (off + i, 0)),
            pl.BlockSpec((1, d), lambda i: (0, 0)),
            pl.BlockSpec(memory_space=pl.ANY),
        ],
        out_specs=pl.BlockSpec((tm, d), lambda i: (i, 0)),
        out_shape=jax.ShapeDtypeStruct((n_rows, d), F32),
        scratch_shapes=[pltpu.VMEM((2, TOP_K, tm // SUBLANES, SUBLANES, dh), U32), pltpu.SemaphoreType.DMA((2,))],
        compiler_params=_params(1),
        name="combine",
    )(dest_flat, dest_flat, hmid, cw_t, gain, ys)


def _rope_tables(n_tokens):
    rows = n_tokens // GRID_W
    row = jnp.repeat(jnp.arange(rows, dtype=F32), GRID_W)
    col = jnp.tile(jnp.arange(GRID_W, dtype=F32), rows)
    inv_freq = ROPE_THETA ** (-jnp.arange(ROPE_HALF, dtype=F32) / ROPE_HALF)
    ar, ac = row[:, None] * inv_freq, col[:, None] * inv_freq
    cos = jnp.concatenate([jnp.cos(ar), jnp.cos(ac), jnp.cos(ar), jnp.cos(ac)], axis=1)
    sin = jnp.concatenate([-jnp.sin(ar), -jnp.sin(ac), jnp.sin(ar), jnp.sin(ac)], axis=1)
    return cos, sin


def _tile(n, pref):
    return min(n, pref)


def kernel(x_prompt, x_sample, meta_tokens, mix_norm, w_in, q_gain, k_gain, w_attn_o, w_fourier_o, w_out,
           moe_norm, w_router_group, w_router_expert, w_expert_gate, w_expert_up, w_expert_down, final_norm):
    bp, n_p, d = x_prompt.shape
    bs, n_s, _ = x_sample.shape
    n_meta = meta_tokens.shape[0]
    t_p, t_s = bp * n_p, bs * n_s
    t = t_p + t_s
    f_dim = d // 2
    cg = f_dim // N_FOURIER_GROUPS
    n_groups = w_router_group.shape[-1]
    n_exp = w_router_expert.shape[-1]
    assert n_exp == n_groups * EXPERTS_PER_GROUP and n_groups <= 8

    tm = _tile(math.gcd(n_p, n_s), 512)
    bm = _tile(t, 512)

    q_lo, k_lo, v_lo = f_dim, f_dim + d, f_dim + d + d // 4
    w0 = w_in[0]
    w_in_b = jnp.concatenate([w0[:, :q_lo], _pair_major(w0[:, q_lo:k_lo]), _pair_major(w0[:, k_lo:v_lo]),
                              w0[:, v_lo:]], axis=1).astype(BF16)
    wa_b, wf_b, wo_b = w_attn_o[0].astype(BF16), w_fourier_o[0].astype(BF16), w_out[0].astype(BF16)
    qg = (_pair_major(q_gain[0]) * (HEAD_DIM ** -0.5 * LOG2E)).reshape(1, HEAD_DIM)
    kg = _pair_major(k_gain[0]).reshape(1, HEAD_DIM)
    wr_t = jnp.zeros((8 + n_exp, d), F32)
    wr_t = wr_t.at[:n_groups].set(w_router_group[0].T).at[8:].set(w_router_expert[0].T).astype(BF16)
    tri = (jnp.arange(tm)[:, None] < jnp.arange(tm)[None, :]).astype(BF16)
    cos, sin = _rope_tables(max(n_p, n_s))
    cdc, cds = _chan_dft(cg)
    gain_in = mix_norm[0].reshape(1, d)

    xp, xs_in = x_prompt.reshape(t_p, d), x_sample.reshape(t_s, d)
    tm_in = _tile(math.gcd(t_p, t_s), INPROJ_ROWS)
    proj = _inproj(xp, gain_in, w_in_b, None, tm=tm_in, row_off=0, t_total=t)
    uf, q, k, v, g = _inproj(xs_in, gain_in, w_in_b, proj, tm=tm_in, row_off=t_p, t_total=t)
    ufm, _, km, vm, _ = _inproj(meta_tokens, gain_in, w_in_b, None, tm=n_meta, row_off=0, t_total=n_meta)

    kinds = ((0, bp, n_p), (t_p, bs, n_s))
    n_kv = k.shape[1] // HEAD_DIM
    steps = [n_seq * n_kv * (n_tok // _tile(n_tok, ATTN_Q_TILE)) for _, n_seq, n_tok in kinds]
    w_exp = [w_expert_gate[0], w_expert_up[0], w_expert_down[0]]
    w_flat = [w.reshape(-1, w.shape[-1]) for w in w_exp]
    fused_cast = all(w.shape[0] % (SUBLANES * sum(steps)) == 0 for w in w_flat)
    cast_rows = [w.shape[0] // sum(steps) for w in w_flat] if fused_cast else []

    attn, z, w_b = None, None, None
    for (row_off, n_seq, n_tok), first_step in zip(kinds, (0, steps[0])):
        res = _attention(q, k, v, km, vm, qg, kg, cos, sin, attn, row_off=row_off, n_seq=n_seq,
                         n_tok=n_tok, tq=_tile(n_tok, ATTN_Q_TILE),
                         cast_src=w_flat if fused_cast else (), cast_prev=w_b, cast_rows=cast_rows,
                         cast_row_off=[first_step * r for r in cast_rows])
        attn, w_b = res[0], (res[1:] if fused_cast else None)
        z = _fourier(uf, ufm, _seq_dft(n_tok, n_meta), cdc, cds, z, row_off=row_off, n_seq=n_seq,
                     n_tok=n_tok, tl=_tile(n_tok, max(512, FOURIER_TILE_ELEMS // n_tok)), tf=_tile(f_dim, 512))
    if fused_cast:
        wg_b, wu_b, wd_b = [wb.reshape(w.shape) for wb, w in zip(w_b, w_exp)]
    else:
        wg_b, wu_b, wd_b = [w.astype(BF16) for w in w_exp]

    merged = _merge(attn, z, g, wa_b, wf_b, tm=tm)
    hmid, hn2p, eid, cw, rank, counts = _outproj_route(
        merged, xp, xs_in, wo_b, moe_norm[0].reshape(1, d), wr_t, tri, tm=tm, n_groups=n_groups, n_exp=n_exp)

    cnt = counts[:, 0].astype(I32)
    padded = (cnt + bm - 1) // bm * bm
    pad_end = jnp.cumsum(padded).astype(I32)
    pad_start = pad_end - padded
    onehot = eid[..., None] == jnp.arange(n_exp, dtype=I32)
    dest = jnp.sum(jnp.where(onehot, pad_start, 0), axis=-1) + rank
    nb = (t * TOP_K) // bm + n_exp
    n_valid = (pad_end[-1:] // bm).astype(I32)
    blk_start = jnp.arange(nb, dtype=I32) * bm
    block_expert = jnp.minimum(jnp.sum(pad_end[None, :] <= blk_start[:, None], axis=1), n_exp - 1).astype(I32)

    xs = _dispatch(hn2p, dest, pad_end, tm=tm, bm=bm, n_rows=nb * bm)
    ys = _expert_ffn(xs, wg_b, wu_b, wd_b, block_expert, n_valid, bm=bm)
    cw_t = jnp.transpose(cw, (0, 2, 1)).reshape(t, TOP_K)
    fgain = final_norm.reshape(1, d)
    y_p = _combine(ys, dest, hmid, cw_t, fgain, tm=tm, row_off=0, n_rows=t_p)
    y_s = _combine(ys, dest, hmid, cw_t, fgain, tm=tm, row_off=t_p, n_rows=t_s)
    return y_p.reshape(bp, n_p, d), y_s.reshape(bs, n_s, d)
```

```python
import functools
import math

import jax
import jax.numpy as jnp
from jax import lax
from jax.experimental import pallas as pl
from jax.experimental.pallas import tpu as pltpu

F32, BF16, I32, U32 = jnp.float32, jnp.bfloat16, jnp.int32, jnp.uint32

GRID_W = 64
HEAD_DIM = 128
ROPE_HALF = 32
Q_PER_KV = 4
N_FOURIER_GROUPS = 4
ROPE_THETA = 10000.0
NORM_EPS = 1e-6
TOP_K = 2
EXPERTS_PER_GROUP = 8
LOG2E = 1.4426950408889634
ATTN_KV_CHUNK = 512
ATTN_UNIT_ROWS = 256
ATTN_Q_TILE = 512
INPROJ_ROWS = 1024
FOURIER_TILE_ELEMS = 2 * 1024 * 1024
SUBLANES = 8
ROW_DMA_UNROLL = 32
DISPATCH_RING = 3

VMEM_LIMIT_BYTES = 56 * 1024 * 1024


def _params(n_axes):
    return pltpu.CompilerParams(dimension_semantics=("arbitrary",) * n_axes,
                                vmem_limit_bytes=VMEM_LIMIT_BYTES)


def _resident(shape):
    return pl.BlockSpec(shape, lambda *_: (0,) * len(shape), pipeline_mode=pl.Buffered(1))


def _pack_bf16_pair(lo, hi):
    lo_bits = lax.bitcast_convert_type(lo.astype(BF16).astype(F32), U32) >> 16
    hi_bits = lax.bitcast_convert_type(hi.astype(BF16).astype(F32), U32) & jnp.uint32(0xFFFF0000)
    return lo_bits | hi_bits


def _unpack_bf16_pair(w):
    lo = lax.bitcast_convert_type(w << 16, F32)
    hi = lax.bitcast_convert_type(w & jnp.uint32(0xFFFF0000), F32)
    return lo, hi


_F_TILES, _Q_TILES, _G_TILES = 2, 4, 8
_N_TILES = _F_TILES + _Q_TILES + 2 + _G_TILES


def _inproj_body(x_ref, gain_ref, w_ref, *rest):
    uf_ref, q_ref, k_ref, v_ref, g_ref, hn_ref, xbuf_ref, xsem = rest[-8:]
    i, n = pl.program_id(0), pl.program_id(1)
    tm = hn_ref.shape[0]
    slot = i % 2

    def x_copy(tile, dst_slot):
        rows = pl.ds(pl.multiple_of(tile * tm, tm), tm)
        return pltpu.make_async_copy(x_ref.at[rows], xbuf_ref.at[dst_slot], xsem.at[dst_slot])

    @pl.when(n == 0)
    def _norm():
        pl.when(i == 0)(x_copy(0, 0).start)
        x_copy(i, slot).wait()
        pl.when(i + 1 < pl.num_programs(0))(x_copy(i + 1, 1 - slot).start)
        x = xbuf_ref[slot]
        y = x * lax.rsqrt(jnp.mean(x * x, axis=-1, keepdims=True) + NORM_EPS)
        hn_ref[...] = (y * gain_ref[...]).astype(BF16)

    q0 = _F_TILES
    k0 = q0 + _Q_TILES

    def project_into(out_ref):
        def branch():
            acc = jnp.dot(hn_ref[...], w_ref[...], preferred_element_type=F32)
            out_ref[...] = acc.astype(out_ref.dtype)
        return branch

    pl.when(n < q0)(project_into(uf_ref))
    pl.when((n >= q0) & (n < k0))(project_into(q_ref))
    pl.when(n == k0)(project_into(k_ref))
    pl.when(n == k0 + 1)(project_into(v_ref))
    pl.when(n > k0 + 1)(project_into(g_ref))


def _inproj(x, gain, w_in, prev, *, tm, row_off, t_total):
    tx, d = x.shape
    tn = d // 4
    f_dim, q_dim, kv_dim = d // 2, d, d // 4
    assert w_in.shape == (d, _N_TILES * tn) and tx % tm == 0 and row_off % tm == 0
    ob = row_off // tm
    q0, k0, g0 = _F_TILES, _F_TILES + _Q_TILES, _F_TILES + _Q_TILES + 2
    in_specs = [
        pl.BlockSpec(memory_space=pl.ANY),
        pl.BlockSpec((1, d), lambda i, n: (0, 0)),
        pl.BlockSpec((d, tn), lambda i, n: (0, n)),
    ]
    args = [x, gain, w_in]
    aliases = {}
    if prev is not None:
        in_specs += [pl.BlockSpec(memory_space=pl.ANY)] * len(prev)
        aliases = {len(args) + j: j for j in range(len(prev))}
        args += list(prev)
    return pl.pallas_call(
        _inproj_body,
        grid=(tx // tm, _N_TILES),
        in_specs=in_specs,
        out_specs=[
            pl.BlockSpec((tm, tn), lambda i, n: (ob + i, jnp.clip(n, 0, _F_TILES - 1))),
            pl.BlockSpec((tm, tn), lambda i, n: (ob + i, jnp.clip(n - q0, 0, _Q_TILES - 1))),
            pl.BlockSpec((tm, tn), lambda i, n: (ob + i, 0)),
            pl.BlockSpec((tm, tn), lambda i, n: (ob + i, 0)),
            pl.BlockSpec((tm, tn), lambda i, n: (ob + i, jnp.clip(n - g0, 0, _G_TILES - 1))),
        ],
        out_shape=[
            jax.ShapeDtypeStruct((t_total, f_dim), BF16),
            jax.ShapeDtypeStruct((t_total, q_dim), F32),
            jax.ShapeDtypeStruct((t_total, kv_dim), F32),
            jax.ShapeDtypeStruct((t_total, kv_dim), BF16),
            jax.ShapeDtypeStruct((t_total, 2 * d), BF16),
        ],
        scratch_shapes=[pltpu.VMEM((tm, d), BF16), pltpu.VMEM((2, tm, d), F32), pltpu.SemaphoreType.DMA((2,))],
        input_output_aliases=aliases,
        compiler_params=_params(2),
        name="inproj",
    )(*args)


def _head_norm(a, gain):
    return a * lax.rsqrt(jnp.mean(a * a, axis=-1, keepdims=True) + NORM_EPS) * gain


def _pair_major(a):
    lead = a.shape[:-1]
    a = a.reshape(lead + (a.shape[-1] // HEAD_DIM, 2, 2, ROPE_HALF))
    return jnp.swapaxes(a, -3, -2).reshape(lead + (-1,))


def _rope(y, cos, sin):
    return y * cos + pltpu.roll(y, HEAD_DIM // 2, 1) * sin


def _attn_body(q_ref, k_ref, v_ref, km_ref, vm_ref, qg_ref, kg_ref, cq_ref, sq_ref, ck_ref, sk_ref, *rest,
               n_cast):
    cast_in = rest[:n_cast]
    kb_ref, kmb_ref, ve_ref, vme_ref, s_ref = rest[-5:]
    o_ref = rest[-6 - n_cast]
    cast_out = rest[len(rest) - 5 - n_cast:len(rest) - 5]
    tq, n_tok = q_ref.shape[0], k_ref.shape[0]

    for w_ref, wb_ref in zip(cast_in, cast_out):
        wb_ref[...] = w_ref[...].astype(BF16)

    tk = min(n_tok, ATTN_KV_CHUNK)
    n_chunks, lanes = n_tok // tk, tk // HEAD_DIM
    nt = (((1,), (1,)), ((), ()))

    @pl.when(pl.program_id(2) == 0)
    def _prepare_keys():
        kb_ref[...] = _rope(_head_norm(k_ref[...], kg_ref[...]), ck_ref[...], sk_ref[...]).astype(BF16)
        kmb_ref[...] = _head_norm(km_ref[...], kg_ref[...]).astype(BF16)
        ve_ref[:, :HEAD_DIM] = v_ref[...]
        ve_ref[:, HEAD_DIM:] = jnp.ones((n_tok, HEAD_DIM), BF16)
        vme_ref[:, :HEAD_DIM] = vm_ref[...]
        vme_ref[:, HEAD_DIM:] = jnp.ones((vm_ref.shape[0], HEAD_DIM), BF16)

    qg = qg_ref[...]
    ru = min(tq, ATTN_UNIT_ROWS)
    units = [(r0, h) for r0 in range(0, tq, ru) for h in range(Q_PER_KV)]

    def scores(u):
        r0, h = units[u]
        rows = slice(r0, r0 + ru)
        qh = _rope(_head_norm(q_ref[rows, h * HEAD_DIM:(h + 1) * HEAD_DIM], qg), cq_ref[rows, :], sq_ref[rows, :])
        qh = qh.astype(BF16)
        mp = None
        for c in range(n_chunks):
            s = lax.dot_general(qh, kb_ref[c * tk:(c + 1) * tk, :], nt, preferred_element_type=F32)
            s_ref[u % 2, :, c * tk:(c + 1) * tk] = s
            for j in range(lanes):
                t = s[:, j * HEAD_DIM:(j + 1) * HEAD_DIM]
                mp = t if mp is None else jnp.maximum(mp, t)
            yield None
        sm = lax.dot_general(qh, kmb_ref[...], nt, preferred_element_type=F32)
        m = jnp.maximum(jnp.max(mp, axis=-1, keepdims=True), jnp.max(sm, axis=-1, keepdims=True))
        yield m, sm

    def outputs(u, m, sm):
        r0, h = units[u]
        mb = jnp.broadcast_to(m, (ru, HEAD_DIM))
        pm = jnp.exp2(sm - m)
        acc = jnp.dot(pm.astype(BF16), vme_ref[...], preferred_element_type=F32)
        for c in range(n_chunks):
            tiles = []
            for j in range(lanes):
                col = c * tk + j * HEAD_DIM
                tiles.append(jnp.exp2(s_ref[u % 2, :, col:col + HEAD_DIM] - mb).astype(BF16))
            acc = acc + jnp.dot(jnp.concatenate(tiles, axis=1), ve_ref[c * tk:(c + 1) * tk, :],
                                preferred_element_type=F32)
            yield None
        o = acc[:, :HEAD_DIM] / acc[:, HEAD_DIM:]
        o_ref[r0:r0 + ru, h * HEAD_DIM:(h + 1) * HEAD_DIM] = o.astype(o_ref.dtype)
        yield None

    prev_out = None
    for u in range(len(units) + 1):
        cur = scores(u) if u < len(units) else None
        stats = None
        for _ in range(n_chunks + 1):
            if cur is not None:
                stats = next(cur)
            if prev_out is not None:
                next(prev_out)
        prev_out = outputs(u, *stats) if cur is not None else None


def _attention(q, k, v, km, vm, qg, kg, cos, sin, prev, *, row_off, n_seq, n_tok, tq,
               cast_src=(), cast_prev=None, cast_rows=(), cast_row_off=()):
    t, q_dim = q.shape
    n_kv = k.shape[1] // HEAD_DIM
    n_meta = km.shape[0]
    gw = Q_PER_KV * HEAD_DIM
    assert row_off % n_tok == 0 and n_tok % tq == 0
    qmap = lambda b, kh, qi: ((row_off + b * n_tok) // tq + qi, kh)
    kmap = lambda b, kh, qi: (row_off // n_tok + b, kh)
    mmap = lambda b, kh, qi: (0, kh)
    const = lambda b, kh, qi: (0, 0)
    qpos = lambda b, kh, qi: (qi, 0)
    in_specs = [
        pl.BlockSpec((tq, gw), qmap),
        pl.BlockSpec((n_tok, HEAD_DIM), kmap),
        pl.BlockSpec((n_tok, HEAD_DIM), kmap),
        pl.BlockSpec((n_meta, HEAD_DIM), mmap),
        pl.BlockSpec((n_meta, HEAD_DIM), mmap),
        pl.BlockSpec((1, HEAD_DIM), const),
        pl.BlockSpec((1, HEAD_DIM), const),
        pl.BlockSpec((tq, HEAD_DIM), qpos),
        pl.BlockSpec((tq, HEAD_DIM), qpos),
        pl.BlockSpec((n_tok, HEAD_DIM), const),
        pl.BlockSpec((n_tok, HEAD_DIM), const),
    ]
    args = [q, k, v, km, vm, qg, kg, cos, sin, cos, sin]
    nq = n_tok // tq
    out_specs = [pl.BlockSpec((tq, gw), qmap)]
    out_shape = [jax.ShapeDtypeStruct((t, q_dim), BF16)]
    for w, rows, off in zip(cast_src, cast_rows, cast_row_off):
        assert off % rows == 0 and rows % SUBLANES == 0
        wmap = lambda b, kh, qi, rows=rows, off=off: (off // rows + (b * n_kv + kh) * nq + qi, 0)
        in_specs.append(pl.BlockSpec((rows, w.shape[1]), wmap))
        args.append(w)
        out_specs.append(pl.BlockSpec((rows, w.shape[1]), wmap))
        out_shape.append(jax.ShapeDtypeStruct(w.shape, BF16))
    aliases = {}
    if prev is not None:
        in_specs.append(pl.BlockSpec(memory_space=pl.ANY))
        args.append(prev)
        aliases[len(args) - 1] = 0
    for j, wprev in enumerate(cast_prev or ()):
        in_specs.append(pl.BlockSpec(memory_space=pl.ANY))
        args.append(wprev)
        aliases[len(args) - 1] = 1 + j
    return pl.pallas_call(
        functools.partial(_attn_body, n_cast=len(cast_src)),
        grid=(n_seq, n_kv, nq),
        in_specs=in_specs,
        out_specs=out_specs,
        out_shape=out_shape,
        scratch_shapes=[
            pltpu.VMEM((n_tok, HEAD_DIM), BF16),
            pltpu.VMEM((n_meta, HEAD_DIM), BF16),
            pltpu.VMEM((n_tok, 2 * HEAD_DIM), BF16),
            pltpu.VMEM((n_meta, 2 * HEAD_DIM), BF16),
            pltpu.VMEM((2, min(tq, ATTN_UNIT_ROWS), n_tok), F32),
        ],
        input_output_aliases=aliases,
        compiler_params=_params(3),
        name="attention",
    )(*args)


def _dftgen_body(tac_ref, tas_ref, tbc_ref, tbs_ref, c_ref, ns_ref):
    ac, asn = tac_ref[0], tas_ref[0]
    bc, bsn = tbc_ref[...], tbs_ref[...]
    c_ref[...] = (ac * bc - asn * bsn).astype(BF16)
    ns_ref[...] = (-(asn * bc + ac * bsn)).astype(BF16)


def _seq_dft(n_tok, n_meta):
    length = n_tok + n_meta
    rb = min(n_tok, HEAD_DIM)
    w = 2.0 * math.pi / length
    scale = 1.0 / math.sqrt(length)
    p_real = n_meta + jnp.arange(n_tok, dtype=I32)

    def angles(p_rows, p_cols):
        return ((p_rows[:, None] * p_cols[None, :]) % length).astype(F32) * w

    ang_a = angles(n_meta + rb * jnp.arange(n_tok // rb, dtype=I32), p_real)
    ang_b = angles(jnp.arange(rb, dtype=I32), p_real)
    ang_m = angles(p_real, jnp.arange(n_meta, dtype=I32))
    tac = (jnp.cos(ang_a) * scale).reshape(n_tok // rb, 1, n_tok)
    tas = (jnp.sin(ang_a) * scale).reshape(n_tok // rb, 1, n_tok)
    blk = pl.BlockSpec((1, 1, n_tok), lambda a: (a, 0, 0))
    full = pl.BlockSpec((rb, n_tok), lambda a: (0, 0))
    cmat, nsmat = pl.pallas_call(
        _dftgen_body,
        grid=(n_tok // rb,),
        in_specs=[blk, blk, full, full],
        out_specs=[pl.BlockSpec((rb, n_tok), lambda a: (a, 0))] * 2,
        out_shape=[jax.ShapeDtypeStruct((n_tok, n_tok), BF16)] * 2,
        compiler_params=_params(1),
        name="dft_matrices",
    )(tac, tas, jnp.cos(ang_b), jnp.sin(ang_b))
    return cmat, nsmat, (jnp.cos(ang_m) * scale).astype(BF16), (-jnp.sin(ang_m) * scale).astype(BF16)


def _chan_dft(cg):
    idx = jnp.arange(cg, dtype=I32)
    ang = ((idx[:, None] * idx[None, :]) % cg).astype(F32) * (2.0 * math.pi / cg)
    scale = 1.0 / math.sqrt(cg)
    return (jnp.cos(ang) * scale).astype(BF16), (jnp.sin(ang) * scale).astype(BF16)


def _fourier_body(c_ref, ns_ref, cm_ref, nsm_ref, u_ref, um_ref, cdc_ref, cds_ref, *rest):
    z_ref = rest[-1]
    u, um = u_ref[...], um_ref[...]
    a = jnp.dot(c_ref[...], u, preferred_element_type=F32) + jnp.dot(cm_ref[...], um, preferred_element_type=F32)
    b = jnp.dot(ns_ref[...], u, preferred_element_type=F32) + jnp.dot(nsm_ref[...], um, preferred_element_type=F32)
    a, b = a.astype(BF16), b.astype(BF16)
    cg = cdc_ref.shape[0]
    for g in range(z_ref.shape[1] // cg):
        sl = slice(g * cg, (g + 1) * cg)
        z = jnp.dot(a[:, sl], cdc_ref[...], preferred_element_type=F32)
        z = z + jnp.dot(b[:, sl], cds_ref[...], preferred_element_type=F32)
        z_ref[:, sl] = z.astype(z_ref.dtype)


def _fourier(uf, ufm, mats, cdc, cds, prev, *, row_off, n_seq, n_tok, tl, tf):
    cmat, nsmat, cmeta, nsmeta = mats
    t, f_dim = uf.shape
    n_meta = ufm.shape[0]
    cg = cdc.shape[0]
    assert row_off % n_tok == 0 and n_tok % tl == 0 and f_dim % tf == 0 and tf % cg == 0
    cmap = lambda b, j, i: (i, 0)
    const = lambda b, j, i: (0, 0)
    in_specs = [
        pl.BlockSpec((tl, n_tok), cmap),
        pl.BlockSpec((tl, n_tok), cmap),
        pl.BlockSpec((tl, n_meta), cmap),
        pl.BlockSpec((tl, n_meta), cmap),
        pl.BlockSpec((n_tok, tf), lambda b, j, i: (row_off // n_tok + b, j)),
        pl.BlockSpec((n_meta, tf), lambda b, j, i: (0, j)),
        pl.BlockSpec((cg, cg), const),
        pl.BlockSpec((cg, cg), const),
    ]
    args = [cmat, nsmat, cmeta, nsmeta, uf, ufm, cdc, cds]
    aliases = {}
    if prev is not None:
        in_specs.append(pl.BlockSpec(memory_space=pl.ANY))
        args.append(prev)
        aliases = {len(args) - 1: 0}
    return pl.pallas_call(
        _fourier_body,
        grid=(n_seq, f_dim // tf, n_tok // tl),
        in_specs=in_specs,
        out_specs=pl.BlockSpec((tl, tf), lambda b, j, i: ((row_off + b * n_tok) // tl + i, j)),
        out_shape=jax.ShapeDtypeStruct((t, f_dim), BF16),
        input_output_aliases=aliases,
        compiler_params=_params(3),
        name="fourier",
    )(*args)


def _merge_body(a_ref, z_ref, ga_ref, gf_ref, wa_ref, wf_ref, o_ref):
    a = jnp.dot(a_ref[...], wa_ref[...], preferred_element_type=F32)
    f = jnp.dot(z_ref[...], wf_ref[...], preferred_element_type=F32)
    sga = jax.nn.sigmoid(ga_ref[...].astype(F32))
    sgf = jax.nn.sigmoid(gf_ref[...].astype(F32))
    o_ref[...] = (sga * a + sgf * f).astype(o_ref.dtype)


def _merge(attn, z, g, wa, wf, *, tm):
    t, d = attn.shape
    f_dim = z.shape[1]
    return pl.pallas_call(
        _merge_body,
        grid=(t // tm,),
        in_specs=[
            pl.BlockSpec((tm, d), lambda i: (i, 0)),
            pl.BlockSpec((tm, f_dim), lambda i: (i, 0)),
            pl.BlockSpec((tm, d), lambda i: (i, 0)),
            pl.BlockSpec((tm, d), lambda i: (i, 1)),
            _resident((d, d)),
            _resident((f_dim, d)),
        ],
        out_specs=pl.BlockSpec((tm, d), lambda i: (i, 0)),
        out_shape=jax.ShapeDtypeStruct((t, d), BF16),
        compiler_params=_params(1),
        name="merge",
    )(attn, z, g, g, wa, wf)


def _route_body(m_ref, xa_ref, xb_ref, wo_ref, gain_ref, wr_ref, tri_ref,
                hmid_ref, hn_ref, eid_ref, cw_ref, rank_ref, cnt_ref, *, n_a_tiles, n_groups):
    i = pl.program_id(0)
    tm, d = xa_ref.shape
    dh = d // 2
    n_exp = cnt_ref.shape[0]

    @pl.when(i == 0)
    def _init():
        cnt_ref[...] = jnp.zeros_like(cnt_ref)

    x = jnp.where(i < n_a_tiles, xa_ref[...], xb_ref[...])
    h = x + jnp.dot(m_ref[...], wo_ref[...], preferred_element_type=F32)
    hmid_ref[...] = h
    hn = h * lax.rsqrt(jnp.mean(h * h, axis=-1, keepdims=True) + NORM_EPS) * gain_ref[...]
    hb = hn.astype(BF16)
    hn_ref[...] = _pack_bf16_pair(hn[:, :dh], hn[:, dh:])

    lt = lax.dot_general(wr_ref[...], hb, (((1,), (1,)), ((), ())), preferred_element_type=F32)
    row8 = lax.broadcasted_iota(I32, (EXPERTS_PER_GROUP, tm), 0)
    neg = jnp.float32(-jnp.inf)
    lg = jnp.where(row8 < n_groups, lt[0:8], neg)
    gmax = jnp.max(lg, axis=0, keepdims=True)
    gidx = jnp.min(jnp.where(lg == gmax, row8, 8), axis=0, keepdims=True)
    p_g = 1.0 / jnp.sum(jnp.exp(lg - gmax), axis=0, keepdims=True)
    sel = lt[8:8 + EXPERTS_PER_GROUP]
    for g in range(1, n_groups):
        sel = jnp.where(gidx == g, lt[8 + g * EXPERTS_PER_GROUP:8 + (g + 1) * EXPERTS_PER_GROUP], sel)
    m1 = jnp.max(sel, axis=0, keepdims=True)
    i1 = jnp.min(jnp.where(sel == m1, row8, 8), axis=0, keepdims=True)
    sel2 = jnp.where(row8 == i1, neg, sel)
    m2 = jnp.max(sel2, axis=0, keepdims=True)
    i2 = jnp.min(jnp.where(sel2 == m2, row8, 8), axis=0, keepdims=True)
    e21 = jnp.exp(m2 - m1)
    p1 = 1.0 / (1.0 + e21)
    p2 = e21 * p1
    e1 = gidx * EXPERTS_PER_GROUP + i1
    e2 = gidx * EXPERTS_PER_GROUP + i2
    eid_ref[0, 0:1, :] = e1
    eid_ref[0, 1:2, :] = e2
    cw_ref[0, 0:1, :] = p_g * p1
    cw_ref[0, 1:2, :] = p_g * p2

    row_e = lax.broadcasted_iota(I32, (n_exp, tm), 0)
    oh1 = (row_e == e1)
    oh2 = (row_e == e2)
    cs1 = jnp.dot(oh1.astype(BF16), tri_ref[...], preferred_element_type=F32)
    cs2 = jnp.dot(oh2.astype(BF16), tri_ref[...], preferred_element_type=F32)
    oh1f, oh2f = oh1.astype(F32), oh2.astype(F32)
    c1 = jnp.sum(oh1f, axis=1, keepdims=True)
    c2 = jnp.sum(oh2f, axis=1, keepdims=True)
    base = cnt_ref[:, 0:1]
    rank_ref[0, 0:1, :] = jnp.sum(oh1f * (base + cs1), axis=0, keepdims=True).astype(I32)
    rank_ref[0, 1:2, :] = jnp.sum(oh2f * (base + c1 + cs2), axis=0, keepdims=True).astype(I32)
    cnt_ref[...] = cnt_ref[...] + (c1 + c2)


def _outproj_route(merged, xa, xb, wout, gain, wr_t, tri, *, tm, n_groups, n_exp):
    t, d = merged.shape
    nt = t // tm
    na = xa.shape[0] // tm
    tok3 = lambda i: (i, 0, 0)
    body = functools.partial(_route_body, n_a_tiles=na, n_groups=n_groups)
    return pl.pallas_call(
        body,
        grid=(nt,),
        in_specs=[
            pl.BlockSpec((tm, d), lambda i: (i, 0)),
            pl.BlockSpec((tm, d), lambda i: (jnp.minimum(i, na - 1), 0)),
            pl.BlockSpec((tm, d), lambda i: (jnp.maximum(i - na, 0), 0)),
            _resident((d, d)),
            pl.BlockSpec((1, d), lambda i: (0, 0)),
            _resident(wr_t.shape),
            _resident((tm, tm)),
        ],
        out_specs=[
            pl.BlockSpec((tm, d), lambda i: (i, 0)),
            pl.BlockSpec((tm, d // 2), lambda i: (i, 0)),
            pl.BlockSpec((1, TOP_K, tm), tok3),
            pl.BlockSpec((1, TOP_K, tm), tok3),
            pl.BlockSpec((1, TOP_K, tm), tok3),
            pl.BlockSpec((n_exp, HEAD_DIM), lambda i: (0, 0)),
        ],
        out_shape=[
            jax.ShapeDtypeStruct((t, d), F32),
            jax.ShapeDtypeStruct((t, d // 2), U32),
            jax.ShapeDtypeStruct((nt, TOP_K, tm), I32),
            jax.ShapeDtypeStruct((nt, TOP_K, tm), F32),
            jax.ShapeDtypeStruct((nt, TOP_K, tm), I32),
            jax.ShapeDtypeStruct((n_exp, HEAD_DIM), F32),
        ],
        compiler_params=_params(1),
        name="outproj_route",
    )(merged, xa, xb, wout, gain, wr_t, tri)


def _issue_row_copies(idx_ref, n_rows, copy_for, per_trip=None):
    per = ROW_DMA_UNROLL

    def issue(a, carry):
        if per_trip is not None:
            per_trip(a)
        base = a * per
        for j in range(per):
            for k in range(TOP_K):
                copy_for(k, a * (per // SUBLANES) + j // SUBLANES, j % SUBLANES,
                         idx_ref[k * n_rows + base + j]).start()
        return carry

    lax.fori_loop(0, n_rows // per, issue, 0)


def _dispatch_body(pad_end_ref, dest_ref, x_ref, xs_ref, zbuf_ref, buf_ref, zsem, ld_sem, sem, *, bm, n_exp):
    i, n_tiles = pl.program_id(0), pl.num_programs(0)
    tg = buf_ref.shape[1]
    tm = tg * SUBLANES

    @pl.when(i == 0)
    def _zero_padding():
        zbuf_ref[...] = jnp.zeros_like(zbuf_ref)

        def last_block_copy(e):
            end = pad_end_ref[e]
            start = pad_end_ref[e - 1] if e > 0 else 0
            cp = pltpu.make_async_copy(zbuf_ref, xs_ref.at[pl.ds(pl.multiple_of(end - bm, bm), bm)], zsem)
            return end > start, cp

        for e in range(n_exp):
            nonempty, cp = last_block_copy(e)
            pl.when(nonempty)(cp.start)
        for e in range(n_exp):
            nonempty, cp = last_block_copy(e)
            pl.when(nonempty)(cp.wait)

    ring = DISPATCH_RING
    slot = i % ring

    def load(tile, dst_slot):
        rows = pl.ds(pl.multiple_of(tile * tg, tg), tg)
        return pltpu.make_async_copy(x_ref.at[rows], buf_ref.at[dst_slot], ld_sem.at[dst_slot])

    def wait_scatters(of_slot):
        for k in range(TOP_K):
            pltpu.make_async_copy(buf_ref.at[of_slot], buf_ref.at[of_slot], sem.at[of_slot]).wait()

    pl.when(i == 0)(load(0, 0).start)
    pl.when(i >= ring - 1)(lambda: wait_scatters((i + 1) % ring))
    pl.when(i + 1 < n_tiles)(load(i + 1, (i + 1) % ring).start)
    load(i, slot).wait()

    def copy_for(k, g, s, d):
        return pltpu.make_async_copy(buf_ref.at[slot, g, pl.ds(s, 1), :], xs_ref.at[pl.ds(d, 1), :], sem.at[slot])

    _issue_row_copies(dest_ref, tm, copy_for)

    @pl.when(i == n_tiles - 1)
    def _drain():
        for back in range(ring - 2, -1, -1):
            pl.when(i >= back)(lambda back=back: wait_scatters((i - back) % ring))


def _dispatch(hn2p, dest, pad_end, *, tm, bm, n_rows):
    t, dh = hn2p.shape
    n_exp = pad_end.shape[0]
    body = functools.partial(_dispatch_body, bm=bm, n_exp=n_exp)
    return pl.pallas_call(
        body,
        grid_spec=pltpu.PrefetchScalarGridSpec(
            num_scalar_prefetch=1,
            grid=(t // tm,),
            in_specs=[
                pl.BlockSpec((TOP_K * tm,), lambda i, pe: (i,), memory_space=pltpu.SMEM),
                pl.BlockSpec(memory_space=pl.ANY),
            ],
            out_specs=pl.BlockSpec(memory_space=pl.ANY),
            scratch_shapes=[pltpu.VMEM((bm, dh), U32),
                            pltpu.VMEM((DISPATCH_RING, tm // SUBLANES, SUBLANES, dh), U32),
                            pltpu.SemaphoreType.DMA(()), pltpu.SemaphoreType.DMA((DISPATCH_RING,)),
                            pltpu.SemaphoreType.DMA((DISPATCH_RING,))],
        ),
        out_shape=jax.ShapeDtypeStruct((n_rows, dh), U32),
        compiler_params=_params(1),
        name="dispatch",
    )(pad_end, dest.reshape(-1), hn2p.reshape(t // SUBLANES, SUBLANES, dh))


def _ffn_body(bexp_ref, nvalid_ref, xs_ref, wg_ref, wu_ref, wd_ref, ys_ref):
    b = pl.program_id(0)

    @pl.when(b < nvalid_ref[0])
    def _():
        dh = xs_ref.shape[1]
        lo, hi = _unpack_bf16_pair(xs_ref[...])
        lo, hi = lo.astype(BF16), hi.astype(BF16)
        g = jnp.dot(lo, wg_ref[0, :dh, :], preferred_element_type=F32)
        g = g + jnp.dot(hi, wg_ref[0, dh:, :], preferred_element_type=F32)
        u = jnp.dot(lo, wu_ref[0, :dh, :], preferred_element_type=F32)
        u = u + jnp.dot(hi, wu_ref[0, dh:, :], preferred_element_type=F32)
        hid = (jax.nn.silu(g) * u).astype(BF16)
        y = jnp.dot(hid, wd_ref[0], preferred_element_type=F32)
        ys_ref[...] = _pack_bf16_pair(y[:, :dh], y[:, dh:])


def _expert_ffn(xs, wg, wu, wd, block_expert, n_valid, *, bm):
    n_rows, dh = xs.shape
    n_exp, d, de = wg.shape
    nb = n_rows // bm
    blk = lambda b, be, nv: (jnp.minimum(b, nv[0] - 1), 0)
    wmap = lambda b, be, nv: (be[jnp.minimum(b, nv[0] - 1)], 0, 0)
    return pl.pallas_call(
        _ffn_body,
        grid_spec=pltpu.PrefetchScalarGridSpec(
            num_scalar_prefetch=2,
            grid=(nb,),
            in_specs=[
                pl.BlockSpec((bm, dh), blk),
                pl.BlockSpec((1, d, de), wmap),
                pl.BlockSpec((1, d, de), wmap),
                pl.BlockSpec((1, de, d), wmap),
            ],
            out_specs=pl.BlockSpec((bm, dh), blk),
        ),
        out_shape=jax.ShapeDtypeStruct((n_rows, dh), U32),
        compiler_params=_params(1),
        name="expert_ffn",
    )(block_expert, n_valid, xs, wg, wu, wd)


def _combine_body(dest_ref, dest_next_ref, h_ref, cw_ref, gain_ref, ys_ref, o_ref, buf_ref, sem):
    i = pl.program_id(0)
    last = pl.num_programs(0) - 1
    tm, d = h_ref.shape
    dh = d // 2
    slot = i % 2
    per = ROW_DMA_UNROLL

    def gather_into(dst_slot):
        def copy_for(k, g, s, row):
            return pltpu.make_async_copy(ys_ref.at[pl.ds(row, 1), :], buf_ref.at[dst_slot, k, g, pl.ds(s, 1), :],
                                         sem.at[dst_slot])
        return copy_for

    @pl.when(i == 0)
    def _first_tile():
        _issue_row_copies(dest_ref, tm, gather_into(0))

    for k in range(TOP_K):
        pltpu.make_async_copy(buf_ref.at[slot, k], buf_ref.at[slot, k], sem.at[slot]).wait()

    def finish_rows(a):
        for g in range(per // SUBLANES):
            grp = a * (per // SUBLANES) + g
            rows = pl.ds(pl.multiple_of(grp * SUBLANES, SUBLANES), SUBLANES)
            lo1, hi1 = _unpack_bf16_pair(buf_ref[slot, 0, grp])
            lo2, hi2 = _unpack_bf16_pair(buf_ref[slot, 1, grp])
            w1 = cw_ref[rows, 0:1]
            w2 = cw_ref[rows, 1:2]
            o_lo = h_ref[rows, :dh] + (w1 * lo1 + w2 * lo2)
            o_hi = h_ref[rows, dh:] + (w1 * hi1 + w2 * hi2)
            ssq = jnp.sum(o_lo * o_lo, axis=-1, keepdims=True) + jnp.sum(o_hi * o_hi, axis=-1, keepdims=True)
            r = lax.rsqrt(ssq / d + NORM_EPS)
            o_ref[rows, :dh] = o_lo * r * gain_ref[:, :dh]
            o_ref[rows, dh:] = o_hi * r * gain_ref[:, dh:]

    @pl.when(i < last)
    def _overlapped():
        _issue_row_copies(dest_next_ref, tm, gather_into(1 - slot), per_trip=finish_rows)

    @pl.when(i == last)
    def _tail():
        def trip(a, carry):
            finish_rows(a)
            return carry
        lax.fori_loop(0, tm // per, trip, 0)


def _combine(ys, dest, hmid, cw_t, gain, *, tm, row_off, n_rows):
    d = hmid.shape[1]
    dh = d // 2
    off, nt = row_off // tm, n_rows // tm
    dest_flat = dest.reshape(-1)
    return pl.pallas_call(
        _combine_body,
        grid=(nt,),
        in_specs=[
            pl.BlockSpec((TOP_K * tm,), lambda i: (off + i,), memory_space=pltpu.SMEM),
            pl.BlockSpec((TOP_K * tm,), lambda i: (off + jnp.minimum(i + 1, nt - 1),), memory_space=pltpu.SMEM),
            pl.BlockSpec((tm, d), lambda i: (off + i, 0)),
            pl.BlockSpec((tm, TOP_K), lambda i: (off + i, 0)),
            pl.BlockSpec((1, d), lambda i: (0, 0)),
            pl.BlockSpec(memory_space=pl.ANY),
        ],
        out_specs=pl.BlockSpec((tm, d), lambda i: (i, 0)),
        out_shape=jax.ShapeDtypeStruct((n_rows, d), F32),
        scratch_shapes=[pltpu.VMEM((2, TOP_K, tm // SUBLANES, SUBLANES, dh), U32), pltpu.SemaphoreType.DMA((2,))],
        compiler_params=_params(1),
        name="combine",
    )(dest_flat, dest_flat, hmid, cw_t, gain, ys)


def _rope_tables(n_tokens):
    rows = n_tokens // GRID_W
    row = jnp.repeat(jnp.arange(rows, dtype=F32), GRID_W)
    col = jnp.tile(jnp.arange(GRID_W, dtype=F32), rows)
    inv_freq = ROPE_THETA ** (-jnp.arange(ROPE_HALF, dtype=F32) / ROPE_HALF)
    ar, ac = row[:, None] * inv_freq, col[:, None] * inv_freq
    cos = jnp.concatenate([jnp.cos(ar), jnp.cos(ac), jnp.cos(ar), jnp.cos(ac)], axis=1)
    sin = jnp.concatenate([-jnp.sin(ar), -jnp.sin(ac), jnp.sin(ar), jnp.sin(ac)], axis=1)
    return cos, sin


def _tile(n, pref):
    return min(n, pref)


def kernel(x_prompt, x_sample, meta_tokens, mix_norm, w_in, q_gain, k_gain, w_attn_o, w_fourier_o, w_out,
           moe_norm, w_router_group, w_router_expert, w_expert_gate, w_expert_up, w_expert_down, final_norm):
    bp, n_p, d = x_prompt.shape
    bs, n_s, _ = x_sample.shape
    n_meta = meta_tokens.shape[0]
    t_p, t_s = bp * n_p, bs * n_s
    t = t_p + t_s
    f_dim = d // 2
    cg = f_dim // N_FOURIER_GROUPS
    n_groups = w_router_group.shape[-1]
    n_exp = w_router_expert.shape[-1]
    assert n_exp == n_groups * EXPERTS_PER_GROUP and n_groups <= 8

    tm = _tile(math.gcd(n_p, n_s), 512)
    bm = _tile(t, 512)

    q_lo, k_lo, v_lo = f_dim, f_dim + d, f_dim + d + d // 4
    w0 = w_in[0]
    w_in_b = jnp.concatenate([w0[:, :q_lo], _pair_major(w0[:, q_lo:k_lo]), _pair_major(w0[:, k_lo:v_lo]),
                              w0[:, v_lo:]], axis=1).astype(BF16)
    wa_b, wf_b, wo_b = w_attn_o[0].astype(BF16), w_fourier_o[0].astype(BF16), w_out[0].astype(BF16)
    qg = (_pair_major(q_gain[0]) * (HEAD_DIM ** -0.5 * LOG2E)).reshape(1, HEAD_DIM)
    kg = _pair_major(k_gain[0]).reshape(1, HEAD_DIM)
    wr_t = jnp.zeros((8 + n_exp, d), F32)
    wr_t = wr_t.at[:n_groups].set(w_router_group[0].T).at[8:].set(w_router_expert[0].T).astype(BF16)
    tri = (jnp.arange(tm)[:, None] < jnp.arange(tm)[None, :]).astype(BF16)
    cos, sin = _rope_tables(max(n_p, n_s))
    cdc, cds = _chan_dft(cg)
    gain_in = mix_norm[0].reshape(1, d)

    xp, xs_in = x_prompt.reshape(t_p, d), x_sample.reshape(t_s, d)
    tm_in = _tile(math.gcd(t_p, t_s), INPROJ_ROWS)
    proj = _inproj(xp, gain_in, w_in_b, None, tm=tm_in, row_off=0, t_total=t)
    uf, q, k, v, g = _inproj(xs_in, gain_in, w_in_b, proj, tm=tm_in, row_off=t_p, t_total=t)
    ufm, _, km, vm, _ = _inproj(meta_tokens, gain_in, w_in_b, None, tm=n_meta, row_off=0, t_total=n_meta)

    kinds = ((0, bp, n_p), (t_p, bs, n_s))
    n_kv = k.shape[1] // HEAD_DIM
    steps = [n_seq * n_kv * (n_tok // _tile(n_tok, ATTN_Q_TILE)) for _, n_seq, n_tok in kinds]
    w_exp = [w_expert_gate[0], w_expert_up[0], w_expert_down[0]]
    w_flat = [w.reshape(-1, w.shape[-1]) for w in w_exp]
    fused_cast = all(w.shape[0] % (SUBLANES * sum(steps)) == 0 for w in w_flat)
    cast_rows = [w.shape[0] // sum(steps) for w in w_flat] if fused_cast else []

    attn, z, w_b = None, None, None
    for (row_off, n_seq, n_tok), first_step in zip(kinds, (0, steps[0])):
        res = _attention(q, k, v, km, vm, qg, kg, cos, sin, attn, row_off=row_off, n_seq=n_seq,
                         n_tok=n_tok, tq=_tile(n_tok, ATTN_Q_TILE),
                         cast_src=w_flat if fused_cast else (), cast_prev=w_b, cast_rows=cast_rows,
                         cast_row_off=[first_step * r for r in cast_rows])
        attn, w_b = res[0], (res[1:] if fused_cast else None)
        z = _fourier(uf, ufm, _seq_dft(n_tok, n_meta), cdc, cds, z, row_off=row_off, n_seq=n_seq,
                     n_tok=n_tok, tl=_tile(n_tok, max(512, FOURIER_TILE_ELEMS // n_tok)), tf=_tile(f_dim, 512))
    if fused_cast:
        wg_b, wu_b, wd_b = [wb.reshape(w.shape) for wb, w in zip(w_b, w_exp)]
    else:
        wg_b, wu_b, wd_b = [w.astype(BF16) for w in w_exp]

    merged = _merge(attn, z, g, wa_b, wf_b, tm=tm)
    hmid, hn2p, eid, cw, rank, counts = _outproj_route(
        merged, xp, xs_in, wo_b, moe_norm[0].reshape(1, d), wr_t, tri, tm=tm, n_groups=n_groups, n_exp=n_exp)

    cnt = counts[:, 0].astype(I32)
    padded = (cnt + bm - 1) // bm * bm
    pad_end = jnp.cumsum(padded).astype(I32)
    pad_start = pad_end - padded
    onehot = eid[..., None] == jnp.arange(n_exp, dtype=I32)
    dest = jnp.sum(jnp.where(onehot, pad_start, 0), axis=-1) + rank
    nb = (t * TOP_K) // bm + n_exp
    n_valid = (pad_end[-1:] // bm).astype(I32)
    blk_start = jnp.arange(nb, dtype=I32) * bm
    block_expert = jnp.minimum(jnp.sum(pad_end[None, :] <= blk_start[:, None], axis=1), n_exp - 1).astype(I32)

    xs = _dispatch(hn2p, dest, pad_end, tm=tm, bm=bm, n_rows=nb * bm)
    ys = _expert_ffn(xs, wg_b, wu_b, wd_b, block_expert, n_valid, bm=bm)
    cw_t = jnp.transpose(cw, (0, 2, 1)).reshape(t, TOP_K)
    fgain = final_norm.reshape(1, d)
    y_p = _combine(ys, dest, hmid, cw_t, fgain, tm=tm, row_off=0, n_rows=t_p)
    y_s = _combine(ys, dest, hmid, cw_t, fgain, tm=tm, row_off=t_p, n_rows=t_s)
    return y_p.reshape(bp, n_p, d), y_s.reshape(bs, n_s, d)
```

```python
import functools
import math

import jax
import jax.numpy as jnp
from jax import lax
from jax.experimental import pallas as pl
from jax.experimental.pallas import tpu as pltpu

F32, BF16, I32, U32 = jnp.float32, jnp.bfloat16, jnp.int32, jnp.uint32

GRID_W = 64
HEAD_DIM = 128
ROPE_HALF = 32
Q_PER_KV = 4
N_FOURIER_GROUPS = 4
ROPE_THETA = 10000.0
NORM_EPS = 1e-6
TOP_K = 2
EXPERTS_PER_GROUP = 8
LOG2E = 1.4426950408889634
ATTN_KV_CHUNK = 512
ATTN_UNIT_ROWS = 256
ATTN_Q_TILE = 512
TOKEN_TILE = 512
EXPERT_BLOCK_ROWS = 512
FOURIER_COLS = 512
INPROJ_ROWS = 1024
FOURIER_TILE_ELEMS = 2 * 1024 * 1024
SUBLANES = 8
ROW_DMA_UNROLL = 32
DISPATCH_RING = 3

VMEM_LIMIT_BYTES = 56 * 1024 * 1024


def _params(n_axes):
    return pltpu.CompilerParams(dimension_semantics=("arbitrary",) * n_axes,
                                vmem_limit_bytes=VMEM_LIMIT_BYTES)


def _resident(shape):
    return pl.BlockSpec(shape, lambda *_: (0,) * len(shape), pipeline_mode=pl.Buffered(1))


def _pack_bf16_pair(lo, hi):
    lo_bits = lax.bitcast_convert_type(lo.astype(BF16).astype(F32), U32) >> 16
    hi_bits = lax.bitcast_convert_type(hi.astype(BF16).astype(F32), U32) & jnp.uint32(0xFFFF0000)
    return lo_bits | hi_bits


def _unpack_bf16_pair(w):
    lo = lax.bitcast_convert_type(w << 16, F32)
    hi = lax.bitcast_convert_type(w & jnp.uint32(0xFFFF0000), F32)
    return lo, hi


_F_TILES, _Q_TILES, _G_TILES = 2, 4, 8
_N_TILES = _F_TILES + _Q_TILES + 2 + _G_TILES


def _inproj_body(x_ref, gain_ref, w_ref, *rest):
    uf_ref, q_ref, k_ref, v_ref, g_ref, hn_ref, xbuf_ref, xsem = rest[-8:]
    i, n = pl.program_id(0), pl.program_id(1)
    tm = hn_ref.shape[0]
    slot = i % 2

    def x_copy(tile, dst_slot):
        rows = pl.ds(pl.multiple_of(tile * tm, tm), tm)
        return pltpu.make_async_copy(x_ref.at[rows], xbuf_ref.at[dst_slot], xsem.at[dst_slot])

    @pl.when(n == 0)
    def _norm():
        pl.when(i == 0)(x_copy(0, 0).start)
        x_copy(i, slot).wait()
        pl.when(i + 1 < pl.num_programs(0))(x_copy(i + 1, 1 - slot).start)
        x = xbuf_ref[slot]
        y = x * lax.rsqrt(jnp.mean(x * x, axis=-1, keepdims=True) + NORM_EPS)
        hn_ref[...] = (y * gain_ref[...]).astype(BF16)

    q0 = _F_TILES
    k0 = q0 + _Q_TILES

    def project_into(out_ref):
        def branch():
            acc = jnp.dot(hn_ref[...], w_ref[...], preferred_element_type=F32)
            out_ref[...] = acc.astype(out_ref.dtype)
        return branch

    pl.when(n < q0)(project_into(uf_ref))
    pl.when((n >= q0) & (n < k0))(project_into(q_ref))
    pl.when(n == k0)(project_into(k_ref))
    pl.when(n == k0 + 1)(project_into(v_ref))
    pl.when(n > k0 + 1)(project_into(g_ref))


def _inproj(x, gain, w_in, prev, *, tm, row_off, t_total):
    tx, d = x.shape
    tn = d // 4
    f_dim, q_dim, kv_dim = d // 2, d, d // 4
    assert w_in.shape == (d, _N_TILES * tn) and tx % tm == 0 and row_off % tm == 0
    ob = row_off // tm
    q0, k0, g0 = _F_TILES, _F_TILES + _Q_TILES, _F_TILES + _Q_TILES + 2
    in_specs = [
        pl.BlockSpec(memory_space=pl.ANY),
        pl.BlockSpec((1, d), lambda i, n: (0, 0)),
        pl.BlockSpec((d, tn), lambda i, n: (0, n)),
    ]
    args = [x, gain, w_in]
    aliases = {}
    if prev is not None:
        in_specs += [pl.BlockSpec(memory_space=pl.ANY)] * len(prev)
        aliases = {len(args) + j: j for j in range(len(prev))}
        args += list(prev)
    return pl.pallas_call(
        _inproj_body,
        grid=(tx // tm, _N_TILES),
        in_specs=in_specs,
        out_specs=[
            pl.BlockSpec((tm, tn), lambda i, n: (ob + i, jnp.clip(n, 0, _F_TILES - 1))),
            pl.BlockSpec((tm, tn), lambda i, n: (ob + i, jnp.clip(n - q0, 0, _Q_TILES - 1))),
            pl.BlockSpec((tm, tn), lambda i, n: (ob + i, 0)),
            pl.BlockSpec((tm, tn), lambda i, n: (ob + i, 0)),
            pl.BlockSpec((tm, tn), lambda i, n: (ob + i, jnp.clip(n - g0, 0, _G_TILES - 1))),
        ],
        out_shape=[
            jax.ShapeDtypeStruct((t_total, f_dim), BF16),
            jax.ShapeDtypeStruct((t_total, q_dim), F32),
            jax.ShapeDtypeStruct((t_total, kv_dim), F32),
            jax.ShapeDtypeStruct((t_total, kv_dim), BF16),
            jax.ShapeDtypeStruct((t_total, 2 * d), BF16),
        ],
        scratch_shapes=[pltpu.VMEM((tm, d), BF16), pltpu.VMEM((2, tm, d), F32), pltpu.SemaphoreType.DMA((2,))],
        input_output_aliases=aliases,
        compiler_params=_params(2),
        name="inproj",
    )(*args)


def _head_norm(a, gain):
    return a * lax.rsqrt(jnp.mean(a * a, axis=-1, keepdims=True) + NORM_EPS) * gain


def _pair_major(a):
    lead = a.shape[:-1]
    a = a.reshape(lead + (a.shape[-1] // HEAD_DIM, 2, 2, ROPE_HALF))
    return jnp.swapaxes(a, -3, -2).reshape(lead + (-1,))


def _rope(y, cos, sin):
    return y * cos + pltpu.roll(y, HEAD_DIM // 2, 1) * sin


def _attn_body(q_ref, k_ref, v_ref, km_ref, vm_ref, qg_ref, kg_ref, cq_ref, sq_ref, ck_ref, sk_ref, *rest,
               n_cast):
    cast_in = rest[:n_cast]
    kb_ref, kmb_ref, ve_ref, vme_ref, s_ref = rest[-5:]
    o_ref = rest[-6 - n_cast]
    cast_out = rest[len(rest) - 5 - n_cast:len(rest) - 5]
    tq, n_tok = q_ref.shape[0], k_ref.shape[0]

    for w_ref, wb_ref in zip(cast_in, cast_out):
        wb_ref[...] = w_ref[...].astype(BF16)

    tk = min(n_tok, ATTN_KV_CHUNK)
    n_chunks, lanes = n_tok // tk, tk // HEAD_DIM
    nt = (((1,), (1,)), ((), ()))

    @pl.when(pl.program_id(2) == 0)
    def _prepare_keys():
        kb_ref[...] = _rope(_head_norm(k_ref[...], kg_ref[...]), ck_ref[...], sk_ref[...]).astype(BF16)
        kmb_ref[...] = _head_norm(km_ref[...], kg_ref[...]).astype(BF16)
        ve_ref[:, :HEAD_DIM] = v_ref[...]
        ve_ref[:, HEAD_DIM:] = jnp.ones((n_tok, HEAD_DIM), BF16)
        vme_ref[:, :HEAD_DIM] = vm_ref[...]
        vme_ref[:, HEAD_DIM:] = jnp.ones((vm_ref.shape[0], HEAD_DIM), BF16)

    qg = qg_ref[...]
    ru = min(tq, ATTN_UNIT_ROWS)
    units = [(r0, h) for r0 in range(0, tq, ru) for h in range(Q_PER_KV)]

    def scores(u):
        r0, h = units[u]
        rows = slice(r0, r0 + ru)
        qh = _rope(_head_norm(q_ref[rows, h * HEAD_DIM:(h + 1) * HEAD_DIM], qg), cq_ref[rows, :], sq_ref[rows, :])
        qh = qh.astype(BF16)
        mp = None
        for c in range(n_chunks):
            s = lax.dot_general(qh, kb_ref[c * tk:(c + 1) * tk, :], nt, preferred_element_type=F32)
            s_ref[u % 2, :, c * tk:(c + 1) * tk] = s
            for j in range(lanes):
                t = s[:, j * HEAD_DIM:(j + 1) * HEAD_DIM]
                mp = t if mp is None else jnp.maximum(mp, t)
            yield None
        sm = lax.dot_general(qh, kmb_ref[...], nt, preferred_element_type=F32)
        m = jnp.maximum(jnp.max(mp, axis=-1, keepdims=True), jnp.max(sm, axis=-1, keepdims=True))
        yield m, sm

    def outputs(u, m, sm):
        r0, h = units[u]
        mb = jnp.broadcast_to(m, (ru, HEAD_DIM))
        pm = jnp.exp2(sm - m)
        acc = jnp.dot(pm.astype(BF16), vme_ref[...], preferred_element_type=F32)
        for c in range(n_chunks):
            tiles = []
            for j in range(lanes):
                col = c * tk + j * HEAD_DIM
                tiles.append(jnp.exp2(s_ref[u % 2, :, col:col + HEAD_DIM] - mb).astype(BF16))
            acc = acc + jnp.dot(jnp.concatenate(tiles, axis=1), ve_ref[c * tk:(c + 1) * tk, :],
                                preferred_element_type=F32)
            yield None
        o = acc[:, :HEAD_DIM] / acc[:, HEAD_DIM:]
        o_ref[r0:r0 + ru, h * HEAD_DIM:(h + 1) * HEAD_DIM] = o.astype(o_ref.dtype)
        yield None

    prev_out = None
    for u in range(len(units) + 1):
        cur = scores(u) if u < len(units) else None
        stats = None
        for _ in range(n_chunks + 1):
            if cur is not None:
                stats = next(cur)
            if prev_out is not None:
                next(prev_out)
        prev_out = outputs(u, *stats) if cur is not None else None


def _attention(q, k, v, km, vm, qg, kg, cos, sin, prev, *, row_off, n_seq, n_tok, tq,
               cast_src=(), cast_prev=None, cast_rows=(), cast_row_off=()):
    t, q_dim = q.shape
    n_kv = k.shape[1] // HEAD_DIM
    n_meta = km.shape[0]
    gw = Q_PER_KV * HEAD_DIM
    assert row_off % n_tok == 0 and n_tok % tq == 0
    qmap = lambda b, kh, qi: ((row_off + b * n_tok) // tq + qi, kh)
    kmap = lambda b, kh, qi: (row_off // n_tok + b, kh)
    mmap = lambda b, kh, qi: (0, kh)
    const = lambda b, kh, qi: (0, 0)
    qpos = lambda b, kh, qi: (qi, 0)
    in_specs = [
        pl.BlockSpec((tq, gw), qmap),
        pl.BlockSpec((n_tok, HEAD_DIM), kmap),
        pl.BlockSpec((n_tok, HEAD_DIM), kmap),
        pl.BlockSpec((n_meta, HEAD_DIM), mmap),
        pl.BlockSpec((n_meta, HEAD_DIM), mmap),
        pl.BlockSpec((1, HEAD_DIM), const),
        pl.BlockSpec((1, HEAD_DIM), const),
        pl.BlockSpec((tq, HEAD_DIM), qpos),
        pl.BlockSpec((tq, HEAD_DIM), qpos),
        pl.BlockSpec((n_tok, HEAD_DIM), const),
        pl.BlockSpec((n_tok, HEAD_DIM), const),
    ]
    args = [q, k, v, km, vm, qg, kg, cos, sin, cos, sin]
    nq = n_tok // tq
    out_specs = [pl.BlockSpec((tq, gw), qmap)]
    out_shape = [jax.ShapeDtypeStruct((t, q_dim), BF16)]
    for w, rows, off in zip(cast_src, cast_rows, cast_row_off):
        assert off % rows == 0 and rows % SUBLANES == 0
        wmap = lambda b, kh, qi, rows=rows, off=off: (off // rows + (b * n_kv + kh) * nq + qi, 0)
        in_specs.append(pl.BlockSpec((rows, w.shape[1]), wmap))
        args.append(w)
        out_specs.append(pl.BlockSpec((rows, w.shape[1]), wmap))
        out_shape.append(jax.ShapeDtypeStruct(w.shape, BF16))
    aliases = {}
    if prev is not None:
        in_specs.append(pl.BlockSpec(memory_space=pl.ANY))
        args.append(prev)
        aliases[len(args) - 1] = 0
    for j, wprev in enumerate(cast_prev or ()):
        in_specs.append(pl.BlockSpec(memory_space=pl.ANY))
        args.append(wprev)
        aliases[len(args) - 1] = 1 + j
    return pl.pallas_call(
        functools.partial(_attn_body, n_cast=len(cast_src)),
        grid=(n_seq, n_kv, nq),
        in_specs=in_specs,
        out_specs=out_specs,
        out_shape=out_shape,
        scratch_shapes=[
            pltpu.VMEM((n_tok, HEAD_DIM), BF16),
            pltpu.VMEM((n_meta, HEAD_DIM), BF16),
            pltpu.VMEM((n_tok, 2 * HEAD_DIM), BF16),
            pltpu.VMEM((n_meta, 2 * HEAD_DIM), BF16),
            pltpu.VMEM((2, min(tq, ATTN_UNIT_ROWS), n_tok), F32),
        ],
        input_output_aliases=aliases,
        compiler_params=_params(3),
        name="attention",
    )(*args)


def _dftgen_body(tac_ref, tas_ref, tbc_ref, tbs_ref, c_ref, ns_ref):
    ac, asn = tac_ref[0], tas_ref[0]
    bc, bsn = tbc_ref[...], tbs_ref[...]
    c_ref[...] = (ac * bc - asn * bsn).astype(BF16)
    ns_ref[...] = (-(asn * bc + ac * bsn)).astype(BF16)


def _seq_dft(n_tok, n_meta):
    length = n_tok + n_meta
    rb = min(n_tok, HEAD_DIM)
    w = 2.0 * math.pi / length
    scale = 1.0 / math.sqrt(length)
    p_real = n_meta + jnp.arange(n_tok, dtype=I32)

    def angles(p_rows, p_cols):
        return ((p_rows[:, None] * p_cols[None, :]) % length).astype(F32) * w

    ang_a = angles(n_meta + rb * jnp.arange(n_tok // rb, dtype=I32), p_real)
    ang_b = angles(jnp.arange(rb, dtype=I32), p_real)
    ang_m = angles(p_real, jnp.arange(n_meta, dtype=I32))
    tac = (jnp.cos(ang_a) * scale).reshape(n_tok // rb, 1, n_tok)
    tas = (jnp.sin(ang_a) * scale).reshape(n_tok // rb, 1, n_tok)
    blk = pl.BlockSpec((1, 1, n_tok), lambda a: (a, 0, 0))
    full = pl.BlockSpec((rb, n_tok), lambda a: (0, 0))
    cmat, nsmat = pl.pallas_call(
        _dftgen_body,
        grid=(n_tok // rb,),
        in_specs=[blk, blk, full, full],
        out_specs=[pl.BlockSpec((rb, n_tok), lambda a: (a, 0))] * 2,
        out_shape=[jax.ShapeDtypeStruct((n_tok, n_tok), BF16)] * 2,
        compiler_params=_params(1),
        name="dft_matrices",
    )(tac, tas, jnp.cos(ang_b), jnp.sin(ang_b))
    return cmat, nsmat, (jnp.cos(ang_m) * scale).astype(BF16), (-jnp.sin(ang_m) * scale).astype(BF16)


def _chan_dft(cg):
    idx = jnp.arange(cg, dtype=I32)
    ang = ((idx[:, None] * idx[None, :]) % cg).astype(F32) * (2.0 * math.pi / cg)
    scale = 1.0 / math.sqrt(cg)
    return (jnp.cos(ang) * scale).astype(BF16), (jnp.sin(ang) * scale).astype(BF16)


def _fourier_body(c_ref, ns_ref, cm_ref, nsm_ref, u_ref, um_ref, cdc_ref, cds_ref, *rest):
    z_ref = rest[-1]
    u, um = u_ref[...], um_ref[...]
    a = jnp.dot(c_ref[...], u, preferred_element_type=F32) + jnp.dot(cm_ref[...], um, preferred_element_type=F32)
    b = jnp.dot(ns_ref[...], u, preferred_element_type=F32) + jnp.dot(nsm_ref[...], um, preferred_element_type=F32)
    a, b = a.astype(BF16), b.astype(BF16)
    cg = cdc_ref.shape[0]
    for g in range(z_ref.shape[1] // cg):
        sl = slice(g * cg, (g + 1) * cg)
        z = jnp.dot(a[:, sl], cdc_ref[...], preferred_element_type=F32)
        z = z + jnp.dot(b[:, sl], cds_ref[...], preferred_element_type=F32)
        z_ref[:, sl] = z.astype(z_ref.dtype)


def _fourier(uf, ufm, mats, cdc, cds, prev, *, row_off, n_seq, n_tok, tl, tf):
    cmat, nsmat, cmeta, nsmeta = mats
    t, f_dim = uf.shape
    n_meta = ufm.shape[0]
    cg = cdc.shape[0]
    assert row_off % n_tok == 0 and n_tok % tl == 0 and f_dim % tf == 0 and tf % cg == 0
    cmap = lambda b, j, i: (i, 0)
    const = lambda b, j, i: (0, 0)
    in_specs = [
        pl.BlockSpec((tl, n_tok), cmap),
        pl.BlockSpec((tl, n_tok), cmap),
        pl.BlockSpec((tl, n_meta), cmap),
        pl.BlockSpec((tl, n_meta), cmap),
        pl.BlockSpec((n_tok, tf), lambda b, j, i: (row_off // n_tok + b, j)),
        pl.BlockSpec((n_meta, tf), lambda b, j, i: (0, j)),
        pl.BlockSpec((cg, cg), const),
        pl.BlockSpec((cg, cg), const),
    ]
    args = [cmat, nsmat, cmeta, nsmeta, uf, ufm, cdc, cds]
    aliases = {}
    if prev is not None:
        in_specs.append(pl.BlockSpec(memory_space=pl.ANY))
        args.append(prev)
        aliases = {len(args) - 1: 0}
    return pl.pallas_call(
        _fourier_body,
        grid=(n_seq, f_dim // tf, n_tok // tl),
        in_specs=in_specs,
        out_specs=pl.BlockSpec((tl, tf), lambda b, j, i: ((row_off + b * n_tok) // tl + i, j)),
        out_shape=jax.ShapeDtypeStruct((t, f_dim), BF16),
        input_output_aliases=aliases,
        compiler_params=_params(3),
        name="fourier",
    )(*args)


def _merge_body(a_ref, z_ref, ga_ref, gf_ref, wa_ref, wf_ref, o_ref):
    a = jnp.dot(a_ref[...], wa_ref[...], preferred_element_type=F32)
    f = jnp.dot(z_ref[...], wf_ref[...], preferred_element_type=F32)
    sga = jax.nn.sigmoid(ga_ref[...].astype(F32))
    sgf = jax.nn.sigmoid(gf_ref[...].astype(F32))
    o_ref[...] = (sga * a + sgf * f).astype(o_ref.dtype)


def _merge(attn, z, g, wa, wf, *, tm):
    t, d = attn.shape
    f_dim = z.shape[1]
    return pl.pallas_call(
        _merge_body,
        grid=(t // tm,),
        in_specs=[
            pl.BlockSpec((tm, d), lambda i: (i, 0)),
            pl.BlockSpec((tm, f_dim), lambda i: (i, 0)),
            pl.BlockSpec((tm, d), lambda i: (i, 0)),
            pl.BlockSpec((tm, d), lambda i: (i, 1)),
            _resident((d, d)),
            _resident((f_dim, d)),
        ],
        out_specs=pl.BlockSpec((tm, d), lambda i: (i, 0)),
        out_shape=jax.ShapeDtypeStruct((t, d), BF16),
        compiler_params=_params(1),
        name="merge",
    )(attn, z, g, g, wa, wf)


def _route_body(m_ref, xa_ref, xb_ref, wo_ref, gain_ref, wr_ref, tri_ref,
                hmid_ref, hn_ref, eid_ref, cw_ref, rank_ref, cnt_ref, *, n_a_tiles, n_groups):
    i = pl.program_id(0)
    tm, d = xa_ref.shape
    dh = d // 2
    n_exp = cnt_ref.shape[0]

    @pl.when(i == 0)
    def _init():
        cnt_ref[...] = jnp.zeros_like(cnt_ref)

    x = jnp.where(i < n_a_tiles, xa_ref[...], xb_ref[...])
    h = x + jnp.dot(m_ref[...], wo_ref[...], preferred_element_type=F32)
    hmid_ref[...] = h
    hn = h * lax.rsqrt(jnp.mean(h * h, axis=-1, keepdims=True) + NORM_EPS) * gain_ref[...]
    hb = hn.astype(BF16)
    hn_ref[...] = _pack_bf16_pair(hn[:, :dh], hn[:, dh:])

    lt = lax.dot_general(wr_ref[...], hb, (((1,), (1,)), ((), ())), preferred_element_type=F32)
    row8 = lax.broadcasted_iota(I32, (EXPERTS_PER_GROUP, tm), 0)
    neg = jnp.float32(-jnp.inf)
    lg = jnp.where(row8 < n_groups, lt[0:8], neg)
    gmax = jnp.max(lg, axis=0, keepdims=True)
    gidx = jnp.min(jnp.where(lg == gmax, row8, 8), axis=0, keepdims=True)
    p_g = 1.0 / jnp.sum(jnp.exp(lg - gmax), axis=0, keepdims=True)
    sel = lt[8:8 + EXPERTS_PER_GROUP]
    for g in range(1, n_groups):
        sel = jnp.where(gidx == g, lt[8 + g * EXPERTS_PER_GROUP:8 + (g + 1) * EXPERTS_PER_GROUP], sel)
    m1 = jnp.max(sel, axis=0, keepdims=True)
    i1 = jnp.min(jnp.where(sel == m1, row8, 8), axis=0, keepdims=True)
    sel2 = jnp.where(row8 == i1, neg, sel)
    m2 = jnp.max(sel2, axis=0, keepdims=True)
    i2 = jnp.min(jnp.where(sel2 == m2, row8, 8), axis=0, keepdims=True)
    e21 = jnp.exp(m2 - m1)
    p1 = 1.0 / (1.0 + e21)
    p2 = e21 * p1
    e1 = gidx * EXPERTS_PER_GROUP + i1
    e2 = gidx * EXPERTS_PER_GROUP + i2
    eid_ref[0, 0:1, :] = e1
    eid_ref[0, 1:2, :] = e2
    cw_ref[0, 0:1, :] = p_g * p1
    cw_ref[0, 1:2, :] = p_g * p2

    row_e = lax.broadcasted_iota(I32, (n_exp, tm), 0)
    oh1 = (row_e == e1)
    oh2 = (row_e == e2)
    cs1 = jnp.dot(oh1.astype(BF16), tri_ref[...], preferred_element_type=F32)
    cs2 = jnp.dot(oh2.astype(BF16), tri_ref[...], preferred_element_type=F32)
    oh1f, oh2f = oh1.astype(F32), oh2.astype(F32)
    c1 = jnp.sum(oh1f, axis=1, keepdims=True)
    c2 = jnp.sum(oh2f, axis=1, keepdims=True)
    base = cnt_ref[:, 0:1]
    rank_ref[0, 0:1, :] = jnp.sum(oh1f * (base + cs1), axis=0, keepdims=True).astype(I32)
    rank_ref[0, 1:2, :] = jnp.sum(oh2f * (base + c1 + cs2), axis=0, keepdims=True).astype(I32)
    cnt_ref[...] = cnt_ref[...] + (c1 + c2)


def _outproj_route(merged, xa, xb, wout, gain, wr_t, tri, *, tm, n_groups, n_exp):
    t, d = merged.shape
    nt = t // tm
    na = xa.shape[0] // tm
    tok3 = lambda i: (i, 0, 0)
    body = functools.partial(_route_body, n_a_tiles=na, n_groups=n_groups)
    return pl.pallas_call(
        body,
        grid=(nt,),
        in_specs=[
            pl.BlockSpec((tm, d), lambda i: (i, 0)),
            pl.BlockSpec((tm, d), lambda i: (jnp.minimum(i, na - 1), 0)),
            pl.BlockSpec((tm, d), lambda i: (jnp.maximum(i - na, 0), 0)),
            _resident((d, d)),
            pl.BlockSpec((1, d), lambda i: (0, 0)),
            _resident(wr_t.shape),
            _resident((tm, tm)),
        ],
        out_specs=[
            pl.BlockSpec((tm, d), lambda i: (i, 0)),
            pl.BlockSpec((tm, d // 2), lambda i: (i, 0)),
            pl.BlockSpec((1, TOP_K, tm), tok3),
            pl.BlockSpec((1, TOP_K, tm), tok3),
            pl.BlockSpec((1, TOP_K, tm), tok3),
            pl.BlockSpec((n_exp, HEAD_DIM), lambda i: (0, 0)),
        ],
        out_shape=[
            jax.ShapeDtypeStruct((t, d), F32),
            jax.ShapeDtypeStruct((t, d // 2), U32),
            jax.ShapeDtypeStruct((nt, TOP_K, tm), I32),
            jax.ShapeDtypeStruct((nt, TOP_K, tm), F32),
            jax.ShapeDtypeStruct((nt, TOP_K, tm), I32),
            jax.ShapeDtypeStruct((n_exp, HEAD_DIM), F32),
        ],
        compiler_params=_params(1),
        name="outproj_route",
    )(merged, xa, xb, wout, gain, wr_t, tri)


def _issue_row_copies(idx_ref, n_rows, copy_for, per_trip=None):
    per = ROW_DMA_UNROLL

    def issue(a, carry):
        if per_trip is not None:
            per_trip(a)
        base = a * per
        for j in range(per):
            for k in range(TOP_K):
                copy_for(k, a * (per // SUBLANES) + j // SUBLANES, j % SUBLANES,
                         idx_ref[k * n_rows + base + j]).start()
        return carry

    lax.fori_loop(0, n_rows // per, issue, 0)


def _dispatch_body(pad_end_ref, dest_ref, x_ref, xs_ref, zbuf_ref, buf_ref, zsem, ld_sem, sem, *, bm, n_exp):
    i, n_tiles = pl.program_id(0), pl.num_programs(0)
    tg = buf_ref.shape[1]
    tm = tg * SUBLANES

    @pl.when(i == 0)
    def _zero_padding():
        zbuf_ref[...] = jnp.zeros_like(zbuf_ref)

        def last_block_copy(e):
            end = pad_end_ref[e]
            start = pad_end_ref[e - 1] if e > 0 else 0
            cp = pltpu.make_async_copy(zbuf_ref, xs_ref.at[pl.ds(pl.multiple_of(end - bm, bm), bm)], zsem)
            return end > start, cp

        for e in range(n_exp):
            nonempty, cp = last_block_copy(e)
            pl.when(nonempty)(cp.start)
        for e in range(n_exp):
            nonempty, cp = last_block_copy(e)
            pl.when(nonempty)(cp.wait)

    ring = DISPATCH_RING
    slot = i % ring

    def load(tile, dst_slot):
        rows = pl.ds(pl.multiple_of(tile * tg, tg), tg)
        return pltpu.make_async_copy(x_ref.at[rows], buf_ref.at[dst_slot], ld_sem.at[dst_slot])

    def wait_scatters(of_slot):
        for k in range(TOP_K):
            pltpu.make_async_copy(buf_ref.at[of_slot], buf_ref.at[of_slot], sem.at[of_slot]).wait()

    pl.when(i == 0)(load(0, 0).start)
    pl.when(i >= ring - 1)(lambda: wait_scatters((i + 1) % ring))
    pl.when(i + 1 < n_tiles)(load(i + 1, (i + 1) % ring).start)
    load(i, slot).wait()

    def copy_for(k, g, s, d):
        return pltpu.make_async_copy(buf_ref.at[slot, g, pl.ds(s, 1), :], xs_ref.at[pl.ds(d, 1), :], sem.at[slot])

    _issue_row_copies(dest_ref, tm, copy_for)

    @pl.when(i == n_tiles - 1)
    def _drain():
        for back in range(ring - 2, -1, -1):
            pl.when(i >= back)(lambda back=back: wait_scatters((i - back) % ring))


def _dispatch(hn2p, dest, pad_end, *, tm, bm, n_rows):
    t, dh = hn2p.shape
    n_exp = pad_end.shape[0]
    body = functools.partial(_dispatch_body, bm=bm, n_exp=n_exp)
    return pl.pallas_call(
        body,
        grid_spec=pltpu.PrefetchScalarGridSpec(
            num_scalar_prefetch=1,
            grid=(t // tm,),
            in_specs=[
                pl.BlockSpec((TOP_K * tm,), lambda i, pe: (i,), memory_space=pltpu.SMEM),
                pl.BlockSpec(memory_space=pl.ANY),
            ],
            out_specs=pl.BlockSpec(memory_space=pl.ANY),
            scratch_shapes=[pltpu.VMEM((bm, dh), U32),
                            pltpu.VMEM((DISPATCH_RING, tm // SUBLANES, SUBLANES, dh), U32),
                            pltpu.SemaphoreType.DMA(()), pltpu.SemaphoreType.DMA((DISPATCH_RING,)),
                            pltpu.SemaphoreType.DMA((DISPATCH_RING,))],
        ),
        out_shape=jax.ShapeDtypeStruct((n_rows, dh), U32),
        compiler_params=_params(1),
        name="dispatch",
    )(pad_end, dest.reshape(-1), hn2p.reshape(t // SUBLANES, SUBLANES, dh))


def _ffn_body(bexp_ref, nvalid_ref, xs_ref, wg_ref, wu_ref, wd_ref, ys_ref):
    b = pl.program_id(0)

    @pl.when(b < nvalid_ref[0])
    def _():
        dh = xs_ref.shape[1]
        lo, hi = _unpack_bf16_pair(xs_ref[...])
        lo, hi = lo.astype(BF16), hi.astype(BF16)
        g = jnp.dot(lo, wg_ref[0, :dh, :], preferred_element_type=F32)
        g = g + jnp.dot(hi, wg_ref[0, dh:, :], preferred_element_type=F32)
        u = jnp.dot(lo, wu_ref[0, :dh, :], preferred_element_type=F32)
        u = u + jnp.dot(hi, wu_ref[0, dh:, :], preferred_element_type=F32)
        hid = (jax.nn.silu(g) * u).astype(BF16)
        y = jnp.dot(hid, wd_ref[0], preferred_element_type=F32)
        ys_ref[...] = _pack_bf16_pair(y[:, :dh], y[:, dh:])


def _expert_ffn(xs, wg, wu, wd, block_expert, n_valid, *, bm):
    n_rows, dh = xs.shape
    n_exp, d, de = wg.shape
    nb = n_rows // bm
    blk = lambda b, be, nv: (jnp.minimum(b, nv[0] - 1), 0)
    wmap = lambda b, be, nv: (be[jnp.minimum(b, nv[0] - 1)], 0, 0)
    return pl.pallas_call(
        _ffn_body,
        grid_spec=pltpu.PrefetchScalarGridSpec(
            num_scalar_prefetch=2,
            grid=(nb,),
            in_specs=[
                pl.BlockSpec((bm, dh), blk),
                pl.BlockSpec((1, d, de), wmap),
                pl.BlockSpec((1, d, de), wmap),
                pl.BlockSpec((1, de, d), wmap),
            ],
            out_specs=pl.BlockSpec((bm, dh), blk),
        ),
        out_shape=jax.ShapeDtypeStruct((n_rows, dh), U32),
        compiler_params=_params(1),
        name="expert_ffn",
    )(block_expert, n_valid, xs, wg, wu, wd)


def _combine_body(dest_ref, dest_next_ref, h_ref, cw_ref, gain_ref, ys_ref, o_ref, buf_ref, sem):
    i = pl.program_id(0)
    last = pl.num_programs(0) - 1
    tm, d = h_ref.shape
    dh = d // 2
    slot = i % 2
    per = ROW_DMA_UNROLL

    def gather_into(dst_slot):
        def copy_for(k, g, s, row):
            return pltpu.make_async_copy(ys_ref.at[pl.ds(row, 1), :], buf_ref.at[dst_slot, k, g, pl.ds(s, 1), :],
                                         sem.at[dst_slot])
        return copy_for

    @pl.when(i == 0)
    def _first_tile():
        _issue_row_copies(dest_ref, tm, gather_into(0))

    for k in range(TOP_K):
        pltpu.make_async_copy(buf_ref.at[slot, k], buf_ref.at[slot, k], sem.at[slot]).wait()

    def finish_rows(a):
        for g in range(per // SUBLANES):
            grp = a * (per // SUBLANES) + g
            rows = pl.ds(pl.multiple_of(grp * SUBLANES, SUBLANES), SUBLANES)
            lo1, hi1 = _unpack_bf16_pair(buf_ref[slot, 0, grp])
            lo2, hi2 = _unpack_bf16_pair(buf_ref[slot, 1, grp])
            w1 = cw_ref[rows, 0:1]
            w2 = cw_ref[rows, 1:2]
            o_lo = h_ref[rows, :dh] + (w1 * lo1 + w2 * lo2)
            o_hi = h_ref[rows, dh:] + (w1 * hi1 + w2 * hi2)
            ssq = jnp.sum(o_lo * o_lo, axis=-1, keepdims=True) + jnp.sum(o_hi * o_hi, axis=-1, keepdims=True)
            r = lax.rsqrt(ssq / d + NORM_EPS)
            o_ref[rows, :dh] = o_lo * r * gain_ref[:, :dh]
            o_ref[rows, dh:] = o_hi * r * gain_ref[:, dh:]

    @pl.when(i < last)
    def _overlapped():
        _issue_row_copies(dest_next_ref, tm, gather_into(1 - slot), per_trip=finish_rows)

    @pl.when(i == last)
    def _tail():
        def trip(a, carry):
            finish_rows(a)
            return carry
        lax.fori_loop(0, tm // per, trip, 0)


def _combine(ys, dest, hmid, cw_t, gain, *, tm, row_off, n_rows):
    d = hmid.shape[1]
    dh = d // 2
    off, nt = row_off // tm, n_rows // tm
    dest_flat = dest.reshape(-1)
    return pl.pallas_call(
        _combine_body,
        grid=(nt,),
        in_specs=[
            pl.BlockSpec((TOP_K * tm,), lambda i: (off + i,), memory_space=pltpu.SMEM),
            pl.BlockSpec((TOP_K * tm,), lambda i: (off + jnp.minimum(i + 1, nt - 1),), memory_space=pltpu.SMEM),
            pl.BlockSpec((tm, d), lambda i: (off + i, 0)),
            pl.BlockSpec((tm, TOP_K), lambda i: (off + i, 0)),
            pl.BlockSpec((1, d), lambda i: (0, 0)),
            pl.BlockSpec(memory_space=pl.ANY),
        ],
        out_specs=pl.BlockSpec((tm, d), lambda i: (i, 0)),
        out_shape=jax.ShapeDtypeStruct((n_rows, d), F32),
        scratch_shapes=[pltpu.VMEM((2, TOP_K, tm // SUBLANES, SUBLANES, dh), U32), pltpu.SemaphoreType.DMA((2,))],
        compiler_params=_params(1),
        name="combine",
    )(dest_flat, dest_flat, hmid, cw_t, gain, ys)


def _rope_tables(n_tokens):
    rows = n_tokens // GRID_W
    row = jnp.repeat(jnp.arange(rows, dtype=F32), GRID_W)
    col = jnp.tile(jnp.arange(GRID_W, dtype=F32), rows)
    inv_freq = ROPE_THETA ** (-jnp.arange(ROPE_HALF, dtype=F32) / ROPE_HALF)
    ar, ac = row[:, None] * inv_freq, col[:, None] * inv_freq
    cos = jnp.concatenate([jnp.cos(ar), jnp.cos(ac), jnp.cos(ar), jnp.cos(ac)], axis=1)
    sin = jnp.concatenate([-jnp.sin(ar), -jnp.sin(ac), jnp.sin(ar), jnp.sin(ac)], axis=1)
    return cos, sin


def _tile(n, pref):
    return min(n, pref)


def kernel(x_prompt, x_sample, meta_tokens, mix_norm, w_in, q_gain, k_gain, w_attn_o, w_fourier_o, w_out,
           moe_norm, w_router_group, w_router_expert, w_expert_gate, w_expert_up, w_expert_down, final_norm):
    bp, n_p, d = x_prompt.shape
    bs, n_s, _ = x_sample.shape
    n_meta = meta_tokens.shape[0]
    t_p, t_s = bp * n_p, bs * n_s
    t = t_p + t_s
    f_dim = d // 2
    cg = f_dim // N_FOURIER_GROUPS
    n_groups = w_router_group.shape[-1]
    n_exp = w_router_expert.shape[-1]
    assert n_exp == n_groups * EXPERTS_PER_GROUP and n_groups <= 8

    tm = _tile(math.gcd(n_p, n_s), TOKEN_TILE)
    bm = _tile(t, EXPERT_BLOCK_ROWS)

    q_lo, v_lo = f_dim, f_dim + d + d // 4
    w0 = w_in[0]
    w_in_b = jnp.concatenate([w0[:, :q_lo], _pair_major(w0[:, q_lo:v_lo]), w0[:, v_lo:]], axis=1).astype(BF16)
    wa_b, wf_b, wo_b = w_attn_o[0].astype(BF16), w_fourier_o[0].astype(BF16), w_out[0].astype(BF16)
    qg = (_pair_major(q_gain[0]) * (HEAD_DIM ** -0.5 * LOG2E)).reshape(1, HEAD_DIM)
    kg = _pair_major(k_gain[0]).reshape(1, HEAD_DIM)
    wr_t = jnp.zeros((8 + n_exp, d), F32)
    wr_t = wr_t.at[:n_groups].set(w_router_group[0].T).at[8:].set(w_router_expert[0].T).astype(BF16)
    tri = (jnp.arange(tm)[:, None] < jnp.arange(tm)[None, :]).astype(BF16)
    cos, sin = _rope_tables(max(n_p, n_s))
    cdc, cds = _chan_dft(cg)
    gain_in = mix_norm[0].reshape(1, d)

    xp, xs_in = x_prompt.reshape(t_p, d), x_sample.reshape(t_s, d)
    tm_in = _tile(math.gcd(t_p, t_s), INPROJ_ROWS)
    proj = _inproj(xp, gain_in, w_in_b, None, tm=tm_in, row_off=0, t_total=t)
    uf, q, k, v, g = _inproj(xs_in, gain_in, w_in_b, proj, tm=tm_in, row_off=t_p, t_total=t)
    ufm, _, km, vm, _ = _inproj(meta_tokens, gain_in, w_in_b, None, tm=n_meta, row_off=0, t_total=n_meta)

    kinds = ((0, bp, n_p), (t_p, bs, n_s))
    n_kv = k.shape[1] // HEAD_DIM
    steps = [n_seq * n_kv * (n_tok // _tile(n_tok, ATTN_Q_TILE)) for _, n_seq, n_tok in kinds]
    w_exp = [w_expert_gate[0], w_expert_up[0], w_expert_down[0]]
    w_flat = [w.reshape(-1, w.shape[-1]) for w in w_exp]
    fused_cast = all(w.shape[0] % (SUBLANES * sum(steps)) == 0 for w in w_flat)
    cast_rows = [w.shape[0] // sum(steps) for w in w_flat] if fused_cast else []

    attn, z, w_b = None, None, None
    for (row_off, n_seq, n_tok), first_step in zip(kinds, (0, steps[0])):
        res = _attention(q, k, v, km, vm, qg, kg, cos, sin, attn, row_off=row_off, n_seq=n_seq,
                         n_tok=n_tok, tq=_tile(n_tok, ATTN_Q_TILE),
                         cast_src=w_flat if fused_cast else (), cast_prev=w_b, cast_rows=cast_rows,
                         cast_row_off=[first_step * r for r in cast_rows])
        attn, w_b = res[0], (res[1:] if fused_cast else None)
        z = _fourier(uf, ufm, _seq_dft(n_tok, n_meta), cdc, cds, z, row_off=row_off, n_seq=n_seq,
                     n_tok=n_tok, tl=_tile(n_tok, max(TOKEN_TILE, FOURIER_TILE_ELEMS // n_tok)),
                     tf=_tile(f_dim, FOURIER_COLS))
    if fused_cast:
        wg_b, wu_b, wd_b = [wb.reshape(w.shape) for wb, w in zip(w_b, w_exp)]
    else:
        wg_b, wu_b, wd_b = [w.astype(BF16) for w in w_exp]

    merged = _merge(attn, z, g, wa_b, wf_b, tm=tm)
    hmid, hn2p, eid, cw, rank, counts = _outproj_route(
        merged, xp, xs_in, wo_b, moe_norm[0].reshape(1, d), wr_t, tri, tm=tm, n_groups=n_groups, n_exp=n_exp)

    cnt = counts[:, 0].astype(I32)
    padded = (cnt + bm - 1) // bm * bm
    pad_end = jnp.cumsum(padded).astype(I32)
    pad_start = pad_end - padded
    onehot = eid[..., None] == jnp.arange(n_exp, dtype=I32)
    dest = jnp.sum(jnp.where(onehot, pad_start, 0), axis=-1) + rank
    nb = (t * TOP_K) // bm + n_exp
    n_valid = (pad_end[-1:] // bm).astype(I32)
    blk_start = jnp.arange(nb, dtype=I32) * bm
    block_expert = jnp.minimum(jnp.sum(pad_end[None, :] <= blk_start[:, None], axis=1), n_exp - 1).astype(I32)

    xs = _dispatch(hn2p, dest, pad_end, tm=tm, bm=bm, n_rows=nb * bm)
    ys = _expert_ffn(xs, wg_b, wu_b, wd_b, block_expert, n_valid, bm=bm)
    cw_t = jnp.transpose(cw, (0, 2, 1)).reshape(t, TOP_K)
    fgain = final_norm.reshape(1, d)
    y_p = _combine(ys, dest, hmid, cw_t, fgain, tm=tm, row_off=0, n_rows=t_p)
    y_s = _combine(ys, dest, hmid, cw_t, fgain, tm=tm, row_off=t_p, n_rows=t_s)
    return y_p.reshape(bp, n_p, d), y_s.reshape(bs, n_s, d)
```

```python
import functools
import math

import jax
import jax.numpy as jnp
from jax import lax
from jax.experimental import pallas as pl
from jax.experimental.pallas import tpu as pltpu

F32, BF16, I32, U32 = jnp.float32, jnp.bfloat16, jnp.int32, jnp.uint32

GRID_W = 64
HEAD_DIM = 128
ROPE_HALF = 32
Q_PER_KV = 4
N_FOURIER_GROUPS = 4
ROPE_THETA = 10000.0
NORM_EPS = 1e-6
TOP_K = 2
EXPERTS_PER_GROUP = 8
LOG2E = 1.4426950408889634
ATTN_KV_CHUNK = 512
ATTN_UNIT_ROWS = 256
ATTN_Q_TILE = 512
TOKEN_TILE = 512
EXPERT_BLOCK_ROWS = 512
FOURIER_COLS = 512
INPROJ_ROWS = 1024
FOURIER_TILE_ELEMS = 2 * 1024 * 1024
SUBLANES = 8
ROW_DMA_UNROLL = 32
ISSUE_PARTS = 2
DISPATCH_RING = 3

VMEM_LIMIT_BYTES = 56 * 1024 * 1024


def _params(n_axes):
    return pltpu.CompilerParams(dimension_semantics=("arbitrary",) * n_axes,
                                vmem_limit_bytes=VMEM_LIMIT_BYTES)


def _resident(shape):
    return pl.BlockSpec(shape, lambda *_: (0,) * len(shape), pipeline_mode=pl.Buffered(1))


def _pack_bf16_pair(lo, hi):
    lo_bits = lax.bitcast_convert_type(lo.astype(BF16).astype(F32), U32) >> 16
    hi_bits = lax.bitcast_convert_type(hi.astype(BF16).astype(F32), U32) & jnp.uint32(0xFFFF0000)
    return lo_bits | hi_bits


def _unpack_bf16_pair(w):
    lo = lax.bitcast_convert_type(w << 16, F32)
    hi = lax.bitcast_convert_type(w & jnp.uint32(0xFFFF0000), F32)
    return lo, hi


_F_TILES, _Q_TILES, _G_TILES = 2, 4, 8
_N_TILES = _F_TILES + _Q_TILES + 2 + _G_TILES


def _inproj_body(x_ref, gain_ref, w_ref, *rest):
    uf_ref, q_ref, k_ref, v_ref, g_ref, hn_ref, xbuf_ref, xsem = rest[-8:]
    i, n = pl.program_id(0), pl.program_id(1)
    tm = hn_ref.shape[0]
    slot = i % 2

    def x_copy(tile, dst_slot):
        rows = pl.ds(pl.multiple_of(tile * tm, tm), tm)
        return pltpu.make_async_copy(x_ref.at[rows], xbuf_ref.at[dst_slot], xsem.at[dst_slot])

    @pl.when(n == 0)
    def _norm():
        pl.when(i == 0)(x_copy(0, 0).start)
        x_copy(i, slot).wait()
        pl.when(i + 1 < pl.num_programs(0))(x_copy(i + 1, 1 - slot).start)
        x = xbuf_ref[slot]
        y = x * lax.rsqrt(jnp.mean(x * x, axis=-1, keepdims=True) + NORM_EPS)
        hn_ref[...] = (y * gain_ref[...]).astype(BF16)

    q0 = _F_TILES
    k0 = q0 + _Q_TILES

    def project_into(out_ref):
        def branch():
            acc = jnp.dot(hn_ref[...], w_ref[...], preferred_element_type=F32)
            out_ref[...] = acc.astype(out_ref.dtype)
        return branch

    pl.when(n < q0)(project_into(uf_ref))
    pl.when((n >= q0) & (n < k0))(project_into(q_ref))
    pl.when(n == k0)(project_into(k_ref))
    pl.when(n == k0 + 1)(project_into(v_ref))
    pl.when(n > k0 + 1)(project_into(g_ref))


def _inproj(x, gain, w_in, prev, *, tm, row_off, t_total):
    tx, d = x.shape
    tn = d // 4
    f_dim, q_dim, kv_dim = d // 2, d, d // 4
    assert w_in.shape == (d, _N_TILES * tn) and tx % tm == 0 and row_off % tm == 0
    ob = row_off // tm
    q0, k0, g0 = _F_TILES, _F_TILES + _Q_TILES, _F_TILES + _Q_TILES + 2
    in_specs = [
        pl.BlockSpec(memory_space=pl.ANY),
        pl.BlockSpec((1, d), lambda i, n: (0, 0)),
        pl.BlockSpec((d, tn), lambda i, n: (0, n)),
    ]
    args = [x, gain, w_in]
    aliases = {}
    if prev is not None:
        in_specs += [pl.BlockSpec(memory_space=pl.ANY)] * len(prev)
        aliases = {len(args) + j: j for j in range(len(prev))}
        args += list(prev)
    return pl.pallas_call(
        _inproj_body,
        grid=(tx // tm, _N_TILES),
        in_specs=in_specs,
        out_specs=[
            pl.BlockSpec((tm, tn), lambda i, n: (ob + i, jnp.clip(n, 0, _F_TILES - 1))),
            pl.BlockSpec((tm, tn), lambda i, n: (ob + i, jnp.clip(n - q0, 0, _Q_TILES - 1))),
            pl.BlockSpec((tm, tn), lambda i, n: (ob + i, 0)),
            pl.BlockSpec((tm, tn), lambda i, n: (ob + i, 0)),
            pl.BlockSpec((tm, tn), lambda i, n: (ob + i, jnp.clip(n - g0, 0, _G_TILES - 1))),
        ],
        out_shape=[
            jax.ShapeDtypeStruct((t_total, f_dim), BF16),
            jax.ShapeDtypeStruct((t_total, q_dim), F32),
            jax.ShapeDtypeStruct((t_total, kv_dim), F32),
            jax.ShapeDtypeStruct((t_total, kv_dim), BF16),
            jax.ShapeDtypeStruct((t_total, 2 * d), BF16),
        ],
        scratch_shapes=[pltpu.VMEM((tm, d), BF16), pltpu.VMEM((2, tm, d), F32), pltpu.SemaphoreType.DMA((2,))],
        input_output_aliases=aliases,
        compiler_params=_params(2),
        name="inproj",
    )(*args)


def _head_norm(a, gain):
    return a * lax.rsqrt(jnp.mean(a * a, axis=-1, keepdims=True) + NORM_EPS) * gain


def _pair_major(a):
    lead = a.shape[:-1]
    a = a.reshape(lead + (a.shape[-1] // HEAD_DIM, 2, 2, ROPE_HALF))
    return jnp.swapaxes(a, -3, -2).reshape(lead + (-1,))


def _rope(y, cos, sin):
    return y * cos + pltpu.roll(y, HEAD_DIM // 2, 1) * sin


def _attn_body(q_ref, k_ref, v_ref, km_ref, vm_ref, qg_ref, kg_ref, cq_ref, sq_ref, ck_ref, sk_ref, *rest,
               n_cast):
    cast_in = rest[:n_cast]
    kb_ref, kmb_ref, ve_ref, vme_ref, s_ref = rest[-5:]
    o_ref = rest[-6 - n_cast]
    cast_out = rest[len(rest) - 5 - n_cast:len(rest) - 5]
    tq, n_tok = q_ref.shape[0], k_ref.shape[0]

    for w_ref, wb_ref in zip(cast_in, cast_out):
        wb_ref[...] = w_ref[...].astype(BF16)

    tk = min(n_tok, ATTN_KV_CHUNK)
    n_chunks, lanes = n_tok // tk, tk // HEAD_DIM
    nt = (((1,), (1,)), ((), ()))

    @pl.when(pl.program_id(2) == 0)
    def _prepare_keys():
        kb_ref[...] = _rope(_head_norm(k_ref[...], kg_ref[...]), ck_ref[...], sk_ref[...]).astype(BF16)
        kmb_ref[...] = _head_norm(km_ref[...], kg_ref[...]).astype(BF16)
        ve_ref[:, :HEAD_DIM] = v_ref[...]
        ve_ref[:, HEAD_DIM:] = jnp.ones((n_tok, HEAD_DIM), BF16)
        vme_ref[:, :HEAD_DIM] = vm_ref[...]
        vme_ref[:, HEAD_DIM:] = jnp.ones((vm_ref.shape[0], HEAD_DIM), BF16)

    qg = qg_ref[...]
    ru = min(tq, ATTN_UNIT_ROWS)
    units = [(r0, h) for r0 in range(0, tq, ru) for h in range(Q_PER_KV)]

    def scores(u):
        r0, h = units[u]
        rows = slice(r0, r0 + ru)
        qh = _rope(_head_norm(q_ref[rows, h * HEAD_DIM:(h + 1) * HEAD_DIM], qg), cq_ref[rows, :], sq_ref[rows, :])
        qh = qh.astype(BF16)
        mp = None
        for c in range(n_chunks):
            s = lax.dot_general(qh, kb_ref[c * tk:(c + 1) * tk, :], nt, preferred_element_type=F32)
            s_ref[u % 2, :, c * tk:(c + 1) * tk] = s
            for j in range(lanes):
                t = s[:, j * HEAD_DIM:(j + 1) * HEAD_DIM]
                mp = t if mp is None else jnp.maximum(mp, t)
            yield None
        sm = lax.dot_general(qh, kmb_ref[...], nt, preferred_element_type=F32)
        m = jnp.maximum(jnp.max(mp, axis=-1, keepdims=True), jnp.max(sm, axis=-1, keepdims=True))
        yield m, sm

    def outputs(u, m, sm):
        r0, h = units[u]
        mb = jnp.broadcast_to(m, (ru, HEAD_DIM))
        pm = jnp.exp2(sm - m)
        acc = jnp.dot(pm.astype(BF16), vme_ref[...], preferred_element_type=F32)
        for c in range(n_chunks):
            tiles = []
            for j in range(lanes):
                col = c * tk + j * HEAD_DIM
                tiles.append(jnp.exp2(s_ref[u % 2, :, col:col + HEAD_DIM] - mb).astype(BF16))
            acc = acc + jnp.dot(jnp.concatenate(tiles, axis=1), ve_ref[c * tk:(c + 1) * tk, :],
                                preferred_element_type=F32)
            yield None
        o = acc[:, :HEAD_DIM] / acc[:, HEAD_DIM:]
        o_ref[r0:r0 + ru, h * HEAD_DIM:(h + 1) * HEAD_DIM] = o.astype(o_ref.dtype)
        yield None

    prev_out = None
    for u in range(len(units) + 1):
        cur = scores(u) if u < len(units) else None
        stats = None
        for _ in range(n_chunks + 1):
            if cur is not None:
                stats = next(cur)
            if prev_out is not None:
                next(prev_out)
        prev_out = outputs(u, *stats) if cur is not None else None


def _attention(q, k, v, km, vm, qg, kg, cos, sin, prev, *, row_off, n_seq, n_tok, tq,
               cast_src=(), cast_prev=None, cast_rows=(), cast_row_off=()):
    t, q_dim = q.shape
    n_kv = k.shape[1] // HEAD_DIM
    n_meta = km.shape[0]
    gw = Q_PER_KV * HEAD_DIM
    assert row_off % n_tok == 0 and n_tok % tq == 0
    qmap = lambda b, kh, qi: ((row_off + b * n_tok) // tq + qi, kh)
    kmap = lambda b, kh, qi: (row_off // n_tok + b, kh)
    mmap = lambda b, kh, qi: (0, kh)
    const = lambda b, kh, qi: (0, 0)
    qpos = lambda b, kh, qi: (qi, 0)
    in_specs = [
        pl.BlockSpec((tq, gw), qmap),
        pl.BlockSpec((n_tok, HEAD_DIM), kmap),
        pl.BlockSpec((n_tok, HEAD_DIM), kmap),
        pl.BlockSpec((n_meta, HEAD_DIM), mmap),
        pl.BlockSpec((n_meta, HEAD_DIM), mmap),
        pl.BlockSpec((1, HEAD_DIM), const),
        pl.BlockSpec((1, HEAD_DIM), const),
        pl.BlockSpec((tq, HEAD_DIM), qpos),
        pl.BlockSpec((tq, HEAD_DIM), qpos),
        pl.BlockSpec((n_tok, HEAD_DIM), const),
        pl.BlockSpec((n_tok, HEAD_DIM), const),
    ]
    args = [q, k, v, km, vm, qg, kg, cos, sin, cos, sin]
    nq = n_tok // tq
    out_specs = [pl.BlockSpec((tq, gw), qmap)]
    out_shape = [jax.ShapeDtypeStruct((t, q_dim), BF16)]
    for w, rows, off in zip(cast_src, cast_rows, cast_row_off):
        assert off % rows == 0 and rows % SUBLANES == 0
        wmap = lambda b, kh, qi, rows=rows, off=off: (off // rows + (b * n_kv + kh) * nq + qi, 0)
        in_specs.append(pl.BlockSpec((rows, w.shape[1]), wmap))
        args.append(w)
        out_specs.append(pl.BlockSpec((rows, w.shape[1]), wmap))
        out_shape.append(jax.ShapeDtypeStruct(w.shape, BF16))
    aliases = {}
    if prev is not None:
        in_specs.append(pl.BlockSpec(memory_space=pl.ANY))
        args.append(prev)
        aliases[len(args) - 1] = 0
    for j, wprev in enumerate(cast_prev or ()):
        in_specs.append(pl.BlockSpec(memory_space=pl.ANY))
        args.append(wprev)
        aliases[len(args) - 1] = 1 + j
    return pl.pallas_call(
        functools.partial(_attn_body, n_cast=len(cast_src)),
        grid=(n_seq, n_kv, nq),
        in_specs=in_specs,
        out_specs=out_specs,
        out_shape=out_shape,
        scratch_shapes=[
            pltpu.VMEM((n_tok, HEAD_DIM), BF16),
            pltpu.VMEM((n_meta, HEAD_DIM), BF16),
            pltpu.VMEM((n_tok, 2 * HEAD_DIM), BF16),
            pltpu.VMEM((n_meta, 2 * HEAD_DIM), BF16),
            pltpu.VMEM((2, min(tq, ATTN_UNIT_ROWS), n_tok), F32),
        ],
        input_output_aliases=aliases,
        compiler_params=_params(3),
        name="attention",
    )(*args)


def _dftgen_body(tac_ref, tas_ref, tbc_ref, tbs_ref, c_ref, ns_ref):
    ac, asn = tac_ref[0], tas_ref[0]
    bc, bsn = tbc_ref[...], tbs_ref[...]
    c_ref[...] = (ac * bc - asn * bsn).astype(BF16)
    ns_ref[...] = (-(asn * bc + ac * bsn)).astype(BF16)


def _seq_dft(n_tok, n_meta):
    length = n_tok + n_meta
    rb = min(n_tok, HEAD_DIM)
    w = 2.0 * math.pi / length
    scale = 1.0 / math.sqrt(length)
    p_real = n_meta + jnp.arange(n_tok, dtype=I32)

    def angles(p_rows, p_cols):
        return ((p_rows[:, None] * p_cols[None, :]) % length).astype(F32) * w

    ang_a = angles(n_meta + rb * jnp.arange(n_tok // rb, dtype=I32), p_real)
    ang_b = angles(jnp.arange(rb, dtype=I32), p_real)
    ang_m = angles(p_real, jnp.arange(n_meta, dtype=I32))
    tac = (jnp.cos(ang_a) * scale).reshape(n_tok // rb, 1, n_tok)
    tas = (jnp.sin(ang_a) * scale).reshape(n_tok // rb, 1, n_tok)
    blk = pl.BlockSpec((1, 1, n_tok), lambda a: (a, 0, 0))
    full = pl.BlockSpec((rb, n_tok), lambda a: (0, 0))
    cmat, nsmat = pl.pallas_call(
        _dftgen_body,
        grid=(n_tok // rb,),
        in_specs=[blk, blk, full, full],
        out_specs=[pl.BlockSpec((rb, n_tok), lambda a: (a, 0))] * 2,
        out_shape=[jax.ShapeDtypeStruct((n_tok, n_tok), BF16)] * 2,
        compiler_params=_params(1),
        name="dft_matrices",
    )(tac, tas, jnp.cos(ang_b), jnp.sin(ang_b))
    return cmat, nsmat, (jnp.cos(ang_m) * scale).astype(BF16), (-jnp.sin(ang_m) * scale).astype(BF16)


def _chan_dft(cg):
    idx = jnp.arange(cg, dtype=I32)
    ang = ((idx[:, None] * idx[None, :]) % cg).astype(F32) * (2.0 * math.pi / cg)
    scale = 1.0 / math.sqrt(cg)
    return (jnp.cos(ang) * scale).astype(BF16), (jnp.sin(ang) * scale).astype(BF16)


def _fourier_body(c_ref, ns_ref, cm_ref, nsm_ref, u_ref, um_ref, cdc_ref, cds_ref, *rest):
    z_ref = rest[-1]
    u, um = u_ref[...], um_ref[...]
    a = jnp.dot(c_ref[...], u, preferred_element_type=F32) + jnp.dot(cm_ref[...], um, preferred_element_type=F32)
    b = jnp.dot(ns_ref[...], u, preferred_element_type=F32) + jnp.dot(nsm_ref[...], um, preferred_element_type=F32)
    a, b = a.astype(BF16), b.astype(BF16)
    cg = cdc_ref.shape[0]
    for g in range(z_ref.shape[1] // cg):
        sl = slice(g * cg, (g + 1) * cg)
        z = jnp.dot(a[:, sl], cdc_ref[...], preferred_element_type=F32)
        z = z + jnp.dot(b[:, sl], cds_ref[...], preferred_element_type=F32)
        z_ref[:, sl] = z.astype(z_ref.dtype)


def _fourier(uf, ufm, mats, cdc, cds, prev, *, row_off, n_seq, n_tok, tl, tf):
    cmat, nsmat, cmeta, nsmeta = mats
    t, f_dim = uf.shape
    n_meta = ufm.shape[0]
    cg = cdc.shape[0]
    assert row_off % n_tok == 0 and n_tok % tl == 0 and f_dim % tf == 0 and tf % cg == 0
    cmap = lambda b, j, i: (i, 0)
    const = lambda b, j, i: (0, 0)
    in_specs = [
        pl.BlockSpec((tl, n_tok), cmap),
        pl.BlockSpec((tl, n_tok), cmap),
        pl.BlockSpec((tl, n_meta), cmap),
        pl.BlockSpec((tl, n_meta), cmap),
        pl.BlockSpec((n_tok, tf), lambda b, j, i: (row_off // n_tok + b, j)),
        pl.BlockSpec((n_meta, tf), lambda b, j, i: (0, j)),
        pl.BlockSpec((cg, cg), const),
        pl.BlockSpec((cg, cg), const),
    ]
    args = [cmat, nsmat, cmeta, nsmeta, uf, ufm, cdc, cds]
    aliases = {}
    if prev is not None:
        in_specs.append(pl.BlockSpec(memory_space=pl.ANY))
        args.append(prev)
        aliases = {len(args) - 1: 0}
    return pl.pallas_call(
        _fourier_body,
        grid=(n_seq, f_dim // tf, n_tok // tl),
        in_specs=in_specs,
        out_specs=pl.BlockSpec((tl, tf), lambda b, j, i: ((row_off + b * n_tok) // tl + i, j)),
        out_shape=jax.ShapeDtypeStruct((t, f_dim), BF16),
        input_output_aliases=aliases,
        compiler_params=_params(3),
        name="fourier",
    )(*args)


def _merge_body(a_ref, z_ref, ga_ref, gf_ref, wa_ref, wf_ref, o_ref):
    a = jnp.dot(a_ref[...], wa_ref[...], preferred_element_type=F32)
    f = jnp.dot(z_ref[...], wf_ref[...], preferred_element_type=F32)
    sga = jax.nn.sigmoid(ga_ref[...].astype(F32))
    sgf = jax.nn.sigmoid(gf_ref[...].astype(F32))
    o_ref[...] = (sga * a + sgf * f).astype(o_ref.dtype)


def _merge(attn, z, g, wa, wf, *, tm):
    t, d = attn.shape
    f_dim = z.shape[1]
    return pl.pallas_call(
        _merge_body,
        grid=(t // tm,),
        in_specs=[
            pl.BlockSpec((tm, d), lambda i: (i, 0)),
            pl.BlockSpec((tm, f_dim), lambda i: (i, 0)),
            pl.BlockSpec((tm, d), lambda i: (i, 0)),
            pl.BlockSpec((tm, d), lambda i: (i, 1)),
            _resident((d, d)),
            _resident((f_dim, d)),
        ],
        out_specs=pl.BlockSpec((tm, d), lambda i: (i, 0)),
        out_shape=jax.ShapeDtypeStruct((t, d), BF16),
        compiler_params=_params(1),
        name="merge",
    )(attn, z, g, g, wa, wf)


def _route_body(m_ref, xa_ref, xb_ref, wo_ref, gain_ref, wr_ref, tri_ref,
                hmid_ref, hn_ref, eid_ref, cw_ref, rank_ref, cnt_ref, *, n_a_tiles, n_groups):
    i = pl.program_id(0)
    tm, d = xa_ref.shape
    dh = d // 2
    n_exp = cnt_ref.shape[0]

    @pl.when(i == 0)
    def _init():
        cnt_ref[...] = jnp.zeros_like(cnt_ref)

    x = jnp.where(i < n_a_tiles, xa_ref[...], xb_ref[...])
    h = x + jnp.dot(m_ref[...], wo_ref[...], preferred_element_type=F32)
    hmid_ref[...] = h
    hn = h * lax.rsqrt(jnp.mean(h * h, axis=-1, keepdims=True) + NORM_EPS) * gain_ref[...]
    hb = hn.astype(BF16)
    hn_ref[...] = _pack_bf16_pair(hn[:, :dh], hn[:, dh:])

    lt = lax.dot_general(wr_ref[...], hb, (((1,), (1,)), ((), ())), preferred_element_type=F32)
    row8 = lax.broadcasted_iota(I32, (EXPERTS_PER_GROUP, tm), 0)
    neg = jnp.float32(-jnp.inf)
    lg = jnp.where(row8 < n_groups, lt[0:8], neg)
    gmax = jnp.max(lg, axis=0, keepdims=True)
    gidx = jnp.min(jnp.where(lg == gmax, row8, 8), axis=0, keepdims=True)
    p_g = 1.0 / jnp.sum(jnp.exp(lg - gmax), axis=0, keepdims=True)
    sel = lt[8:8 + EXPERTS_PER_GROUP]
    for g in range(1, n_groups):
        sel = jnp.where(gidx == g, lt[8 + g * EXPERTS_PER_GROUP:8 + (g + 1) * EXPERTS_PER_GROUP], sel)
    m1 = jnp.max(sel, axis=0, keepdims=True)
    i1 = jnp.min(jnp.where(sel == m1, row8, 8), axis=0, keepdims=True)
    sel2 = jnp.where(row8 == i1, neg, sel)
    m2 = jnp.max(sel2, axis=0, keepdims=True)
    i2 = jnp.min(jnp.where(sel2 == m2, row8, 8), axis=0, keepdims=True)
    e21 = jnp.exp(m2 - m1)
    p1 = 1.0 / (1.0 + e21)
    p2 = e21 * p1
    e1 = gidx * EXPERTS_PER_GROUP + i1
    e2 = gidx * EXPERTS_PER_GROUP + i2
    eid_ref[0, 0:1, :] = e1
    eid_ref[0, 1:2, :] = e2
    cw_ref[0, 0:1, :] = p_g * p1
    cw_ref[0, 1:2, :] = p_g * p2

    row_e = lax.broadcasted_iota(I32, (n_exp, tm), 0)
    oh1 = (row_e == e1)
    oh2 = (row_e == e2)
    cs1 = jnp.dot(oh1.astype(BF16), tri_ref[...], preferred_element_type=F32)
    cs2 = jnp.dot(oh2.astype(BF16), tri_ref[...], preferred_element_type=F32)
    oh1f, oh2f = oh1.astype(F32), oh2.astype(F32)
    c1 = jnp.sum(oh1f, axis=1, keepdims=True)
    c2 = jnp.sum(oh2f, axis=1, keepdims=True)
    base = cnt_ref[:, 0:1]
    rank_ref[0, 0:1, :] = jnp.sum(oh1f * (base + cs1), axis=0, keepdims=True).astype(I32)
    rank_ref[0, 1:2, :] = jnp.sum(oh2f * (base + c1 + cs2), axis=0, keepdims=True).astype(I32)
    cnt_ref[...] = cnt_ref[...] + (c1 + c2)


def _outproj_route(merged, xa, xb, wout, gain, wr_t, tri, *, tm, n_groups, n_exp):
    t, d = merged.shape
    nt = t // tm
    na = xa.shape[0] // tm
    tok3 = lambda i: (i, 0, 0)
    body = functools.partial(_route_body, n_a_tiles=na, n_groups=n_groups)
    return pl.pallas_call(
        body,
        grid=(nt,),
        in_specs=[
            pl.BlockSpec((tm, d), lambda i: (i, 0)),
            pl.BlockSpec((tm, d), lambda i: (jnp.minimum(i, na - 1), 0)),
            pl.BlockSpec((tm, d), lambda i: (jnp.maximum(i - na, 0), 0)),
            _resident((d, d)),
            pl.BlockSpec((1, d), lambda i: (0, 0)),
            _resident(wr_t.shape),
            _resident((tm, tm)),
        ],
        out_specs=[
            pl.BlockSpec((tm, d), lambda i: (i, 0)),
            pl.BlockSpec((tm, d // 2), lambda i: (i, 0)),
            pl.BlockSpec((1, TOP_K, tm), tok3),
            pl.BlockSpec((1, TOP_K, tm), tok3),
            pl.BlockSpec((1, TOP_K, tm), tok3),
            pl.BlockSpec((n_exp, HEAD_DIM), lambda i: (0, 0)),
        ],
        out_shape=[
            jax.ShapeDtypeStruct((t, d), F32),
            jax.ShapeDtypeStruct((t, d // 2), U32),
            jax.ShapeDtypeStruct((nt, TOP_K, tm), I32),
            jax.ShapeDtypeStruct((nt, TOP_K, tm), F32),
            jax.ShapeDtypeStruct((nt, TOP_K, tm), I32),
            jax.ShapeDtypeStruct((n_exp, HEAD_DIM), F32),
        ],
        compiler_params=_params(1),
        name="outproj_route",
    )(merged, xa, xb, wout, gain, wr_t, tri)


def _issue_row_copies(idx_ref, n_rows, copy_for, per_trip=None):
    per = ROW_DMA_UNROLL

    def issue(a, carry):
        base = a * per
        parts = 1 if per_trip is None else ISSUE_PARTS
        for part in range(parts):
            if per_trip is not None:
                per_trip(a, part, parts)
            for j in range(part * per // parts, (part + 1) * per // parts):
                for k in range(TOP_K):
                    copy_for(k, a * (per // SUBLANES) + j // SUBLANES, j % SUBLANES,
                             idx_ref[k * n_rows + base + j]).start()
        return carry

    lax.fori_loop(0, n_rows // per, issue, 0)


def _dispatch_body(pad_end_ref, dest_ref, x_ref, xs_ref, zbuf_ref, buf_ref, zsem, ld_sem, sem, *, bm, n_exp):
    i, n_tiles = pl.program_id(0), pl.num_programs(0)
    tg = buf_ref.shape[1]
    tm = tg * SUBLANES

    @pl.when(i == 0)
    def _zero_padding():
        zbuf_ref[...] = jnp.zeros_like(zbuf_ref)

        def last_block_copy(e):
            end = pad_end_ref[e]
            start = pad_end_ref[e - 1] if e > 0 else 0
            cp = pltpu.make_async_copy(zbuf_ref, xs_ref.at[pl.ds(pl.multiple_of(end - bm, bm), bm)], zsem)
            return end > start, cp

        for e in range(n_exp):
            nonempty, cp = last_block_copy(e)
            pl.when(nonempty)(cp.start)
        for e in range(n_exp):
            nonempty, cp = last_block_copy(e)
            pl.when(nonempty)(cp.wait)

    ring = DISPATCH_RING
    slot = i % ring

    def load(tile, dst_slot):
        rows = pl.ds(pl.multiple_of(tile * tg, tg), tg)
        return pltpu.make_async_copy(x_ref.at[rows], buf_ref.at[dst_slot], ld_sem.at[dst_slot])

    def wait_scatters(of_slot):
        for k in range(TOP_K):
            pltpu.make_async_copy(buf_ref.at[of_slot], buf_ref.at[of_slot], sem.at[of_slot]).wait()

    pl.when(i == 0)(load(0, 0).start)
    pl.when(i >= ring - 1)(lambda: wait_scatters((i + 1) % ring))
    pl.when(i + 1 < n_tiles)(load(i + 1, (i + 1) % ring).start)
    load(i, slot).wait()

    def copy_for(k, g, s, d):
        return pltpu.make_async_copy(buf_ref.at[slot, g, pl.ds(s, 1), :], xs_ref.at[pl.ds(d, 1), :], sem.at[slot])

    _issue_row_copies(dest_ref, tm, copy_for)

    @pl.when(i == n_tiles - 1)
    def _drain():
        for back in range(ring - 2, -1, -1):
            pl.when(i >= back)(lambda back=back: wait_scatters((i - back) % ring))


def _dispatch(hn2p, dest, pad_end, *, tm, bm, n_rows):
    t, dh = hn2p.shape
    n_exp = pad_end.shape[0]
    body = functools.partial(_dispatch_body, bm=bm, n_exp=n_exp)
    return pl.pallas_call(
        body,
        grid_spec=pltpu.PrefetchScalarGridSpec(
            num_scalar_prefetch=1,
            grid=(t // tm,),
            in_specs=[
                pl.BlockSpec((TOP_K * tm,), lambda i, pe: (i,), memory_space=pltpu.SMEM),
                pl.BlockSpec(memory_space=pl.ANY),
            ],
            out_specs=pl.BlockSpec(memory_space=pl.ANY),
            scratch_shapes=[pltpu.VMEM((bm, dh), U32),
                            pltpu.VMEM((DISPATCH_RING, tm // SUBLANES, SUBLANES, dh), U32),
                            pltpu.SemaphoreType.DMA(()), pltpu.SemaphoreType.DMA((DISPATCH_RING,)),
                            pltpu.SemaphoreType.DMA((DISPATCH_RING,))],
        ),
        out_shape=jax.ShapeDtypeStruct((n_rows, dh), U32),
        compiler_params=_params(1),
        name="dispatch",
    )(pad_end, dest.reshape(-1), hn2p.reshape(t // SUBLANES, SUBLANES, dh))


def _ffn_body(bexp_ref, nvalid_ref, xs_ref, wg_ref, wu_ref, wd_ref, ys_ref):
    b = pl.program_id(0)

    @pl.when(b < nvalid_ref[0])
    def _():
        dh = xs_ref.shape[1]
        lo, hi = _unpack_bf16_pair(xs_ref[...])
        lo, hi = lo.astype(BF16), hi.astype(BF16)
        g = jnp.dot(lo, wg_ref[0, :dh, :], preferred_element_type=F32)
        g = g + jnp.dot(hi, wg_ref[0, dh:, :], preferred_element_type=F32)
        u = jnp.dot(lo, wu_ref[0, :dh, :], preferred_element_type=F32)
        u = u + jnp.dot(hi, wu_ref[0, dh:, :], preferred_element_type=F32)
        hid = (jax.nn.silu(g) * u).astype(BF16)
        y = jnp.dot(hid, wd_ref[0], preferred_element_type=F32)
        ys_ref[...] = _pack_bf16_pair(y[:, :dh], y[:, dh:])


def _expert_ffn(xs, wg, wu, wd, block_expert, n_valid, *, bm):
    n_rows, dh = xs.shape
    n_exp, d, de = wg.shape
    nb = n_rows // bm
    blk = lambda b, be, nv: (jnp.minimum(b, nv[0] - 1), 0)
    wmap = lambda b, be, nv: (be[jnp.minimum(b, nv[0] - 1)], 0, 0)
    return pl.pallas_call(
        _ffn_body,
        grid_spec=pltpu.PrefetchScalarGridSpec(
            num_scalar_prefetch=2,
            grid=(nb,),
            in_specs=[
                pl.BlockSpec((bm, dh), blk),
                pl.BlockSpec((1, d, de), wmap),
                pl.BlockSpec((1, d, de), wmap),
                pl.BlockSpec((1, de, d), wmap),
            ],
            out_specs=pl.BlockSpec((bm, dh), blk),
        ),
        out_shape=jax.ShapeDtypeStruct((n_rows, dh), U32),
        compiler_params=_params(1),
        name="expert_ffn",
    )(block_expert, n_valid, xs, wg, wu, wd)


def _combine_body(dest_ref, dest_next_ref, h_ref, cw_ref, gain_ref, ys_ref, o_ref, buf_ref, sem):
    i = pl.program_id(0)
    last = pl.num_programs(0) - 1
    tm, d = h_ref.shape
    dh = d // 2
    slot = i % 2
    per = ROW_DMA_UNROLL

    def gather_into(dst_slot):
        def copy_for(k, g, s, row):
            return pltpu.make_async_copy(ys_ref.at[pl.ds(row, 1), :], buf_ref.at[dst_slot, k, g, pl.ds(s, 1), :],
                                         sem.at[dst_slot])
        return copy_for

    @pl.when(i == 0)
    def _first_tile():
        _issue_row_copies(dest_ref, tm, gather_into(0))

    for k in range(TOP_K):
        pltpu.make_async_copy(buf_ref.at[slot, k], buf_ref.at[slot, k], sem.at[slot]).wait()

    def finish_rows(a, part=0, parts=1):
        groups = per // SUBLANES // parts
        for g in range(part * groups, (part + 1) * groups):
            grp = a * (per // SUBLANES) + g
            rows = pl.ds(pl.multiple_of(grp * SUBLANES, SUBLANES), SUBLANES)
            lo1, hi1 = _unpack_bf16_pair(buf_ref[slot, 0, grp])
            lo2, hi2 = _unpack_bf16_pair(buf_ref[slot, 1, grp])
            w1 = cw_ref[rows, 0:1]
            w2 = cw_ref[rows, 1:2]
            o_lo = h_ref[rows, :dh] + (w1 * lo1 + w2 * lo2)
            o_hi = h_ref[rows, dh:] + (w1 * hi1 + w2 * hi2)
            ssq = jnp.sum(o_lo * o_lo, axis=-1, keepdims=True) + jnp.sum(o_hi * o_hi, axis=-1, keepdims=True)
            r = lax.rsqrt(ssq / d + NORM_EPS)
            o_ref[rows, :dh] = o_lo * r * gain_ref[:, :dh]
            o_ref[rows, dh:] = o_hi * r * gain_ref[:, dh:]

    @pl.when(i < last)
    def _overlapped():
        _issue_row_copies(dest_next_ref, tm, gather_into(1 - slot), per_trip=finish_rows)

    @pl.when(i == last)
    def _tail():
        def trip(a, carry):
            finish_rows(a)
            return carry
        lax.fori_loop(0, tm // per, trip, 0)


def _combine(ys, dest, hmid, cw_t, gain, *, tm, row_off, n_rows):
    d = hmid.shape[1]
    dh = d // 2
    off, nt = row_off // tm, n_rows // tm
    dest_flat = dest.reshape(-1)
    return pl.pallas_call(
        _combine_body,
        grid=(nt,),
        in_specs=[
            pl.BlockSpec((TOP_K * tm,), lambda i: (off + i,), memory_space=pltpu.SMEM),
            pl.BlockSpec((TOP_K * tm,), lambda i: (off + jnp.minimum(i + 1, nt - 1),), memory_space=pltpu.SMEM),
            pl.BlockSpec((tm, d), lambda i: (off + i, 0)),
            pl.BlockSpec((tm, TOP_K), lambda i: (off + i, 0)),
            pl.BlockSpec((1, d), lambda i: (0, 0)),
            pl.BlockSpec(memory_space=pl.ANY),
        ],
        out_specs=pl.BlockSpec((tm, d), lambda i: (i, 0)),
        out_shape=jax.ShapeDtypeStruct((n_rows, d), F32),
        scratch_shapes=[pltpu.VMEM((2, TOP_K, tm // SUBLANES, SUBLANES, dh), U32), pltpu.SemaphoreType.DMA((2,))],
        compiler_params=_params(1),
        name="combine",
    )(dest_flat, dest_flat, hmid, cw_t, gain, ys)


def _rope_tables(n_tokens):
    rows = n_tokens // GRID_W
    row = jnp.repeat(jnp.arange(rows, dtype=F32), GRID_W)
    col = jnp.tile(jnp.arange(GRID_W, dtype=F32), rows)
    inv_freq = ROPE_THETA ** (-jnp.arange(ROPE_HALF, dtype=F32) / ROPE_HALF)
    ar, ac = row[:, None] * inv_freq, col[:, None] * inv_freq
    cos = jnp.concatenate([jnp.cos(ar), jnp.cos(ac), jnp.cos(ar), jnp.cos(ac)], axis=1)
    sin = jnp.concatenate([-jnp.sin(ar), -jnp.sin(ac), jnp.sin(ar), jnp.sin(ac)], axis=1)
    return cos, sin


def _tile(n, pref):
    return min(n, pref)


def kernel(x_prompt, x_sample, meta_tokens, mix_norm, w_in, q_gain, k_gain, w_attn_o, w_fourier_o, w_out,
           moe_norm, w_router_group, w_router_expert, w_expert_gate, w_expert_up, w_expert_down, final_norm):
    bp, n_p, d = x_prompt.shape
    bs, n_s, _ = x_sample.shape
    n_meta = meta_tokens.shape[0]
    t_p, t_s = bp * n_p, bs * n_s
    t = t_p + t_s
    f_dim = d // 2
    cg = f_dim // N_FOURIER_GROUPS
    n_groups = w_router_group.shape[-1]
    n_exp = w_router_expert.shape[-1]
    assert n_exp == n_groups * EXPERTS_PER_GROUP and n_groups <= 8

    tm = _tile(math.gcd(n_p, n_s), TOKEN_TILE)
    bm = _tile(t, EXPERT_BLOCK_ROWS)

    q_lo, v_lo = f_dim, f_dim + d + d // 4
    w0 = w_in[0]
    w_in_b = jnp.concatenate([w0[:, :q_lo], _pair_major(w0[:, q_lo:v_lo]), w0[:, v_lo:]], axis=1).astype(BF16)
    wa_b, wf_b, wo_b = w_attn_o[0].astype(BF16), w_fourier_o[0].astype(BF16), w_out[0].astype(BF16)
    qg = (_pair_major(q_gain[0]) * (HEAD_DIM ** -0.5 * LOG2E)).reshape(1, HEAD_DIM)
    kg = _pair_major(k_gain[0]).reshape(1, HEAD_DIM)
    wr_t = jnp.zeros((8 + n_exp, d), F32)
    wr_t = wr_t.at[:n_groups].set(w_router_group[0].T).at[8:].set(w_router_expert[0].T).astype(BF16)
    tri = (jnp.arange(tm)[:, None] < jnp.arange(tm)[None, :]).astype(BF16)
    cos, sin = _rope_tables(max(n_p, n_s))
    cdc, cds = _chan_dft(cg)
    gain_in = mix_norm[0].reshape(1, d)

    xp, xs_in = x_prompt.reshape(t_p, d), x_sample.reshape(t_s, d)
    tm_in = _tile(math.gcd(t_p, t_s), INPROJ_ROWS)
    proj = _inproj(xp, gain_in, w_in_b, None, tm=tm_in, row_off=0, t_total=t)
    uf, q, k, v, g = _inproj(xs_in, gain_in, w_in_b, proj, tm=tm_in, row_off=t_p, t_total=t)
    ufm, _, km, vm, _ = _inproj(meta_tokens, gain_in, w_in_b, None, tm=n_meta, row_off=0, t_total=n_meta)

    kinds = ((0, bp, n_p), (t_p, bs, n_s))
    n_kv = k.shape[1] // HEAD_DIM
    steps = [n_seq * n_kv * (n_tok // _tile(n_tok, ATTN_Q_TILE)) for _, n_seq, n_tok in kinds]
    w_exp = [w_expert_gate[0], w_expert_up[0], w_expert_down[0]]
    w_flat = [w.reshape(-1, w.shape[-1]) for w in w_exp]
    fused_cast = all(w.shape[0] % (SUBLANES * sum(steps)) == 0 for w in w_flat)
    cast_rows = [w.shape[0] // sum(steps) for w in w_flat] if fused_cast else []

    attn, z, w_b = None, None, None
    for (row_off, n_seq, n_tok), first_step in zip(kinds, (0, steps[0])):
        res = _attention(q, k, v, km, vm, qg, kg, cos, sin, attn, row_off=row_off, n_seq=n_seq,
                         n_tok=n_tok, tq=_tile(n_tok, ATTN_Q_TILE),
                         cast_src=w_flat if fused_cast else (), cast_prev=w_b, cast_rows=cast_rows,
                         cast_row_off=[first_step * r for r in cast_rows])
        attn, w_b = res[0], (res[1:] if fused_cast else None)
        z = _fourier(uf, ufm, _seq_dft(n_tok, n_meta), cdc, cds, z, row_off=row_off, n_seq=n_seq,
                     n_tok=n_tok, tl=_tile(n_tok, max(TOKEN_TILE, FOURIER_TILE_ELEMS // n_tok)),
                     tf=_tile(f_dim, FOURIER_COLS))
    if fused_cast:
        wg_b, wu_b, wd_b = [wb.reshape(w.shape) for wb, w in zip(w_b, w_exp)]
    else:
        wg_b, wu_b, wd_b = [w.astype(BF16) for w in w_exp]

    merged = _merge(attn, z, g, wa_b, wf_b, tm=tm)
    hmid, hn2p, eid, cw, rank, counts = _outproj_route(
        merged, xp, xs_in, wo_b, moe_norm[0].reshape(1, d), wr_t, tri, tm=tm, n_groups=n_groups, n_exp=n_exp)

    cnt = counts[:, 0].astype(I32)
    padded = (cnt + bm - 1) // bm * bm
    pad_end = jnp.cumsum(padded).astype(I32)
    pad_start = pad_end - padded
    onehot = eid[..., None] == jnp.arange(n_exp, dtype=I32)
    dest = jnp.sum(jnp.where(onehot, pad_start, 0), axis=-1) + rank
    nb = (t * TOP_K) // bm + n_exp
    n_valid = (pad_end[-1:] // bm).astype(I32)
    blk_start = jnp.arange(nb, dtype=I32) * bm
    block_expert = jnp.minimum(jnp.sum(pad_end[None, :] <= blk_start[:, None], axis=1), n_exp - 1).astype(I32)

    xs = _dispatch(hn2p, dest, pad_end, tm=tm, bm=bm, n_rows=nb * bm)
    ys = _expert_ffn(xs, wg_b, wu_b, wd_b, block_expert, n_valid, bm=bm)
    cw_t = jnp.transpose(cw, (0, 2, 1)).reshape(t, TOP_K)
    fgain = final_norm.reshape(1, d)
    y_p = _combine(ys, dest, hmid, cw_t, fgain, tm=tm, row_off=0, n_rows=t_p)
    y_s = _combine(ys, dest, hmid, cw_t, fgain, tm=tm, row_off=t_p, n_rows=t_s)
    return y_p.reshape(bp, n_p, d), y_s.reshape(bs, n_s, d)
```

```python
import functools
import math

import jax
import jax.numpy as jnp
from jax import lax
from jax.experimental import pallas as pl
from jax.experimental.pallas import tpu as pltpu

F32, BF16, I32, U32 = jnp.float32, jnp.bfloat16, jnp.int32, jnp.uint32

GRID_W = 64
HEAD_DIM = 128
ROPE_HALF = 32
Q_PER_KV = 4
N_FOURIER_GROUPS = 4
ROPE_THETA = 10000.0
NORM_EPS = 1e-6
TOP_K = 2
EXPERTS_PER_GROUP = 8
LOG2E = 1.4426950408889634
ATTN_KV_CHUNK = 512
ATTN_UNIT_ROWS = 256
ATTN_Q_TILE = 512
TOKEN_TILE = 512
EXPERT_BLOCK_ROWS = 512
FOURIER_COLS = 512
INPROJ_ROWS = 1024
FOURIER_TILE_ELEMS = 2 * 1024 * 1024
SUBLANES = 8
ROW_DMA_UNROLL = 32
ISSUE_PARTS = 2
DISPATCH_RING = 3

VMEM_LIMIT_BYTES = 56 * 1024 * 1024


def _params(n_axes):
    return pltpu.CompilerParams(dimension_semantics=("arbitrary",) * n_axes,
                                vmem_limit_bytes=VMEM_LIMIT_BYTES)


def _resident(shape):
    return pl.BlockSpec(shape, lambda *_: (0,) * len(shape), pipeline_mode=pl.Buffered(1))


def _pack_bf16_pair(lo, hi):
    lo_bits = lax.bitcast_convert_type(lo.astype(BF16).astype(F32), U32) >> 16
    hi_bits = lax.bitcast_convert_type(hi.astype(BF16).astype(F32), U32) & jnp.uint32(0xFFFF0000)
    return lo_bits | hi_bits


def _unpack_bf16_pair(w):
    lo = lax.bitcast_convert_type(w << 16, F32)
    hi = lax.bitcast_convert_type(w & jnp.uint32(0xFFFF0000), F32)
    return lo, hi


_F_TILES, _Q_TILES, _G_TILES = 2, 4, 8
_N_TILES = _F_TILES + _Q_TILES + 2 + _G_TILES


def _inproj_body(x_ref, gain_ref, w_ref, *rest):
    uf_ref, q_ref, k_ref, v_ref, g_ref, hn_ref, xbuf_ref, xsem = rest[-8:]
    i, n = pl.program_id(0), pl.program_id(1)
    tm = hn_ref.shape[0]
    slot = i % 2

    def x_copy(tile, dst_slot):
        rows = pl.ds(pl.multiple_of(tile * tm, tm), tm)
        return pltpu.make_async_copy(x_ref.at[rows], xbuf_ref.at[dst_slot], xsem.at[dst_slot])

    @pl.when(n == 0)
    def _norm():
        pl.when(i == 0)(x_copy(0, 0).start)
        x_copy(i, slot).wait()
        pl.when(i + 1 < pl.num_programs(0))(x_copy(i + 1, 1 - slot).start)
        x = xbuf_ref[slot]
        y = x * lax.rsqrt(jnp.mean(x * x, axis=-1, keepdims=True) + NORM_EPS)
        hn_ref[...] = (y * gain_ref[...]).astype(BF16)

    q0 = _F_TILES
    k0 = q0 + _Q_TILES

    def project_into(out_ref):
        def branch():
            acc = jnp.dot(hn_ref[...], w_ref[...], preferred_element_type=F32)
            out_ref[...] = acc.astype(out_ref.dtype)
        return branch

    pl.when(n < q0)(project_into(uf_ref))
    pl.when((n >= q0) & (n < k0))(project_into(q_ref))
    pl.when(n == k0)(project_into(k_ref))
    pl.when(n == k0 + 1)(project_into(v_ref))
    pl.when(n > k0 + 1)(project_into(g_ref))


def _inproj(x, gain, w_in, prev, *, tm, row_off, t_total):
    tx, d = x.shape
    tn = d // 4
    f_dim, q_dim, kv_dim = d // 2, d, d // 4
    assert w_in.shape == (d, _N_TILES * tn) and tx % tm == 0 and row_off % tm == 0
    ob = row_off // tm
    q0, k0, g0 = _F_TILES, _F_TILES + _Q_TILES, _F_TILES + _Q_TILES + 2
    in_specs = [
        pl.BlockSpec(memory_space=pl.ANY),
        pl.BlockSpec((1, d), lambda i, n: (0, 0)),
        pl.BlockSpec((d, tn), lambda i, n: (0, n)),
    ]
    args = [x, gain, w_in]
    aliases = {}
    if prev is not None:
        in_specs += [pl.BlockSpec(memory_space=pl.ANY)] * len(prev)
        aliases = {len(args) + j: j for j in range(len(prev))}
        args += list(prev)
    return pl.pallas_call(
        _inproj_body,
        grid=(tx // tm, _N_TILES),
        in_specs=in_specs,
        out_specs=[
            pl.BlockSpec((tm, tn), lambda i, n: (ob + i, jnp.clip(n, 0, _F_TILES - 1))),
            pl.BlockSpec((tm, tn), lambda i, n: (ob + i, jnp.clip(n - q0, 0, _Q_TILES - 1))),
            pl.BlockSpec((tm, tn), lambda i, n: (ob + i, 0)),
            pl.BlockSpec((tm, tn), lambda i, n: (ob + i, 0)),
            pl.BlockSpec((tm, tn), lambda i, n: (ob + i, jnp.clip(n - g0, 0, _G_TILES - 1))),
        ],
        out_shape=[
            jax.ShapeDtypeStruct((t_total, f_dim), BF16),
            jax.ShapeDtypeStruct((t_total, q_dim), F32),
            jax.ShapeDtypeStruct((t_total, kv_dim), F32),
            jax.ShapeDtypeStruct((t_total, kv_dim), BF16),
            jax.ShapeDtypeStruct((t_total, 2 * d), BF16),
        ],
        scratch_shapes=[pltpu.VMEM((tm, d), BF16), pltpu.VMEM((2, tm, d), F32), pltpu.SemaphoreType.DMA((2,))],
        input_output_aliases=aliases,
        compiler_params=_params(2),
        name="inproj",
    )(*args)


def _head_norm(a, gain):
    return a * lax.rsqrt(jnp.mean(a * a, axis=-1, keepdims=True) + NORM_EPS) * gain


def _pair_major(a):
    lead = a.shape[:-1]
    a = a.reshape(lead + (a.shape[-1] // HEAD_DIM, 2, 2, ROPE_HALF))
    return jnp.swapaxes(a, -3, -2).reshape(lead + (-1,))


def _rope(y, cos, sin):
    return y * cos + pltpu.roll(y, HEAD_DIM // 2, 1) * sin


def _attn_body(q_ref, k_ref, v_ref, km_ref, vm_ref, qg_ref, kg_ref, cq_ref, sq_ref, ck_ref, sk_ref, *rest,
               n_cast):
    cast_in = rest[:n_cast]
    kb_ref, kmb_ref, ve_ref, vme_ref, s_ref = rest[-5:]
    o_ref = rest[-6 - n_cast]
    cast_out = rest[len(rest) - 5 - n_cast:len(rest) - 5]
    tq, n_tok = q_ref.shape[0], k_ref.shape[0]

    for w_ref, wb_ref in zip(cast_in, cast_out):
        wb_ref[...] = w_ref[...].astype(BF16)

    tk = min(n_tok, ATTN_KV_CHUNK)
    n_chunks, lanes = n_tok // tk, tk // HEAD_DIM
    nt = (((1,), (1,)), ((), ()))

    @pl.when(pl.program_id(2) == 0)
    def _prepare_keys():
        kb_ref[...] = _rope(_head_norm(k_ref[...], kg_ref[...]), ck_ref[...], sk_ref[...]).astype(BF16)
        kmb_ref[...] = _head_norm(km_ref[...], kg_ref[...]).astype(BF16)
        ve_ref[:, :HEAD_DIM] = v_ref[...]
        ve_ref[:, HEAD_DIM:] = jnp.ones((n_tok, HEAD_DIM), BF16)
        vme_ref[:, :HEAD_DIM] = vm_ref[...]
        vme_ref[:, HEAD_DIM:] = jnp.ones((vm_ref.shape[0], HEAD_DIM), BF16)

    qg = qg_ref[...]
    ru = min(tq, ATTN_UNIT_ROWS)
    units = [(r0, h) for r0 in range(0, tq, ru) for h in range(Q_PER_KV)]

    def scores(u):
        r0, h = units[u]
        rows = slice(r0, r0 + ru)
        qh = _rope(_head_norm(q_ref[rows, h * HEAD_DIM:(h + 1) * HEAD_DIM], qg), cq_ref[rows, :], sq_ref[rows, :])
        qh = qh.astype(BF16)
        mp = None
        for c in range(n_chunks):
            s = lax.dot_general(qh, kb_ref[c * tk:(c + 1) * tk, :], nt, preferred_element_type=F32)
            s_ref[u % 2, :, c * tk:(c + 1) * tk] = s
            for j in range(lanes):
                t = s[:, j * HEAD_DIM:(j + 1) * HEAD_DIM]
                mp = t if mp is None else jnp.maximum(mp, t)
            yield None
        sm = lax.dot_general(qh, kmb_ref[...], nt, preferred_element_type=F32)
        m = jnp.maximum(jnp.max(mp, axis=-1, keepdims=True), jnp.max(sm, axis=-1, keepdims=True))
        yield m, sm

    def outputs(u, m, sm):
        r0, h = units[u]
        mb = jnp.broadcast_to(m, (ru, HEAD_DIM))
        pm = jnp.exp2(sm - m)
        acc = jnp.dot(pm.astype(BF16), vme_ref[...], preferred_element_type=F32)
        for c in range(n_chunks):
            tiles = []
            for j in range(lanes):
                col = c * tk + j * HEAD_DIM
                tiles.append(jnp.exp2(s_ref[u % 2, :, col:col + HEAD_DIM] - mb).astype(BF16))
            acc = acc + jnp.dot(jnp.concatenate(tiles, axis=1), ve_ref[c * tk:(c + 1) * tk, :],
                                preferred_element_type=F32)
            yield None
        o = acc[:, :HEAD_DIM] / acc[:, HEAD_DIM:]
        o_ref[r0:r0 + ru, h * HEAD_DIM:(h + 1) * HEAD_DIM] = o.astype(o_ref.dtype)
        yield None

    prev_out = None
    for u in range(len(units) + 1):
        cur = scores(u) if u < len(units) else None
        stats = None
        for _ in range(n_chunks + 1):
            if cur is not None:
                stats = next(cur)
            if prev_out is not None:
                next(prev_out)
        prev_out = outputs(u, *stats) if cur is not None else None


def _attention(q, k, v, km, vm, qg, kg, cos, sin, prev, *, row_off, n_seq, n_tok, tq,
               cast_src=(), cast_prev=None, cast_rows=(), cast_row_off=()):
    t, q_dim = q.shape
    n_kv = k.shape[1] // HEAD_DIM
    n_meta = km.shape[0]
    gw = Q_PER_KV * HEAD_DIM
    assert row_off % n_tok == 0 and n_tok % tq == 0
    qmap = lambda b, kh, qi: ((row_off + b * n_tok) // tq + qi, kh)
    kmap = lambda b, kh, qi: (row_off // n_tok + b, kh)
    mmap = lambda b, kh, qi: (0, kh)
    const = lambda b, kh, qi: (0, 0)
    qpos = lambda b, kh, qi: (qi, 0)
    in_specs = [
        pl.BlockSpec((tq, gw), qmap),
        pl.BlockSpec((n_tok, HEAD_DIM), kmap),
        pl.BlockSpec((n_tok, HEAD_DIM), kmap),
        pl.BlockSpec((n_meta, HEAD_DIM), mmap),
        pl.BlockSpec((n_meta, HEAD_DIM), mmap),
        pl.BlockSpec((1, HEAD_DIM), const),
        pl.BlockSpec((1, HEAD_DIM), const),
        pl.BlockSpec((tq, HEAD_DIM), qpos),
        pl.BlockSpec((tq, HEAD_DIM), qpos),
        pl.BlockSpec((n_tok, HEAD_DIM), const),
        pl.BlockSpec((n_tok, HEAD_DIM), const),
    ]
    args = [q, k, v, km, vm, qg, kg, cos, sin, cos, sin]
    nq = n_tok // tq
    out_specs = [pl.BlockSpec((tq, gw), qmap)]
    out_shape = [jax.ShapeDtypeStruct((t, q_dim), BF16)]
    for w, rows, off in zip(cast_src, cast_rows, cast_row_off):
        assert off % rows == 0 and rows % SUBLANES == 0
        wmap = lambda b, kh, qi, rows=rows, off=off: (off // rows + (b * n_kv + kh) * nq + qi, 0)
        in_specs.append(pl.BlockSpec((rows, w.shape[1]), wmap))
        args.append(w)
        out_specs.append(pl.BlockSpec((rows, w.shape[1]), wmap))
        out_shape.append(jax.ShapeDtypeStruct(w.shape, BF16))
    aliases = {}
    if prev is not None:
        in_specs.append(pl.BlockSpec(memory_space=pl.ANY))
        args.append(prev)
        aliases[len(args) - 1] = 0
    for j, wprev in enumerate(cast_prev or ()):
        in_specs.append(pl.BlockSpec(memory_space=pl.ANY))
        args.append(wprev)
        aliases[len(args) - 1] = 1 + j
    return pl.pallas_call(
        functools.partial(_attn_body, n_cast=len(cast_src)),
        grid=(n_seq, n_kv, nq),
        in_specs=in_specs,
        out_specs=out_specs,
        out_shape=out_shape,
        scratch_shapes=[
            pltpu.VMEM((n_tok, HEAD_DIM), BF16),
            pltpu.VMEM((n_meta, HEAD_DIM), BF16),
            pltpu.VMEM((n_tok, 2 * HEAD_DIM), BF16),
            pltpu.VMEM((n_meta, 2 * HEAD_DIM), BF16),
            pltpu.VMEM((2, min(tq, ATTN_UNIT_ROWS), n_tok), F32),
        ],
        input_output_aliases=aliases,
        compiler_params=_params(3),
        name="attention",
    )(*args)


def _dftgen_body(tac_ref, tas_ref, tbc_ref, tbs_ref, c_ref, ns_ref):
    ac, asn = tac_ref[0], tas_ref[0]
    bc, bsn = tbc_ref[...], tbs_ref[...]
    c_ref[...] = (ac * bc - asn * bsn).astype(BF16)
    ns_ref[...] = (-(asn * bc + ac * bsn)).astype(BF16)


def _seq_dft(n_tok, n_meta):
    length = n_tok + n_meta
    rb = min(n_tok, HEAD_DIM)
    w = 2.0 * math.pi / length
    scale = 1.0 / math.sqrt(length)
    p_real = n_meta + jnp.arange(n_tok, dtype=I32)

    def angles(p_rows, p_cols):
        return ((p_rows[:, None] * p_cols[None, :]) % length).astype(F32) * w

    ang_a = angles(n_meta + rb * jnp.arange(n_tok // rb, dtype=I32), p_real)
    ang_b = angles(jnp.arange(rb, dtype=I32), p_real)
    ang_m = angles(p_real, jnp.arange(n_meta, dtype=I32))
    tac = (jnp.cos(ang_a) * scale).reshape(n_tok // rb, 1, n_tok)
    tas = (jnp.sin(ang_a) * scale).reshape(n_tok // rb, 1, n_tok)
    blk = pl.BlockSpec((1, 1, n_tok), lambda a: (a, 0, 0))
    full = pl.BlockSpec((rb, n_tok), lambda a: (0, 0))
    cmat, nsmat = pl.pallas_call(
        _dftgen_body,
        grid=(n_tok // rb,),
        in_specs=[blk, blk, full, full],
        out_specs=[pl.BlockSpec((rb, n_tok), lambda a: (a, 0))] * 2,
        out_shape=[jax.ShapeDtypeStruct((n_tok, n_tok), BF16)] * 2,
        compiler_params=_params(1),
        name="dft_matrices",
    )(tac, tas, jnp.cos(ang_b), jnp.sin(ang_b))
    return cmat, nsmat, (jnp.cos(ang_m) * scale).astype(BF16), (-jnp.sin(ang_m) * scale).astype(BF16)


def _chan_dft(cg):
    idx = jnp.arange(cg, dtype=I32)
    ang = ((idx[:, None] * idx[None, :]) % cg).astype(F32) * (2.0 * math.pi / cg)
    scale = 1.0 / math.sqrt(cg)
    return (jnp.cos(ang) * scale).astype(BF16), (jnp.sin(ang) * scale).astype(BF16)


def _fourier_body(c_ref, ns_ref, cm_ref, nsm_ref, u_ref, um_ref, cdc_ref, cds_ref, *rest):
    z_ref = rest[-1]
    u, um = u_ref[...], um_ref[...]
    a = jnp.dot(c_ref[...], u, preferred_element_type=F32) + jnp.dot(cm_ref[...], um, preferred_element_type=F32)
    b = jnp.dot(ns_ref[...], u, preferred_element_type=F32) + jnp.dot(nsm_ref[...], um, preferred_element_type=F32)
    a, b = a.astype(BF16), b.astype(BF16)
    cg = cdc_ref.shape[0]
    for g in range(z_ref.shape[1] // cg):
        sl = slice(g * cg, (g + 1) * cg)
        z = jnp.dot(a[:, sl], cdc_ref[...], preferred_element_type=F32)
        z = z + jnp.dot(b[:, sl], cds_ref[...], preferred_element_type=F32)
        z_ref[:, sl] = z.astype(z_ref.dtype)


def _fourier(uf, ufm, mats, cdc, cds, prev, *, row_off, n_seq, n_tok, tl, tf):
    cmat, nsmat, cmeta, nsmeta = mats
    t, f_dim = uf.shape
    n_meta = ufm.shape[0]
    cg = cdc.shape[0]
    assert row_off % n_tok == 0 and n_tok % tl == 0 and f_dim % tf == 0 and tf % cg == 0
    cmap = lambda b, j, i: (i, 0)
    const = lambda b, j, i: (0, 0)
    in_specs = [
        pl.BlockSpec((tl, n_tok), cmap),
        pl.BlockSpec((tl, n_tok), cmap),
        pl.BlockSpec((tl, n_meta), cmap),
        pl.BlockSpec((tl, n_meta), cmap),
        pl.BlockSpec((n_tok, tf), lambda b, j, i: (row_off // n_tok + b, j)),
        pl.BlockSpec((n_meta, tf), lambda b, j, i: (0, j)),
        pl.BlockSpec((cg, cg), const),
        pl.BlockSpec((cg, cg), const),
    ]
    args = [cmat, nsmat, cmeta, nsmeta, uf, ufm, cdc, cds]
    aliases = {}
    if prev is not None:
        in_specs.append(pl.BlockSpec(memory_space=pl.ANY))
        args.append(prev)
        aliases = {len(args) - 1: 0}
    return pl.pallas_call(
        _fourier_body,
        grid=(n_seq, f_dim // tf, n_tok // tl),
        in_specs=in_specs,
        out_specs=pl.BlockSpec((tl, tf), lambda b, j, i: ((row_off + b * n_tok) // tl + i, j)),
        out_shape=jax.ShapeDtypeStruct((t, f_dim), BF16),
        input_output_aliases=aliases,
        compiler_params=_params(3),
        name="fourier",
    )(*args)


def _merge_body(a_ref, z_ref, ga_ref, gf_ref, wa_ref, wf_ref, o_ref):
    a = jnp.dot(a_ref[...], wa_ref[...], preferred_element_type=F32)
    f = jnp.dot(z_ref[...], wf_ref[...], preferred_element_type=F32)
    sga = jax.nn.sigmoid(ga_ref[...].astype(F32))
    sgf = jax.nn.sigmoid(gf_ref[...].astype(F32))
    o_ref[...] = (sga * a + sgf * f).astype(o_ref.dtype)


def _merge(attn, z, g, wa, wf, *, tm):
    t, d = attn.shape
    f_dim = z.shape[1]
    return pl.pallas_call(
        _merge_body,
        grid=(t // tm,),
        in_specs=[
            pl.BlockSpec((tm, d), lambda i: (i, 0)),
            pl.BlockSpec((tm, f_dim), lambda i: (i, 0)),
            pl.BlockSpec((tm, d), lambda i: (i, 0)),
            pl.BlockSpec((tm, d), lambda i: (i, 1)),
            _resident((d, d)),
            _resident((f_dim, d)),
        ],
        out_specs=pl.BlockSpec((tm, d), lambda i: (i, 0)),
        out_shape=jax.ShapeDtypeStruct((t, d), BF16),
        compiler_params=_params(1),
        name="merge",
    )(attn, z, g, g, wa, wf)


def _route_body(m_ref, xa_ref, xb_ref, wo_ref, gain_ref, wr_ref, tri_ref,
                hmid_ref, hn_ref, eid_ref, cw_ref, rank_ref, cnt_ref, *, n_a_tiles, n_groups):
    i = pl.program_id(0)
    tm, d = xa_ref.shape
    dh = d // 2
    n_exp = cnt_ref.shape[0]

    @pl.when(i == 0)
    def _init():
        cnt_ref[...] = jnp.zeros_like(cnt_ref)

    x = jnp.where(i < n_a_tiles, xa_ref[...], xb_ref[...])
    h = x + jnp.dot(m_ref[...], wo_ref[...], preferred_element_type=F32)
    hmid_ref[...] = h
    hn = h * lax.rsqrt(jnp.mean(h * h, axis=-1, keepdims=True) + NORM_EPS) * gain_ref[...]
    hb = hn.astype(BF16)
    hn_ref[...] = _pack_bf16_pair(hn[:, :dh], hn[:, dh:])

    lt = lax.dot_general(wr_ref[...], hb, (((1,), (1,)), ((), ())), preferred_element_type=F32)
    row8 = lax.broadcasted_iota(I32, (EXPERTS_PER_GROUP, tm), 0)
    neg = jnp.float32(-jnp.inf)
    lg = jnp.where(row8 < n_groups, lt[0:8], neg)
    gmax = jnp.max(lg, axis=0, keepdims=True)
    gidx = jnp.min(jnp.where(lg == gmax, row8, 8), axis=0, keepdims=True)
    p_g = 1.0 / jnp.sum(jnp.exp(lg - gmax), axis=0, keepdims=True)
    sel = lt[8:8 + EXPERTS_PER_GROUP]
    for g in range(1, n_groups):
        sel = jnp.where(gidx == g, lt[8 + g * EXPERTS_PER_GROUP:8 + (g + 1) * EXPERTS_PER_GROUP], sel)
    m1 = jnp.max(sel, axis=0, keepdims=True)
    i1 = jnp.min(jnp.where(sel == m1, row8, 8), axis=0, keepdims=True)
    sel2 = jnp.where(row8 == i1, neg, sel)
    m2 = jnp.max(sel2, axis=0, keepdims=True)
    i2 = jnp.min(jnp.where(sel2 == m2, row8, 8), axis=0, keepdims=True)
    e21 = jnp.exp(m2 - m1)
    p1 = 1.0 / (1.0 + e21)
    p2 = e21 * p1
    e1 = gidx * EXPERTS_PER_GROUP + i1
    e2 = gidx * EXPERTS_PER_GROUP + i2
    eid_ref[0, 0:1, :] = e1
    eid_ref[0, 1:2, :] = e2
    cw_ref[0, 0:1, :] = p_g * p1
    cw_ref[0, 1:2, :] = p_g * p2

    row_e = lax.broadcasted_iota(I32, (n_exp, tm), 0)
    oh1 = (row_e == e1)
    oh2 = (row_e == e2)
    cs1 = jnp.dot(oh1.astype(BF16), tri_ref[...], preferred_element_type=F32)
    cs2 = jnp.dot(oh2.astype(BF16), tri_ref[...], preferred_element_type=F32)
    oh1f, oh2f = oh1.astype(F32), oh2.astype(F32)
    c1 = jnp.sum(oh1f, axis=1, keepdims=True)
    c2 = jnp.sum(oh2f, axis=1, keepdims=True)
    base = cnt_ref[:, 0:1]
    rank_ref[0, 0:1, :] = jnp.sum(oh1f * (base + cs1), axis=0, keepdims=True).astype(I32)
    rank_ref[0, 1:2, :] = jnp.sum(oh2f * (base + c1 + cs2), axis=0, keepdims=True).astype(I32)
    cnt_ref[...] = cnt_ref[...] + (c1 + c2)


def _outproj_route(merged, xa, xb, wout, gain, wr_t, tri, *, tm, n_groups, n_exp):
    t, d = merged.shape
    nt = t // tm
    na = xa.shape[0] // tm
    tok3 = lambda i: (i, 0, 0)
    body = functools.partial(_route_body, n_a_tiles=na, n_groups=n_groups)
    return pl.pallas_call(
        body,
        grid=(nt,),
        in_specs=[
            pl.BlockSpec((tm, d), lambda i: (i, 0)),
            pl.BlockSpec((tm, d), lambda i: (jnp.minimum(i, na - 1), 0)),
            pl.BlockSpec((tm, d), lambda i: (jnp.maximum(i - na, 0), 0)),
            _resident((d, d)),
            pl.BlockSpec((1, d), lambda i: (0, 0)),
            _resident(wr_t.shape),
            _resident((tm, tm)),
        ],
        out_specs=[
            pl.BlockSpec((tm, d), lambda i: (i, 0)),
            pl.BlockSpec((tm, d // 2), lambda i: (i, 0)),
            pl.BlockSpec((1, TOP_K, tm), tok3),
            pl.BlockSpec((1, TOP_K, tm), tok3),
            pl.BlockSpec((1, TOP_K, tm), tok3),
            pl.BlockSpec((n_exp, HEAD_DIM), lambda i: (0, 0)),
        ],
        out_shape=[
            jax.ShapeDtypeStruct((t, d), F32),
            jax.ShapeDtypeStruct((t, d // 2), U32),
            jax.ShapeDtypeStruct((nt, TOP_K, tm), I32),
            jax.ShapeDtypeStruct((nt, TOP_K, tm), F32),
            jax.ShapeDtypeStruct((nt, TOP_K, tm), I32),
            jax.ShapeDtypeStruct((n_exp, HEAD_DIM), F32),
        ],
        compiler_params=_params(1),
        name="outproj_route",
    )(merged, xa, xb, wout, gain, wr_t, tri)


def _issue_row_copies(idx_ref, n_rows, copy_for, per_trip=None):
    per = ROW_DMA_UNROLL

    def issue(a, carry):
        base = a * per
        parts = 1 if per_trip is None else ISSUE_PARTS
        for part in range(parts):
            if per_trip is not None:
                per_trip(a, part, parts)
            for j in range(part * per // parts, (part + 1) * per // parts):
                for k in range(TOP_K):
                    copy_for(k, a * (per // SUBLANES) + j // SUBLANES, j % SUBLANES,
                             idx_ref[k * n_rows + base + j]).start(priority=(j + k) % 2)
        return carry

    lax.fori_loop(0, n_rows // per, issue, 0)


def _dispatch_body(pad_end_ref, dest_ref, x_ref, xs_ref, zbuf_ref, buf_ref, zsem, ld_sem, sem, *, bm, n_exp):
    i, n_tiles = pl.program_id(0), pl.num_programs(0)
    tg = buf_ref.shape[1]
    tm = tg * SUBLANES

    @pl.when(i == 0)
    def _zero_padding():
        zbuf_ref[...] = jnp.zeros_like(zbuf_ref)

        def last_block_copy(e):
            end = pad_end_ref[e]
            start = pad_end_ref[e - 1] if e > 0 else 0
            cp = pltpu.make_async_copy(zbuf_ref, xs_ref.at[pl.ds(pl.multiple_of(end - bm, bm), bm)], zsem)
            return end > start, cp

        for e in range(n_exp):
            nonempty, cp = last_block_copy(e)
            pl.when(nonempty)(cp.start)
        for e in range(n_exp):
            nonempty, cp = last_block_copy(e)
            pl.when(nonempty)(cp.wait)

    ring = DISPATCH_RING
    slot = i % ring

    def load(tile, dst_slot):
        rows = pl.ds(pl.multiple_of(tile * tg, tg), tg)
        return pltpu.make_async_copy(x_ref.at[rows], buf_ref.at[dst_slot], ld_sem.at[dst_slot])

    def wait_scatters(of_slot):
        for k in range(TOP_K):
            pltpu.make_async_copy(buf_ref.at[of_slot], buf_ref.at[of_slot], sem.at[of_slot]).wait()

    pl.when(i == 0)(load(0, 0).start)
    pl.when(i >= ring - 1)(lambda: wait_scatters((i + 1) % ring))
    pl.when(i + 1 < n_tiles)(load(i + 1, (i + 1) % ring).start)
    load(i, slot).wait()

    def copy_for(k, g, s, d):
        return pltpu.make_async_copy(buf_ref.at[slot, g, pl.ds(s, 1), :], xs_ref.at[pl.ds(d, 1), :], sem.at[slot])

    _issue_row_copies(dest_ref, tm, copy_for)

    @pl.when(i == n_tiles - 1)
    def _drain():
        for back in range(ring - 2, -1, -1):
            pl.when(i >= back)(lambda back=back: wait_scatters((i - back) % ring))


def _dispatch(hn2p, dest, pad_end, *, tm, bm, n_rows):
    t, dh = hn2p.shape
    n_exp = pad_end.shape[0]
    body = functools.partial(_dispatch_body, bm=bm, n_exp=n_exp)
    return pl.pallas_call(
        body,
        grid_spec=pltpu.PrefetchScalarGridSpec(
            num_scalar_prefetch=1,
            grid=(t // tm,),
            in_specs=[
                pl.BlockSpec((TOP_K * tm,), lambda i, pe: (i,), memory_space=pltpu.SMEM),
                pl.BlockSpec(memory_space=pl.ANY),
            ],
            out_specs=pl.BlockSpec(memory_space=pl.ANY),
            scratch_shapes=[pltpu.VMEM((bm, dh), U32),
                            pltpu.VMEM((DISPATCH_RING, tm // SUBLANES, SUBLANES, dh), U32),
                            pltpu.SemaphoreType.DMA(()), pltpu.SemaphoreType.DMA((DISPATCH_RING,)),
                            pltpu.SemaphoreType.DMA((DISPATCH_RING,))],
        ),
        out_shape=jax.ShapeDtypeStruct((n_rows, dh), U32),
        compiler_params=_params(1),
        name="dispatch",
    )(pad_end, dest.reshape(-1), hn2p.reshape(t // SUBLANES, SUBLANES, dh))


def _ffn_body(bexp_ref, nvalid_ref, xs_ref, wg_ref, wu_ref, wd_ref, ys_ref):
    b = pl.program_id(0)

    @pl.when(b < nvalid_ref[0])
    def _():
        dh = xs_ref.shape[1]
        lo, hi = _unpack_bf16_pair(xs_ref[...])
        lo, hi = lo.astype(BF16), hi.astype(BF16)
        g = jnp.dot(lo, wg_ref[0, :dh, :], preferred_element_type=F32)
        g = g + jnp.dot(hi, wg_ref[0, dh:, :], preferred_element_type=F32)
        u = jnp.dot(lo, wu_ref[0, :dh, :], preferred_element_type=F32)
        u = u + jnp.dot(hi, wu_ref[0, dh:, :], preferred_element_type=F32)
        hid = (jax.nn.silu(g) * u).astype(BF16)
        y = jnp.dot(hid, wd_ref[0], preferred_element_type=F32)
        ys_ref[...] = _pack_bf16_pair(y[:, :dh], y[:, dh:])


def _expert_ffn(xs, wg, wu, wd, block_expert, n_valid, *, bm):
    n_rows, dh = xs.shape
    n_exp, d, de = wg.shape
    nb = n_rows // bm
    blk = lambda b, be, nv: (jnp.minimum(b, nv[0] - 1), 0)
    wmap = lambda b, be, nv: (be[jnp.minimum(b, nv[0] - 1)], 0, 0)
    return pl.pallas_call(
        _ffn_body,
        grid_spec=pltpu.PrefetchScalarGridSpec(
            num_scalar_prefetch=2,
            grid=(nb,),
            in_specs=[
                pl.BlockSpec((bm, dh), blk),
                pl.BlockSpec((1, d, de), wmap),
                pl.BlockSpec((1, d, de), wmap),
                pl.BlockSpec((1, de, d), wmap),
            ],
            out_specs=pl.BlockSpec((bm, dh), blk),
        ),
        out_shape=jax.ShapeDtypeStruct((n_rows, dh), U32),
        compiler_params=_params(1),
        name="expert_ffn",
    )(block_expert, n_valid, xs, wg, wu, wd)


def _combine_body(dest_ref, dest_next_ref, h_ref, cw_ref, gain_ref, ys_ref, o_ref, buf_ref, sem):
    i = pl.program_id(0)
    last = pl.num_programs(0) - 1
    tm, d = h_ref.shape
    dh = d // 2
    slot = i % 2
    per = ROW_DMA_UNROLL

    def gather_into(dst_slot):
        def copy_for(k, g, s, row):
            return pltpu.make_async_copy(ys_ref.at[pl.ds(row, 1), :], buf_ref.at[dst_slot, k, g, pl.ds(s, 1), :],
                                         sem.at[dst_slot])
        return copy_for

    @pl.when(i == 0)
    def _first_tile():
        _issue_row_copies(dest_ref, tm, gather_into(0))

    for k in range(TOP_K):
        pltpu.make_async_copy(buf_ref.at[slot, k], buf_ref.at[slot, k], sem.at[slot]).wait()

    def finish_rows(a, part=0, parts=1):
        groups = per // SUBLANES // parts
        for g in range(part * groups, (part + 1) * groups):
            grp = a * (per // SUBLANES) + g
            rows = pl.ds(pl.multiple_of(grp * SUBLANES, SUBLANES), SUBLANES)
            lo1, hi1 = _unpack_bf16_pair(buf_ref[slot, 0, grp])
            lo2, hi2 = _unpack_bf16_pair(buf_ref[slot, 1, grp])
            w1 = cw_ref[rows, 0:1]
            w2 = cw_ref[rows, 1:2]
            o_lo = h_ref[rows, :dh] + (w1 * lo1 + w2 * lo2)
            o_hi = h_ref[rows, dh:] + (w1 * hi1 + w2 * hi2)
            ssq = jnp.sum(o_lo * o_lo, axis=-1, keepdims=True) + jnp.sum(o_hi * o_hi, axis=-1, keepdims=True)
            r = lax.rsqrt(ssq / d + NORM_EPS)
            o_ref[rows, :dh] = o_lo * r * gain_ref[:, :dh]
            o_ref[rows, dh:] = o_hi * r * gain_ref[:, dh:]

    @pl.when(i < last)
    def _overlapped():
        _issue_row_copies(dest_next_ref, tm, gather_into(1 - slot), per_trip=finish_rows)

    @pl.when(i == last)
    def _tail():
        def trip(a, carry):
            finish_rows(a)
            return carry
        lax.fori_loop(0, tm // per, trip, 0)


def _combine(ys, dest, hmid, cw_t, gain, *, tm, row_off, n_rows):
    d = hmid.shape[1]
    dh = d // 2
    off, nt = row_off // tm, n_rows // tm
    dest_flat = dest.reshape(-1)
    return pl.pallas_call(
        _combine_body,
        grid=(nt,),
        in_specs=[
            pl.BlockSpec((TOP_K * tm,), lambda i: (off + i,), memory_space=pltpu.SMEM),
            pl.BlockSpec((TOP_K * tm,), lambda i: (off + jnp.minimum(i + 1, nt - 1),), memory_space=pltpu.SMEM),
            pl.BlockSpec((tm, d), lambda i: (off + i, 0)),
            pl.BlockSpec((tm, TOP_K), lambda i: (off + i, 0)),
            pl.BlockSpec((1, d), lambda i: (0, 0)),
            pl.BlockSpec(memory_space=pl.ANY),
        ],
        out_specs=pl.BlockSpec((tm, d), lambda i: (i, 0)),
        out_shape=jax.ShapeDtypeStruct((n_rows, d), F32),
        scratch_shapes=[pltpu.VMEM((2, TOP_K, tm // SUBLANES, SUBLANES, dh), U32), pltpu.SemaphoreType.DMA((2,))],
        compiler_params=_params(1),
        name="combine",
    )(dest_flat, dest_flat, hmid, cw_t, gain, ys)


def _rope_tables(n_tokens):
    rows = n_tokens // GRID_W
    row = jnp.repeat(jnp.arange(rows, dtype=F32), GRID_W)
    col = jnp.tile(jnp.arange(GRID_W, dtype=F32), rows)
    inv_freq = ROPE_THETA ** (-jnp.arange(ROPE_HALF, dtype=F32) / ROPE_HALF)
    ar, ac = row[:, None] * inv_freq, col[:, None] * inv_freq
    cos = jnp.concatenate([jnp.cos(ar), jnp.cos(ac), jnp.cos(ar), jnp.cos(ac)], axis=1)
    sin = jnp.concatenate([-jnp.sin(ar), -jnp.sin(ac), jnp.sin(ar), jnp.sin(ac)], axis=1)
    return cos, sin


def _tile(n, pref):
    return min(n, pref)


def kernel(x_prompt, x_sample, meta_tokens, mix_norm, w_in, q_gain, k_gain, w_attn_o, w_fourier_o, w_out,
           moe_norm, w_router_group, w_router_expert, w_expert_gate, w_expert_up, w_expert_down, final_norm):
    bp, n_p, d = x_prompt.shape
    bs, n_s, _ = x_sample.shape
    n_meta = meta_tokens.shape[0]
    t_p, t_s = bp * n_p, bs * n_s
    t = t_p + t_s
    f_dim = d // 2
    cg = f_dim // N_FOURIER_GROUPS
    n_groups = w_router_group.shape[-1]
    n_exp = w_router_expert.shape[-1]
    assert n_exp == n_groups * EXPERTS_PER_GROUP and n_groups <= 8

    tm = _tile(math.gcd(n_p, n_s), TOKEN_TILE)
    bm = _tile(t, EXPERT_BLOCK_ROWS)

    q_lo, v_lo = f_dim, f_dim + d + d // 4
    w0 = w_in[0]
    w_in_b = jnp.concatenate([w0[:, :q_lo], _pair_major(w0[:, q_lo:v_lo]), w0[:, v_lo:]], axis=1).astype(BF16)
    wa_b, wf_b, wo_b = w_attn_o[0].astype(BF16), w_fourier_o[0].astype(BF16), w_out[0].astype(BF16)
    qg = (_pair_major(q_gain[0]) * (HEAD_DIM ** -0.5 * LOG2E)).reshape(1, HEAD_DIM)
    kg = _pair_major(k_gain[0]).reshape(1, HEAD_DIM)
    wr_t = jnp.zeros((8 + n_exp, d), F32)
    wr_t = wr_t.at[:n_groups].set(w_router_group[0].T).at[8:].set(w_router_expert[0].T).astype(BF16)
    tri = (jnp.arange(tm)[:, None] < jnp.arange(tm)[None, :]).astype(BF16)
    cos, sin = _rope_tables(max(n_p, n_s))
    cdc, cds = _chan_dft(cg)
    gain_in = mix_norm[0].reshape(1, d)

    xp, xs_in = x_prompt.reshape(t_p, d), x_sample.reshape(t_s, d)
    tm_in = _tile(math.gcd(t_p, t_s), INPROJ_ROWS)
    proj = _inproj(xp, gain_in, w_in_b, None, tm=tm_in, row_off=0, t_total=t)
    uf, q, k, v, g = _inproj(xs_in, gain_in, w_in_b, proj, tm=tm_in, row_off=t_p, t_total=t)
    ufm, _, km, vm, _ = _inproj(meta_tokens, gain_in, w_in_b, None, tm=n_meta, row_off=0, t_total=n_meta)

    kinds = ((0, bp, n_p), (t_p, bs, n_s))
    n_kv = k.shape[1] // HEAD_DIM
    steps = [n_seq * n_kv * (n_tok // _tile(n_tok, ATTN_Q_TILE)) for _, n_seq, n_tok in kinds]
    w_exp = [w_expert_gate[0], w_expert_up[0], w_expert_down[0]]
    w_flat = [w.reshape(-1, w.shape[-1]) for w in w_exp]
    fused_cast = all(w.shape[0] % (SUBLANES * sum(steps)) == 0 for w in w_flat)
    cast_rows = [w.shape[0] // sum(steps) for w in w_flat] if fused_cast else []

    attn, z, w_b = None, None, None
    for (row_off, n_seq, n_tok), first_step in zip(kinds, (0, steps[0])):
        res = _attention(q, k, v, km, vm, qg, kg, cos, sin, attn, row_off=row_off, n_seq=n_seq,
                         n_tok=n_tok, tq=_tile(n_tok, ATTN_Q_TILE),
                         cast_src=w_flat if fused_cast else (), cast_prev=w_b, cast_rows=cast_rows,
                         cast_row_off=[first_step * r for r in cast_rows])
        attn, w_b = res[0], (res[1:] if fused_cast else None)
        z = _fourier(uf, ufm, _seq_dft(n_tok, n_meta), cdc, cds, z, row_off=row_off, n_seq=n_seq,
                     n_tok=n_tok, tl=_tile(n_tok, max(TOKEN_TILE, FOURIER_TILE_ELEMS // n_tok)),
                     tf=_tile(f_dim, FOURIER_COLS))
    if fused_cast:
        wg_b, wu_b, wd_b = [wb.reshape(w.shape) for wb, w in zip(w_b, w_exp)]
    else:
        wg_b, wu_b, wd_b = [w.astype(BF16) for w in w_exp]

    merged = _merge(attn, z, g, wa_b, wf_b, tm=tm)
    hmid, hn2p, eid, cw, rank, counts = _outproj_route(
        merged, xp, xs_in, wo_b, moe_norm[0].reshape(1, d), wr_t, tri, tm=tm, n_groups=n_groups, n_exp=n_exp)

    cnt = counts[:, 0].astype(I32)
    padded = (cnt + bm - 1) // bm * bm
    pad_end = jnp.cumsum(padded).astype(I32)
    pad_start = pad_end - padded
    onehot = eid[..., None] == jnp.arange(n_exp, dtype=I32)
    dest = jnp.sum(jnp.where(onehot, pad_start, 0), axis=-1) + rank
    nb = (t * TOP_K) // bm + n_exp
    n_valid = (pad_end[-1:] // bm).astype(I32)
    blk_start = jnp.arange(nb, dtype=I32) * bm
    block_expert = jnp.minimum(jnp.sum(pad_end[None, :] <= blk_start[:, None], axis=1), n_exp - 1).astype(I32)

    xs = _dispatch(hn2p, dest, pad_end, tm=tm, bm=bm, n_rows=nb * bm)
    ys = _expert_ffn(xs, wg_b, wu_b, wd_b, block_expert, n_valid, bm=bm)
    cw_t = jnp.transpose(cw, (0, 2, 1)).reshape(t, TOP_K)
    fgain = final_norm.reshape(1, d)
    y_p = _combine(ys, dest, hmid, cw_t, fgain, tm=tm, row_off=0, n_rows=t_p)
    y_s = _combine(ys, dest, hmid, cw_t, fgain, tm=tm, row_off=t_p, n_rows=t_s)
    return y_p.reshape(bp, n_p, d), y_s.reshape(bs, n_s, d)
```
